```python
import math
import jax, jax.numpy as jnp
from jax import lax
import numpy as np

D_MODEL = 1024
BATCH = 16
SEQ = 2048
DEPTH = 1

CTX_LEN = 256
GRID_W = 64
HEAD_DIM = 64
ROPE_BASE = 10000.0
LN_EPS = 1e-6
RET_HEADS = 4
RET_DK = 64
RET_DV = 128
RET_CHUNK = 128
ATT_HEADS = 8
ATT_KV_HEADS = 2
ATT_GROUP = ATT_HEADS // ATT_KV_HEADS
Q_BLOCK = 128
RET_Q_W = RET_HEADS * RET_DK
RET_K_W = RET_HEADS * RET_DK
RET_V_W = RET_HEADS * RET_DV
RET_G_W = RET_HEADS * RET_DV
ATT_Q_W = ATT_HEADS * HEAD_DIM
ATT_K_W = ATT_KV_HEADS * HEAD_DIM
ATT_V_W = ATT_KV_HEADS * HEAD_DIM
IN_WIDTHS = (RET_Q_W, RET_K_W, RET_V_W, RET_G_W, ATT_Q_W, ATT_K_W, ATT_V_W)
IN_WIDTH = sum(IN_WIDTHS)
IN_SPLITS = [sum(IN_WIDTHS[:i + 1]) for i in range(len(IN_WIDTHS) - 1)]
MIX_WIDTH = RET_V_W + ATT_Q_W
N_EXPERTS = 32
TOP_K = 4
D_FF = D_MODEL
SWIGLU_LIMIT = 7.0
SWIGLU_ALPHA = 1.702
MOE_BLOCK = 512
DN_ALPHA = (2.0 * DEPTH) ** 0.25
DN_BETA = (8.0 * DEPTH) ** -0.25

kernel_name = "hybrid_retention_gqa_moe_diffusion_block"


def layer_norm(x, g, b):
    xf = x.astype(jnp.float32)
    mu = jnp.mean(xf, -1, keepdims=True)
    var = jnp.mean(jnp.square(xf - mu), -1, keepdims=True)
    return ((xf - mu) * lax.rsqrt(var + LN_EPS) * g + b).astype(x.dtype)


def rms_norm(x, g):
    xf = x.astype(jnp.float32)
    return (xf * lax.rsqrt(jnp.mean(xf * xf, -1, keepdims=True) + LN_EPS) * g).astype(x.dtype)


def axial_rope_tables(rows, cols, dim):
    n_freq = dim // 4
    inv = ROPE_BASE ** (-jnp.arange(n_freq, dtype=jnp.float32) / n_freq)
    ang_r = rows.astype(jnp.float32)[:, None] * inv
    ang_c = cols.astype(jnp.float32)[:, None] * inv
    ang = jnp.concatenate([ang_r, ang_r, ang_c, ang_c], -1)
    return jnp.cos(ang), jnp.sin(ang)


def apply_rope(x, cos, sin):
    x1, x2, x3, x4 = jnp.split(x, 4, axis=-1)
    rotated = jnp.concatenate([-x2, x1, -x4, x3], -1)
    return (x * cos[None, :, None, :] + rotated * sin[None, :, None, :]).astype(x.dtype)


def ret_heads(rq, rk, rv, cos=None, sin=None):
    B, T, _ = rq.shape
    q = rq.reshape(B, T, RET_HEADS, RET_DK)
    k = rk.reshape(B, T, RET_HEADS, RET_DK)
    v = rv.reshape(B, T, RET_HEADS, RET_DV)
    if cos is not None:
        q = apply_rope(q, cos, sin)
        k = apply_rope(k, cos, sin)
    k = k * (RET_DK ** -0.5)
    to_bhtd = lambda t: jnp.transpose(t, (0, 2, 1, 3)).astype(jnp.float32)
    return to_bhtd(q), to_bhtd(k), to_bhtd(v)


def retention_scan(q, k, v, log_gamma, s0):
    B, H, T, dk = q.shape
    dv = v.shape[-1]
    n = T // RET_CHUNK
    qc = q.reshape(B, H, n, RET_CHUNK, dk)
    kc = k.reshape(B, H, n, RET_CHUNK, dk)
    vc = v.reshape(B, H, n, RET_CHUNK, dv)
    lg = log_gamma.astype(jnp.float32)
    pos = jnp.arange(RET_CHUNK, dtype=jnp.float32)
    diff = pos[:, None] - pos[None, :]
    decay = jnp.where(diff >= 0, jnp.exp(lg[:, None, None] * jnp.maximum(diff, 0.0)), 0.0)
    scores = jnp.einsum('bhncd,bhnjd->bhncj', qc, kc) * decay[None, :, None]
    y_inner = jnp.einsum('bhncj,bhnje->bhnce', scores, vc)
    k_decay = jnp.exp(lg[:, None] * (RET_CHUNK - 1 - pos)[None, :])
    kv_chunk = jnp.einsum('bhncd,hc,bhnce->nbhde', kc, k_decay, vc)
    gamma_c = jnp.exp(lg * RET_CHUNK)[None, :, None, None]

    def step(state, kv):
        return gamma_c * state + kv, state

    s_final, s_prev = lax.scan(step, s0, kv_chunk)
    q_decay = jnp.exp(lg[:, None] * (pos + 1.0)[None, :])
    y_cross = jnp.einsum('bhncd,hc,nbhde->bhnce', qc, q_decay, s_prev)
    return (y_inner + y_cross).reshape(B, H, T, dv), s_final


def retention_output(y, gate, gn_g, gn_b):
    B, H, T, dv = y.shape
    yt = jnp.transpose(y, (0, 2, 1, 3))
    mu = jnp.mean(yt, -1, keepdims=True)
    var = jnp.mean(jnp.square(yt - mu), -1, keepdims=True)
    yn = ((yt - mu) * lax.rsqrt(var + LN_EPS)).reshape(B, T, H * dv) * gn_g + gn_b
    return (yn * jax.nn.silu(gate.astype(jnp.float32))).astype(gate.dtype)


def block_attention(q, k, v):
    B, Tq, _, dh = q.shape
    nb = Tq // Q_BLOCK
    qb = q.reshape(B, nb, Q_BLOCK, ATT_KV_HEADS, ATT_GROUP, dh).transpose(1, 0, 2, 3, 4, 5)
    scale = dh ** -0.5

    def one_block(qblk):
        s = jnp.einsum('bqhgd,bkhd->bhgqk', qblk, k).astype(jnp.float32) * scale
        p = jax.nn.softmax(s, axis=-1).astype(v.dtype)
        return jnp.einsum('bhgqk,bkhd->bqhgd', p, v)

    o = lax.map(one_block, qb)
    return o.transpose(1, 0, 2, 3, 4, 5).reshape(B, Tq, ATT_HEADS * dh)


def mixer(h, hc, w_in_l, lg_f, lg_b, gn_g, gn_b, qn_g, kn_g, w_o_l, cos, sin, with_ctx_out):
    B, T, _ = h.shape
    L = hc.shape[1]
    rq, rk, rv, rg, aq, ak, av = jnp.split(h @ w_in_l, IN_SPLITS, axis=-1)
    rqc, rkc, rvc, rgc, aqc, akc, avc = jnp.split(hc @ w_in_l, IN_SPLITS, axis=-1)
    flip = lambda t: t[:, :, ::-1]

    q, k, v = ret_heads(rq, rk, rv, cos, sin)
    qc, kc, vc = ret_heads(rqc, rkc, rvc)
    s0 = jnp.zeros((B, RET_HEADS, RET_DK, RET_DV), jnp.float32)
    yc_f, s_ctx_f = retention_scan(qc, kc, vc, lg_f, s0)
    yc_b, s_ctx_b = retention_scan(flip(qc), flip(kc), flip(vc), lg_b, s0)
    y_f, _ = retention_scan(q, k, v, lg_f, s_ctx_f)
    y_b, _ = retention_scan(flip(q), flip(k), flip(v), lg_b, s_ctx_b)
    ret_out = retention_output(y_f + flip(y_b), rg, gn_g, gn_b)

    aq = apply_rope(rms_norm(aq.reshape(B, T, ATT_HEADS, HEAD_DIM), qn_g), cos, sin)
    ak = apply_rope(rms_norm(ak.reshape(B, T, ATT_KV_HEADS, HEAD_DIM), kn_g), cos, sin)
    av = av.reshape(B, T, ATT_KV_HEADS, HEAD_DIM)
    aqc = rms_norm(aqc.reshape(B, L, ATT_HEADS, HEAD_DIM), qn_g)
    akc = rms_norm(akc.reshape(B, L, ATT_KV_HEADS, HEAD_DIM), kn_g)
    avc = avc.reshape(B, L, ATT_KV_HEADS, HEAD_DIM)
    k_all = jnp.concatenate([akc, ak], axis=1)
    v_all = jnp.concatenate([avc, av], axis=1)
    att_out = block_attention(aq, k_all, v_all)

    y = jnp.concatenate([ret_out, att_out], -1) @ w_o_l
    yc = None
    if with_ctx_out:
        ret_c = retention_output(yc_f + flip(yc_b), rgc, gn_g, gn_b)
        att_c = block_attention(aqc, akc, avc)
        yc = jnp.concatenate([ret_c, att_c], -1) @ w_o_l
    return y, yc


def moe_ffn(h, router_w, router_b, w_up, b_up, w_down, b_down):
    n_tok, d = h.shape
    logits = (h @ router_w + router_b).astype(jnp.float32)
    top_logit, top_idx = lax.top_k(logits, TOP_K)
    gates = jax.nn.softmax(top_logit, axis=-1).astype(h.dtype)
    n_assign = n_tok * TOP_K
    flat_e = top_idx.reshape(-1)
    flat_tok = jnp.repeat(jnp.arange(n_tok, dtype=jnp.int32), TOP_K)
    order = jnp.argsort(flat_e)
    e_sorted = flat_e[order]
    tok_sorted = flat_tok[order]
    gate_sorted = gates.reshape(-1)[order]
    counts = jnp.bincount(flat_e, length=N_EXPERTS)
    padded = (counts + MOE_BLOCK - 1) // MOE_BLOCK * MOE_BLOCK
    start = jnp.cumsum(counts) - counts
    pad_end = jnp.cumsum(padded)
    pad_start = pad_end - padded
    dest = pad_start[e_sorted] + (jnp.arange(n_assign, dtype=jnp.int32) - start[e_sorted])
    n_blocks = -(-n_assign // MOE_BLOCK) + N_EXPERTS
    slot_tok = jnp.full((n_blocks * MOE_BLOCK,), n_tok, jnp.int32).at[dest].set(tok_sorted)
    block_start = jnp.arange(n_blocks, dtype=jnp.int32) * MOE_BLOCK
    block_expert = jnp.minimum(jnp.searchsorted(pad_end, block_start, side='right'), N_EXPERTS - 1)
    h_pad = jnp.concatenate([h, jnp.zeros((1, d), h.dtype)], axis=0)

    def expert_block(args):
        toks, e = args
        u = h_pad[toks] @ w_up[e] + b_up[e]
        glu, lin = jnp.split(u, 2, axis=-1)
        glu = jnp.minimum(glu, SWIGLU_LIMIT)
        lin = jnp.clip(lin, -SWIGLU_LIMIT, SWIGLU_LIMIT)
        act = glu * jax.nn.sigmoid(SWIGLU_ALPHA * glu) * (lin + 1.0)
        return act @ w_down[e] + b_down[e]

    y = lax.map(expert_block, (slot_tok.reshape(n_blocks, MOE_BLOCK), block_expert))
    y_assign = y.reshape(-1, d)[dest] * gate_sorted[:, None]
    return jax.ops.segment_sum(y_assign, tok_sorted, num_segments=n_tok)


def setup_inputs(seed: int = 0) -> dict:
    key = jax.random.key(seed)
    ks = jax.random.split(key, 32)
    nrm = lambda k, shape, scale: jax.random.normal(k, shape, jnp.float32) * scale
    L, D = DEPTH, D_MODEL
    in_scales = (1.0, 1.0, DN_BETA, 1.0, 1.0, 1.0, DN_BETA)
    col_scale = jnp.concatenate([jnp.full((w,), s, jnp.float32) for w, s in zip(IN_WIDTHS, in_scales)])
    base_decay = jnp.log(1.0 - 2.0 ** (-5.0 - jnp.arange(RET_HEADS, dtype=jnp.float32)))
    return {
        "x": nrm(ks[0], (BATCH, SEQ, D), 1.0),
        "c": nrm(ks[1], (BATCH, D), 1.0),
        "ctx": nrm(ks[2], (BATCH, CTX_LEN, D), 1.0),
        "c_ctx": nrm(ks[3], (D,), 1.0),
        "ln_in_g": 1.0 + nrm(ks[4], (D,), 0.02),
        "ln_in_b": nrm(ks[5], (D,), 0.02),
        "w_ada": nrm(ks[6], (L, D, 6 * D), D ** -0.5),
        "b_ada": nrm(ks[7], (L, 6 * D), 0.02),
        "w_in": nrm(ks[8], (L, D, IN_WIDTH), D ** -0.5) * col_scale,
        "ret_log_decay_f": base_decay * (1.0 + nrm(ks[9], (L, RET_HEADS), 0.05)),
        "ret_log_decay_b": base_decay * (1.0 + nrm(ks[10], (L, RET_HEADS), 0.05)),
        "ret_gn_g": 1.0 + nrm(ks[11], (L, RET_V_W), 0.02),
        "ret_gn_b": nrm(ks[12], (L, RET_V_W), 0.02),
        "q_norm_g": 1.0 + nrm(ks[13], (L, HEAD_DIM), 0.02),
        "k_norm_g": 1.0 + nrm(ks[14], (L, HEAD_DIM), 0.02),
        "w_o": nrm(ks[15], (L, MIX_WIDTH, D), MIX_WIDTH ** -0.5 * DN_BETA),
        "ln1_g": 1.0 + nrm(ks[16], (L, D), 0.02),
        "ln1_b": nrm(ks[17], (L, D), 0.02),
        "router_w": nrm(ks[18], (L, D, N_EXPERTS), D ** -0.5),
        "router_b": nrm(ks[19], (L, N_EXPERTS), 0.01),
        "w_up": nrm(ks[20], (L, N_EXPERTS, D, 2 * D_FF), D ** -0.5 * DN_BETA),
        "b_up": nrm(ks[21], (L, N_EXPERTS, 2 * D_FF), 0.01),
        "w_down": nrm(ks[22], (L, N_EXPERTS, D_FF, D), D_FF ** -0.5 * DN_BETA),
        "b_down": nrm(ks[23], (L, N_EXPERTS, D), 0.01),
        "ln2_g": 1.0 + nrm(ks[24], (L, D), 0.02),
        "ln2_b": nrm(ks[25], (L, D), 0.02),
    }


def reference(x, c, ctx, c_ctx, ln_in_g, ln_in_b, w_ada, b_ada, w_in, ret_log_decay_f, ret_log_decay_b,
              ret_gn_g, ret_gn_b, q_norm_g, k_norm_g, w_o, ln1_g, ln1_b, router_w, router_b,
              w_up, b_up, w_down, b_down, ln2_g, ln2_b):
    B, T, D = x.shape
    ROWS = T // GRID_W
    rows = jnp.repeat(jnp.arange(ROWS, dtype=jnp.int32), GRID_W)
    cols = jnp.tile(jnp.arange(GRID_W, dtype=jnp.int32), ROWS)
    cos, sin = axial_rope_tables(rows, cols, HEAD_DIM)

    x = layer_norm(x, ln_in_g, ln_in_b)
    ctx = layer_norm(ctx, ln_in_g, ln_in_b)
    for l in range(DEPTH):
        update_ctx = l < DEPTH - 1
        sh1, sc1, g1, sh2, sc2, g2 = jnp.split(jax.nn.silu(c) @ w_ada[l] + b_ada[l], 6, axis=-1)
        csh1, csc1, cg1, csh2, csc2, cg2 = jnp.split(jax.nn.silu(c_ctx) @ w_ada[l] + b_ada[l], 6, axis=-1)

        h = x * (1.0 + sc1[:, None]) + sh1[:, None]
        hc = ctx * (1.0 + csc1) + csh1
        y, yc = mixer(h, hc, w_in[l], ret_log_decay_f[l], ret_log_decay_b[l], ret_gn_g[l], ret_gn_b[l],
                      q_norm_g[l], k_norm_g[l], w_o[l], cos, sin, update_ctx)
        x = layer_norm(DN_ALPHA * x + g1[:, None] * y, ln1_g[l], ln1_b[l])
        if update_ctx:
            ctx = layer_norm(DN_ALPHA * ctx + cg1 * yc, ln1_g[l], ln1_b[l])

        h2 = x * (1.0 + sc2[:, None]) + sh2[:, None]
        f = moe_ffn(h2.reshape(B * T, D), router_w[l], router_b[l], w_up[l], b_up[l], w_down[l], b_down[l])
        x = layer_norm(DN_ALPHA * x + g2[:, None] * f.reshape(B, T, D), ln2_g[l], ln2_b[l])
        if update_ctx:
            hc2 = ctx * (1.0 + csc2) + csh2
            Lc = ctx.shape[1]
            fc = moe_ffn(hc2.reshape(B * Lc, D), router_w[l], router_b[l], w_up[l], b_up[l], w_down[l], b_down[l])
            ctx = layer_norm(DN_ALPHA * ctx + cg2 * fc.reshape(B, Lc, D), ln2_g[l], ln2_b[l])
    return x
```

```python
import functools

import jax
import jax.numpy as jnp
from jax import lax
from jax.experimental import pallas as pl
from jax.experimental.pallas import tpu as pltpu

GRID_W = 64
HEAD_DIM = 64
ROPE_BASE = 10000.0
LN_EPS = 1e-6
RET_HEADS = 4
RET_DK = 64
RET_DV = 128
RET_CHUNK = 128
ATT_HEADS = 8
ATT_KV_HEADS = 2
ATT_GROUP = ATT_HEADS // ATT_KV_HEADS
N_EXPERTS = 32
TOP_K = 4
SWIGLU_LIMIT = 7.0
SWIGLU_ALPHA = 1.702
DEPTH = 1
DN_ALPHA = (2.0 * DEPTH) ** 0.25

RET_Q_W = RET_HEADS * RET_DK
RET_V_W = RET_HEADS * RET_DV
ATT_Q_W = ATT_HEADS * HEAD_DIM
ATT_K_W = ATT_KV_HEADS * HEAD_DIM

LANES = 128
VMEM_LIMIT_BYTES = 56 * 1024 * 1024

TOKEN_TILE = 512
ATT_Q_TILE = 256
EXPERT_ROWS = 512
COMBINE_TILE = 256

F32 = jnp.float32
BF16 = jnp.bfloat16


def _cparams(*sem):
    return pltpu.CompilerParams(dimension_semantics=sem, vmem_limit_bytes=VMEM_LIMIT_BYTES)


def _layer_norm(x, g, b):
    mu = jnp.mean(x, -1, keepdims=True)
    xc = x - mu
    var = jnp.mean(xc * xc, -1, keepdims=True)
    return xc * lax.rsqrt(var + LN_EPS) * g + b


def _silu(x):
    return x * (1.0 / (1.0 + jnp.exp(-x)))


def _ada_kernel(c_ref, w_ref, b_ref, o_ref):
    a = _silu(c_ref[...])
    o_ref[...] = jnp.dot(a, w_ref[...], preferred_element_type=F32,
                         precision=lax.Precision.HIGHEST) + b_ref[...]


def _ada_mod(cc, w, b):
    m, d = cc.shape
    n = w.shape[1]
    tn = 1536
    return pl.pallas_call(
        _ada_kernel,
        grid=(n // tn,),
        in_specs=[pl.BlockSpec((m, d), lambda j: (0, 0)),
                  pl.BlockSpec((d, tn), lambda j: (0, j)),
                  pl.BlockSpec((1, tn), lambda j: (0, j))],
        out_specs=pl.BlockSpec((m, tn), lambda j: (0, j)),
        out_shape=jax.ShapeDtypeStruct((m, n), F32),
        compiler_params=_cparams("arbitrary"),
        name="ada_mod",
    )(cc, w, b)


def _rope(x, cos, sinn, sinp):
    outs = []
    for j in range(x.shape[1] // LANES):
        xg = x[:, j * LANES:(j + 1) * LANES]
        outs.append(xg * cos + pltpu.roll(xg, LANES - 16, 1) * sinn + pltpu.roll(xg, 16, 1) * sinp)
    return outs[0] if len(outs) == 1 else jnp.concatenate(outs, axis=1)


def _head_rms(x, seg, gain):
    outs = []
    for j in range(x.shape[1] // LANES):
        xg = x[:, j * LANES:(j + 1) * LANES]
        sq = xg * xg
        hi = sq.astype(BF16)
        lo = (sq - hi.astype(F32)).astype(BF16)
        ms = (jnp.dot(hi, seg, preferred_element_type=F32)
              + jnp.dot(lo, seg, preferred_element_type=F32)) * (1.0 / HEAD_DIM)
        outs.append(xg * lax.rsqrt(ms + LN_EPS) * gain[:, j * LANES:(j + 1) * LANES])
    return outs[0] if len(outs) == 1 else jnp.concatenate(outs, axis=1)


def _in_proj_kernel(*refs, rope):
    if rope:
        (x_ref, mod_ref, lng_ref, lnb_ref, w_ref, seg_ref, qg_ref, kg_ref, cos_ref, sinn_ref, sinp_ref,
         rq_ref, rk_ref, rv_ref, rg_ref, aq_ref, akt_ref, av_ref) = refs
    else:
        (x_ref, mod_ref, lng_ref, lnb_ref, w_ref, seg_ref, qg_ref, kg_ref,
         rq_ref, rk_ref, rv_ref, rg_ref, aq_ref, akt_ref, av_ref) = refs
    x = x_ref[0]
    xn = _layer_norm(x, lng_ref[...], lnb_ref[...])
    h = xn * (1.0 + mod_ref[0, 1:2, :]) + mod_ref[0, 0:1, :]
    u = jnp.dot(h.astype(BF16), w_ref[...], preferred_element_type=F32)
    o = 0
    rq = u[:, o:o + RET_Q_W]; o += RET_Q_W
    rk = u[:, o:o + RET_Q_W]; o += RET_Q_W
    rv = u[:, o:o + RET_V_W]; o += RET_V_W
    rg = u[:, o:o + RET_V_W]; o += RET_V_W
    aq = u[:, o:o + ATT_Q_W]; o += ATT_Q_W
    ak = u[:, o:o + ATT_K_W]; o += ATT_K_W
    av = u[:, o:o + ATT_K_W]
    seg = seg_ref[...]
    aq = _head_rms(aq, seg, qg_ref[...])
    ak = _head_rms(ak, seg, kg_ref[...])
    if rope:
        cos, sinn, sinp = cos_ref[...], sinn_ref[...], sinp_ref[...]
        rq = _rope(rq, cos, sinn, sinp)
        rk = _rope(rk, cos, sinn, sinp)
        aq = _rope(aq, cos, sinn, sinp)
        ak = _rope(ak, cos, sinn, sinp)
    rq_ref[0] = rq
    rk_ref[0] = rk * (RET_DK ** -0.5)
    rv_ref[0] = rv.astype(BF16)
    rg_ref[0] = rg.astype(BF16)
    aq_ref[0] = (aq * (HEAD_DIM ** -0.5)).astype(BF16)
    akt_ref[0] = ak.T.astype(BF16)
    av_ref[0] = av.astype(BF16)


def _in_proj(x, mod, lng, lnb, w_bf, seg, qg, kg, tables):
    b, t, d = x.shape
    tt = min(TOKEN_TILE, t)
    rope = tables is not None
    mod_b = mod.shape[0]
    mod_map = (lambda bi, i: (bi, 0, 0)) if mod_b > 1 else (lambda bi, i: (0, 0, 0))
    const2 = lambda bi, i: (0, 0)
    tok3 = lambda bi, i: (bi, i, 0)
    in_specs = [pl.BlockSpec((1, tt, d), tok3),
                pl.BlockSpec((1, 6, d), mod_map),
                pl.BlockSpec((1, d), const2),
                pl.BlockSpec((1, d), const2),
                pl.BlockSpec(w_bf.shape, const2),
                pl.BlockSpec(seg.shape, const2),
                pl.BlockSpec(qg.shape, const2),
                pl.BlockSpec(kg.shape, const2)]
    args = [x, mod, lng, lnb, w_bf, seg, qg, kg]
    if rope:
        in_specs += [pl.BlockSpec((tt, LANES), lambda bi, i: (i, 0))] * 3
        args += list(tables)
    out_shape = [jax.ShapeDtypeStruct((b, t, RET_Q_W), F32),
                 jax.ShapeDtypeStruct((b, t, RET_Q_W), F32),
                 jax.ShapeDtypeStruct((b, t, RET_V_W), BF16),
                 jax.ShapeDtypeStruct((b, t, RET_V_W), BF16),
                 jax.ShapeDtypeStruct((b, t, ATT_Q_W), BF16),
                 jax.ShapeDtypeStruct((b, ATT_K_W, t), BF16),
                 jax.ShapeDtypeStruct((b, t, ATT_K_W), BF16)]
    out_specs = [pl.BlockSpec((1, tt, RET_Q_W), tok3),
                 pl.BlockSpec((1, tt, RET_Q_W), tok3),
                 pl.BlockSpec((1, tt, RET_V_W), tok3),
                 pl.BlockSpec((1, tt, RET_V_W), tok3),
                 pl.BlockSpec((1, tt, ATT_Q_W), tok3),
                 pl.BlockSpec((1, ATT_K_W, tt), lambda bi, i: (bi, 0, i)),
                 pl.BlockSpec((1, tt, ATT_K_W), tok3)]
    return pl.pallas_call(
        functools.partial(_in_proj_kernel, rope=rope),
        grid=(b, t // tt),
        in_specs=in_specs,
        out_specs=out_specs,
        out_shape=out_shape,
        compiler_params=_cparams("parallel", "parallel"),
        name="in_proj_rope" if rope else "in_proj_ctx",
    )(*args)


def _retention_kernel(lgf_ref, lgb_ref, q_ref, k_ref, v_ref, g_ref, kc_ref, vc_ref, gng_ref, gnb_ref,
                      o_ref, kvf_ref, kvb_ref, sf_ref, sb_ref):
    pair = pl.program_id(1)
    t = q_ref.shape[1]
    n_chunks = t // RET_CHUNK
    ctx_len = kc_ref.shape[1]
    c = RET_CHUNK
    pos = lax.broadcasted_iota(jnp.int32, (c, 1), 0).astype(F32)
    row = lax.broadcasted_iota(jnp.int32, (c, c), 0)
    col = lax.broadcasted_iota(jnp.int32, (c, c), 1)
    diff = (row - col).astype(F32)
    cpos = lax.broadcasted_iota(jnp.int32, (ctx_len, 1), 0).astype(F32)
    tn = (((0,), (0,)), ((), ()))
    nt = (((1,), (1,)), ((), ()))

    for hh in range(2):
        head = pair * 2 + hh
        lgf = lgf_ref[0, head]
        lgb = lgb_ref[0, head]
        ql = slice(hh * RET_DK, (hh + 1) * RET_DK)
        vl = slice(hh * RET_DV, (hh + 1) * RET_DV)
        kdf = jnp.exp(lgf * (c - 1.0 - pos))
        kdb = jnp.exp(lgb * pos)
        qdf = jnp.exp(lgf * (pos + 1.0))
        qdb = jnp.exp(lgb * (c - pos))
        decay = (jnp.where(diff >= 0, jnp.exp(lgf * jnp.maximum(diff, 0.0)), 0.0)
                 + jnp.where(diff <= 0, jnp.exp(lgb * jnp.maximum(-diff, 0.0)), 0.0))
        gfc = jnp.exp(lgf * c)
        gbc = jnp.exp(lgb * c)

        kctx = kc_ref[0, :, ql]
        vctx = vc_ref[0, :, vl]
        s_f = lax.dot_general((kctx * jnp.exp(lgf * (ctx_len - 1.0 - cpos))).astype(BF16), vctx, tn,
                              preferred_element_type=F32)
        s_b = lax.dot_general((kctx * jnp.exp(lgb * cpos)).astype(BF16), vctx, tn,
                              preferred_element_type=F32)

        def kv_body(n, carry):
            r = pl.ds(pl.multiple_of(n * c, c), c)
            kc = k_ref[0, r, ql]
            vc = v_ref[0, r, vl]
            kvf_ref[n] = lax.dot_general((kc * kdf).astype(BF16), vc, tn, preferred_element_type=F32)
            kvb_ref[n] = lax.dot_general((kc * kdb).astype(BF16), vc, tn, preferred_element_type=F32)
            return carry

        lax.fori_loop(0, n_chunks, kv_body, 0)

        for n in range(n_chunks):
            sf_ref[n] = s_f.astype(BF16)
            s_f = gfc * s_f + kvf_ref[n]
        for n in range(n_chunks - 1, -1, -1):
            sb_ref[n] = s_b.astype(BF16)
            s_b = gbc * s_b + kvb_ref[n]

        gng = gng_ref[:, vl]
        gnb = gnb_ref[:, vl]

        def out_body(n, carry):
            r = pl.ds(pl.multiple_of(n * c, c), c)
            qc = q_ref[0, r, ql]
            kc = k_ref[0, r, ql]
            vc = v_ref[0, r, vl]
            s = lax.dot_general(qc.astype(BF16), kc.astype(BF16), nt, preferred_element_type=F32)
            y = jnp.dot((s * decay).astype(BF16), vc, preferred_element_type=F32)
            y += jnp.dot((qc * qdf).astype(BF16), sf_ref[n], preferred_element_type=F32)
            y += jnp.dot((qc * qdb).astype(BF16), sb_ref[n], preferred_element_type=F32)
            mu = jnp.mean(y, -1, keepdims=True)
            yc = y - mu
            var = jnp.mean(yc * yc, -1, keepdims=True)
            yn = yc * lax.rsqrt(var + LN_EPS) * gng + gnb
            o_ref[0, r, vl] = (yn * _silu(g_ref[0, r, vl].astype(F32))).astype(o_ref.dtype)
            return carry

        lax.fori_loop(0, n_chunks, out_body, 0)


def _retention(lgf, lgb, rq, rk, rv, rg, rkc, rvc, gng, gnb):
    b, t, _ = rq.shape
    ctx_len = rkc.shape[1]
    n_chunks = t // RET_CHUNK
    pair3 = lambda bi, p: (bi, 0, p)
    smem = pl.BlockSpec(memory_space=pltpu.SMEM)
    return pl.pallas_call(
        _retention_kernel,
        grid=(b, RET_HEADS // 2),
        in_specs=[smem, smem,
                  pl.BlockSpec((1, t, 2 * RET_DK), pair3),
                  pl.BlockSpec((1, t, 2 * RET_DK), pair3),
                  pl.BlockSpec((1, t, 2 * RET_DV), pair3),
                  pl.BlockSpec((1, t, 2 * RET_DV), pair3),
                  pl.BlockSpec((1, ctx_len, 2 * RET_DK), pair3),
                  pl.BlockSpec((1, ctx_len, 2 * RET_DV), pair3),
                  pl.BlockSpec((1, 2 * RET_DV), lambda bi, p: (0, p)),
                  pl.BlockSpec((1, 2 * RET_DV), lambda bi, p: (0, p))],
        out_specs=pl.BlockSpec((1, t, 2 * RET_DV), pair3),
        out_shape=jax.ShapeDtypeStruct((b, t, RET_V_W), BF16),
        scratch_shapes=[pltpu.VMEM((n_chunks, RET_DK, RET_DV), F32),
                        pltpu.VMEM((n_chunks, RET_DK, RET_DV), F32),
                        pltpu.VMEM((n_chunks, RET_DK, RET_DV), BF16),
                        pltpu.VMEM((n_chunks, RET_DK, RET_DV), BF16)],
        compiler_params=_cparams("parallel", "parallel"),
        name="retention",
    )(lgf, lgb, rq, rk, rv, rg, rkc, rvc, gng, gnb)


def _attention_kernel(q_ref, kt_ref, v_ref, o_ref):
    tq = q_ref.shape[1]
    lane = lax.broadcasted_iota(jnp.int32, (tq, LANES), 1)
    v = v_ref[0]
    for pair in range(ATT_HEADS // 2):
        halves = []
        for sub in range(2):
            h = pair * 2 + sub
            kvh = h // ATT_GROUP
            kt = kt_ref[0, kvh * HEAD_DIM:(kvh + 1) * HEAD_DIM, :]
            q = q_ref[0, :, h * HEAD_DIM:(h + 1) * HEAD_DIM]
            s = jnp.dot(q, kt, preferred_element_type=F32)
            m = jnp.max(s, -1, keepdims=True)
            p = jnp.exp(s - m)
            l = jnp.sum(p, -1, keepdims=True)
            o = jnp.dot(p.astype(BF16), v, preferred_element_type=F32) / l
            if kvh != sub:
                o = pltpu.roll(o, HEAD_DIM, 1)
            halves.append(o)
        o_ref[0, :, pair * LANES:(pair + 1) * LANES] = jnp.where(
            lane < HEAD_DIM, halves[0], halves[1]).astype(o_ref.dtype)


def _attention(aq, akt, av):
    b, t, _ = aq.shape
    tk = akt.shape[2]
    tq = ATT_Q_TILE
    return pl.pallas_call(
        _attention_kernel,
        grid=(b, t // tq),
        in_specs=[pl.BlockSpec((1, tq, ATT_Q_W), lambda bi, i: (bi, i, 0)),
                  pl.BlockSpec((1, ATT_K_W, tk), lambda bi, i: (bi, 0, 0)),
                  pl.BlockSpec((1, tk, ATT_K_W), lambda bi, i: (bi, 0, 0))],
        out_specs=pl.BlockSpec((1, tq, ATT_Q_W), lambda bi, i: (bi, i, 0)),
        out_shape=jax.ShapeDtypeStruct((b, t, ATT_Q_W), BF16),
        compiler_params=_cparams("parallel", "parallel"),
        name="attention",
    )(aq, akt, av)


def _post_mixer_kernel(ret_ref, att_ref, x_ref, mod_ref, lng_ref, lnb_ref, wo_ref, l1g_ref, l1b_ref,
                       rwh_ref, rwl_ref, rb_ref, tri_ref,
                       x1_ref, h2_ref, idx_ref, gate_ref, pos_ref, cnt_ref):
    first = (pl.program_id(0) == 0) & (pl.program_id(1) == 0)

    @pl.when(first)
    def _():
        cnt_ref[...] = jnp.zeros_like(cnt_ref)

    half = ret_ref.shape[2]
    y = (jnp.dot(ret_ref[0], wo_ref[0:half, :], preferred_element_type=F32)
         + jnp.dot(att_ref[0], wo_ref[half:, :], preferred_element_type=F32))
    xn = _layer_norm(x_ref[0], lng_ref[...], lnb_ref[...])
    x1 = _layer_norm(DN_ALPHA * xn + mod_ref[0, 2:3, :] * y, l1g_ref[...], l1b_ref[...])
    x1_ref[...] = x1
    h2 = x1 * (1.0 + mod_ref[0, 4:5, :]) + mod_ref[0, 3:4, :]
    h2_ref[...] = h2

    h_hi = h2.astype(BF16)
    h_lo = (h2 - h_hi.astype(F32)).astype(BF16)
    logits = (jnp.dot(h_hi, rwh_ref[...], preferred_element_type=F32)
              + jnp.dot(h_lo, rwh_ref[...], preferred_element_type=F32)
              + jnp.dot(h_hi, rwl_ref[...], preferred_element_type=F32)) + rb_ref[...]

    tt = logits.shape[0]
    lane = lax.broadcasted_iota(jnp.int32, (tt, N_EXPERTS), 1)
    sels, vals, idxs = [], [], []
    onehot = jnp.zeros((tt, N_EXPERTS), F32)
    work = logits
    for _ in range(TOP_K):
        m = jnp.max(work, -1, keepdims=True)
        idx = jnp.min(jnp.where(work == m, lane, N_EXPERTS), -1, keepdims=True)
        sel = lane == idx
        sels.append(sel)
        vals.append(m)
        idxs.append(idx)
        onehot = onehot + sel.astype(F32)
        work = jnp.where(sel, -jnp.inf, work)
    exps = [jnp.exp(v - vals[0]) for v in vals]
    denom = exps[0] + exps[1] + exps[2] + exps[3]

    rank = jnp.dot(tri_ref[...], onehot.astype(BF16), preferred_element_type=F32) + cnt_ref[...]
    cnt_ref[...] += jnp.sum(onehot, 0, keepdims=True)

    lane4 = lax.broadcasted_iota(jnp.int32, (tt, TOP_K), 1)
    idx_out = jnp.zeros((tt, TOP_K), jnp.int32)
    gate_out = jnp.zeros((tt, TOP_K), F32)
    pos_out = jnp.zeros((tt, TOP_K), F32)
    for k in range(TOP_K):
        pk = jnp.sum(jnp.where(sels[k], rank, 0.0), -1, keepdims=True)
        idx_out = jnp.where(lane4 == k, idxs[k], idx_out)
        gate_out = jnp.where(lane4 == k, exps[k] / denom, gate_out)
        pos_out = jnp.where(lane4 == k, pk, pos_out)
    idx_ref[...] = idx_out
    gate_ref[...] = gate_out
    pos_ref[...] = pos_out.astype(jnp.int32)


def _post_mixer(ret, att, x, mod, lng, lnb, wo_bf, l1g, l1b, rw_hi, rw_lo, rb, tri):
    b, t, d = x.shape
    tt = TOKEN_TILE
    nt = t // tt
    n = b * t
    const2 = lambda bi, i: (0, 0)
    tok3 = lambda bi, i: (bi, i, 0)
    flat = lambda bi, i: (bi * nt + i, 0)
    vec = pl.BlockSpec((1, d), const2)
    return pl.pallas_call(
        _post_mixer_kernel,
        grid=(b, nt),
        in_specs=[pl.BlockSpec((1, tt, ret.shape[2]), tok3),
                  pl.BlockSpec((1, tt, att.shape[2]), tok3),
                  pl.BlockSpec((1, tt, d), tok3),
                  pl.BlockSpec((1, 6, d), lambda bi, i: (bi, 0, 0)),
                  vec, vec,
                  pl.BlockSpec(wo_bf.shape, const2),
                  vec, vec,
                  pl.BlockSpec(rw_hi.shape, const2),
                  pl.BlockSpec(rw_lo.shape, const2),
                  pl.BlockSpec((1, N_EXPERTS), const2),
                  pl.BlockSpec((tt, tt), const2)],
        out_specs=[pl.BlockSpec((tt, d), flat),
                   pl.BlockSpec((tt, d), flat),
                   pl.BlockSpec((tt, TOP_K), flat),
                   pl.BlockSpec((tt, TOP_K), flat),
                   pl.BlockSpec((tt, TOP_K), flat),
                   pl.BlockSpec((1, N_EXPERTS), const2)],
        out_shape=[jax.ShapeDtypeStruct((n, d), F32),
                   jax.ShapeDtypeStruct((n, d), F32),
                   jax.ShapeDtypeStruct((n, TOP_K), jnp.int32),
                   jax.ShapeDtypeStruct((n, TOP_K), F32),
                   jax.ShapeDtypeStruct((n, TOP_K), jnp.int32),
                   jax.ShapeDtypeStruct((1, N_EXPERTS), F32)],
        compiler_params=_cparams("arbitrary", "arbitrary"),
        name="post_mixer",
    )(ret, att, x, mod, lng, lnb, wo_bf, l1g, l1b, rw_hi, rw_lo, rb, tri)


def _row_gather(src_hbm, idx_ref, dst_ref, sem, n_rows, dst_row):
    def body(r, carry):
        pltpu.make_async_copy(src_hbm.at[pl.ds(idx_ref[0, 0, r], 1)],
                              dst_ref.at[pl.ds(dst_row(r), 1)], sem).start()
        return carry
    lax.fori_loop(0, n_rows, body, 0)


def _experts_kernel(be_ref, nact_ref, tok_ref, gate_ref, h_hbm, wup_ref, bup_ref, wdn_ref, bdn_ref,
                    y_ref, xbuf, sem, wup_bf, wdn_bf):
    j = pl.program_id(0)
    tm = xbuf.shape[0]
    dff = wdn_ref.shape[1]

    @pl.when(j < nact_ref[0])
    def _():
        _row_gather(h_hbm, tok_ref, xbuf, sem, tm, lambda r: r)
        prev = be_ref[jnp.maximum(j - 1, 0)]

        @pl.when((j == 0) | (be_ref[j] != prev))
        def _():
            wup_bf[...] = wup_ref[0].astype(BF16)
            wdn_bf[...] = wdn_ref[0].astype(BF16)

        pltpu.make_async_copy(h_hbm.at[pl.ds(0, tm)], xbuf, sem).wait()
        u = jnp.dot(xbuf[...].astype(BF16), wup_bf[...], preferred_element_type=F32) + bup_ref[0]
        glu = jnp.minimum(u[:, :dff], SWIGLU_LIMIT)
        lin = jnp.clip(u[:, dff:], -SWIGLU_LIMIT, SWIGLU_LIMIT)
        act = glu * (1.0 / (1.0 + jnp.exp(-SWIGLU_ALPHA * glu))) * (lin + 1.0)
        y = jnp.dot(act.astype(BF16), wdn_bf[...], preferred_element_type=F32) + bdn_ref[0]
        y_ref[...] = y * gate_ref[...]

    @pl.when(j >= nact_ref[0])
    def _():
        y_ref[...] = jnp.zeros_like(y_ref)


def _experts(block_expert, n_active, slot_tok, slot_gate, h2, w_up, b_up, w_down, b_down):
    n_blocks = block_expert.shape[0]
    tm = EXPERT_ROWS
    d = h2.shape[1]
    e, _, up_w = w_up.shape
    dff = w_down.shape[1]
    by_expert = lambda j, be, na: (be[j], 0, 0)
    grid_spec = pltpu.PrefetchScalarGridSpec(
        num_scalar_prefetch=2,
        grid=(n_blocks,),
        in_specs=[pl.BlockSpec((1, 1, tm), lambda j, be, na: (j, 0, 0), memory_space=pltpu.SMEM),
                  pl.BlockSpec((tm, 1), lambda j, be, na: (j, 0)),
                  pl.BlockSpec(memory_space=pl.ANY),
                  pl.BlockSpec((1, d, up_w), by_expert),
                  pl.BlockSpec((1, 1, up_w), by_expert),
                  pl.BlockSpec((1, dff, d), by_expert),
                  pl.BlockSpec((1, 1, d), by_expert)],
        out_specs=pl.BlockSpec((tm, d), lambda j, be, na: (j, 0)),
        scratch_shapes=[pltpu.VMEM((tm, d), F32),
                        pltpu.SemaphoreType.DMA(()),
                        pltpu.VMEM((d, up_w), BF16),
                        pltpu.VMEM((dff, d), BF16)])
    return pl.pallas_call(
        _experts_kernel,
        grid_spec=grid_spec,
        out_shape=jax.ShapeDtypeStruct((n_blocks * tm, d), F32),
        compiler_params=_cparams("arbitrary"),
        name="experts",
    )(block_expert, n_active, slot_tok.reshape(n_blocks, 1, tm), slot_gate.reshape(-1, 1), h2,
      w_up, b_up.reshape(e, 1, up_w), w_down, b_down.reshape(e, 1, d))


def _combine_kernel(dest_ref, x1_ref, mod_ref, g_ref, b_ref, y_hbm, o_ref, ybuf, sem):
    tt = x1_ref.shape[0]
    _row_gather(y_hbm, dest_ref, ybuf, sem, tt * TOP_K,
                lambda a: (a % TOP_K) * tt + a // TOP_K)
    pltpu.make_async_copy(y_hbm.at[pl.ds(0, tt * TOP_K)], ybuf, sem).wait()
    f = ybuf[0:tt, :]
    for k in range(1, TOP_K):
        f = f + ybuf[k * tt:(k + 1) * tt, :]
    o_ref[...] = _layer_norm(DN_ALPHA * x1_ref[...] + mod_ref[0, 5:6, :] * f, g_ref[...], b_ref[...])


def _combine(dest, x1, mod, g, b, y_sorted, tokens_per_sample):
    n, d = x1.shape
    tt = COMBINE_TILE
    tiles_per_sample = tokens_per_sample // tt
    const2 = lambda i: (0, 0)
    return pl.pallas_call(
        _combine_kernel,
        grid=(n // tt,),
        in_specs=[pl.BlockSpec((1, 1, tt * TOP_K), lambda i: (i, 0, 0), memory_space=pltpu.SMEM),
                  pl.BlockSpec((tt, d), lambda i: (i, 0)),
                  pl.BlockSpec((1, 6, d), lambda i: (i // tiles_per_sample, 0, 0)),
                  pl.BlockSpec((1, d), const2),
                  pl.BlockSpec((1, d), const2),
                  pl.BlockSpec(memory_space=pl.ANY)],
        out_specs=pl.BlockSpec((tt, d), lambda i: (i, 0)),
        out_shape=jax.ShapeDtypeStruct((n, d), F32),
        scratch_shapes=[pltpu.VMEM((tt * TOP_K, d), F32), pltpu.SemaphoreType.DMA(())],
        compiler_params=_cparams("arbitrary"),
        name="combine",
    )(dest.reshape(n // tt, 1, tt * TOP_K), x1, mod, g, b, y_sorted)


def _rope_tables(t):
    rows = jnp.repeat(jnp.arange(t // GRID_W, dtype=jnp.int32), GRID_W).astype(F32)
    cols = jnp.tile(jnp.arange(GRID_W, dtype=jnp.int32), t // GRID_W).astype(F32)
    n_freq = HEAD_DIM // 4
    inv = ROPE_BASE ** (-jnp.arange(n_freq, dtype=F32) / n_freq)
    ang_r = rows[:, None] * inv
    ang_c = cols[:, None] * inv
    ang = jnp.concatenate([ang_r, ang_r, ang_c, ang_c], -1)
    ang = jnp.concatenate([ang, ang], -1)
    cos, sin = jnp.cos(ang), jnp.sin(ang)
    first_half = (jnp.arange(LANES) % 32) < 16
    return cos, jnp.where(first_half, -sin, 0.0), jnp.where(first_half, 0.0, sin)


def kernel(x, c, ctx, c_ctx, ln_in_g, ln_in_b, w_ada, b_ada, w_in, ret_log_decay_f, ret_log_decay_b,
           ret_gn_g, ret_gn_b, q_norm_g, k_norm_g, w_o, ln1_g, ln1_b, router_w, router_b,
           w_up, b_up, w_down, b_down, ln2_g, ln2_b):
    b, t, d = x.shape
    n = b * t
    row = lambda v: v.reshape(1, -1)

    cc = jnp.concatenate([c, c_ctx[None]], 0)
    cc = jnp.pad(cc, ((0, (-cc.shape[0]) % 8), (0, 0)))
    mod_all = _ada_mod(cc, w_ada[0], row(b_ada[0]))
    mod = mod_all[:b].reshape(b, 6, d)
    mod_ctx = mod_all[b:b + 1].reshape(1, 6, d)

    w_in_bf = w_in[0].astype(BF16)
    lane = jnp.arange(LANES)
    seg = (lane[:, None] // HEAD_DIM == lane[None, :] // HEAD_DIM).astype(BF16)
    qg = jnp.tile(q_norm_g[0], ATT_HEADS).reshape(1, -1)
    kg = jnp.tile(k_norm_g[0], ATT_KV_HEADS).reshape(1, -1)
    lng, lnb = row(ln_in_g), row(ln_in_b)

    rq, rk, rv, rg, aq, akt, av = _in_proj(x, mod, lng, lnb, w_in_bf, seg, qg, kg, _rope_tables(t))
    _, rkc, rvc, _, _, aktc, avc = _in_proj(ctx, mod_ctx, lng, lnb, w_in_bf, seg, qg, kg, None)

    ret = _retention(row(ret_log_decay_f[0]), row(ret_log_decay_b[0]), rq, rk, rv, rg, rkc, rvc,
                     row(ret_gn_g[0]), row(ret_gn_b[0]))
    att = _attention(aq, jnp.concatenate([aktc, akt], 2), jnp.concatenate([avc, av], 1))

    rw = router_w[0]
    rw_hi = rw.astype(BF16)
    rw_lo = (rw - rw_hi.astype(F32)).astype(BF16)
    tri = jnp.tril(jnp.ones((TOKEN_TILE, TOKEN_TILE), BF16), -1)
    x1, h2, top_idx, gates, pos, counts = _post_mixer(
        ret, att, x, mod, lng, lnb, w_o[0].astype(BF16), row(ln1_g[0]), row(ln1_b[0]),
        rw_hi, rw_lo, row(router_b[0]), tri)

    tm = EXPERT_ROWS
    counts = counts[0].astype(jnp.int32)
    padded = (counts + tm - 1) // tm * tm
    pad_end = jnp.cumsum(padded)
    pad_start = pad_end - padded
    n_blocks = (n * TOP_K) // tm + N_EXPERTS
    dest = (pad_start[top_idx] + pos).reshape(-1)
    tok = jnp.repeat(jnp.arange(n, dtype=jnp.int32), TOP_K)
    slot_tok = jnp.zeros((n_blocks * tm,), jnp.int32).at[dest].set(tok)
    slot_gate = jnp.zeros((n_blocks * tm,), F32).at[dest].set(gates.reshape(-1))
    block_start = jnp.arange(n_blocks, dtype=jnp.int32) * tm
    block_expert = jnp.minimum(jnp.sum(pad_end[None, :] <= block_start[:, None], axis=1),
                               N_EXPERTS - 1).astype(jnp.int32)
    n_active = (pad_end[-1:] // tm).astype(jnp.int32)

    y_sorted = _experts(block_expert, n_active, slot_tok, slot_gate, h2, w_up[0], b_up[0],
                        w_down[0], b_down[0])
    out = _combine(dest, x1, mod, row(ln2_g[0]), row(ln2_b[0]), y_sorted, t)
    return out.reshape(b, t, d)
```

```python
import functools

import jax
import jax.numpy as jnp
from jax import lax
from jax.experimental import pallas as pl
from jax.experimental.pallas import tpu as pltpu

GRID_W = 64
HEAD_DIM = 64
ROPE_BASE = 10000.0
LN_EPS = 1e-6
RET_HEADS = 4
RET_DK = 64
RET_DV = 128
RET_CHUNK = 128
ATT_HEADS = 8
ATT_KV_HEADS = 2
ATT_GROUP = ATT_HEADS // ATT_KV_HEADS
N_EXPERTS = 32
TOP_K = 4
SWIGLU_LIMIT = 7.0
SWIGLU_ALPHA = 1.702
DEPTH = 1
DN_ALPHA = (2.0 * DEPTH) ** 0.25

RET_Q_W = RET_HEADS * RET_DK
RET_V_W = RET_HEADS * RET_DV
ATT_Q_W = ATT_HEADS * HEAD_DIM
ATT_K_W = ATT_KV_HEADS * HEAD_DIM

LANES = 128
VMEM_LIMIT_BYTES = 56 * 1024 * 1024

TOKEN_TILE = 512
ATT_Q_TILE = 256
EXPERT_ROWS = 512
SEG_ALIGN = 16
SEG_SIZES = (512, 256, 128, 64, 32, 16)
TAIL_SIZES = tuple(s for s in SEG_SIZES if s < EXPERT_ROWS)
TILE_SORTED_ROWS = 2560

F32 = jnp.float32
BF16 = jnp.bfloat16


def _cparams(*sem):
    return pltpu.CompilerParams(dimension_semantics=sem, vmem_limit_bytes=VMEM_LIMIT_BYTES)


def _layer_norm(x, g, b):
    mu = jnp.mean(x, -1, keepdims=True)
    xc = x - mu
    var = jnp.mean(xc * xc, -1, keepdims=True)
    return xc * lax.rsqrt(var + LN_EPS) * g + b


def _silu(x):
    return x * (1.0 / (1.0 + jnp.exp(-x)))


def _ada_kernel(c_ref, w_ref, b_ref, o_ref):
    a = _silu(c_ref[...])
    o_ref[...] = jnp.dot(a, w_ref[...], preferred_element_type=F32,
                         precision=lax.Precision.HIGHEST) + b_ref[...]


def _ada_mod(cc, w, b):
    m, d = cc.shape
    n = w.shape[1]
    tn = 1536
    return pl.pallas_call(
        _ada_kernel,
        grid=(n // tn,),
        in_specs=[pl.BlockSpec((m, d), lambda j: (0, 0)),
                  pl.BlockSpec((d, tn), lambda j: (0, j)),
                  pl.BlockSpec((1, tn), lambda j: (0, j))],
        out_specs=pl.BlockSpec((m, tn), lambda j: (0, j)),
        out_shape=jax.ShapeDtypeStruct((m, n), F32),
        compiler_params=_cparams("arbitrary"),
        name="ada_mod",
    )(cc, w, b)


def _rope(x, cos, sinn, sinp):
    outs = []
    for j in range(x.shape[1] // LANES):
        xg = x[:, j * LANES:(j + 1) * LANES]
        outs.append(xg * cos + pltpu.roll(xg, LANES - 16, 1) * sinn + pltpu.roll(xg, 16, 1) * sinp)
    return outs[0] if len(outs) == 1 else jnp.concatenate(outs, axis=1)


def _head_rms(x, seg, gain):
    outs = []
    for j in range(x.shape[1] // LANES):
        xg = x[:, j * LANES:(j + 1) * LANES]
        sq = xg * xg
        hi = sq.astype(BF16)
        lo = (sq - hi.astype(F32)).astype(BF16)
        ms = (jnp.dot(hi, seg, preferred_element_type=F32)
              + jnp.dot(lo, seg, preferred_element_type=F32)) * (1.0 / HEAD_DIM)
        outs.append(xg * lax.rsqrt(ms + LN_EPS) * gain[:, j * LANES:(j + 1) * LANES])
    return outs[0] if len(outs) == 1 else jnp.concatenate(outs, axis=1)


def _in_proj_kernel(*refs, rope):
    if rope:
        (x_ref, mod_ref, lng_ref, lnb_ref, w_ref, seg_ref, qg_ref, kg_ref, cos_ref, sinn_ref, sinp_ref,
         rq_ref, rk_ref, rv_ref, rg_ref, aq_ref, akt_ref, av_ref) = refs
    else:
        (x_ref, mod_ref, lng_ref, lnb_ref, w_ref, seg_ref, qg_ref, kg_ref,
         rq_ref, rk_ref, rv_ref, rg_ref, aq_ref, akt_ref, av_ref) = refs
    x = x_ref[0]
    xn = _layer_norm(x, lng_ref[...], lnb_ref[...])
    h = xn * (1.0 + mod_ref[0, 1:2, :]) + mod_ref[0, 0:1, :]
    u = jnp.dot(h.astype(BF16), w_ref[...], preferred_element_type=F32)
    o = 0
    rq = u[:, o:o + RET_Q_W]; o += RET_Q_W
    rk = u[:, o:o + RET_Q_W]; o += RET_Q_W
    rv = u[:, o:o + RET_V_W]; o += RET_V_W
    rg = u[:, o:o + RET_V_W]; o += RET_V_W
    aq = u[:, o:o + ATT_Q_W]; o += ATT_Q_W
    ak = u[:, o:o + ATT_K_W]; o += ATT_K_W
    av = u[:, o:o + ATT_K_W]
    seg = seg_ref[...]
    aq = _head_rms(aq, seg, qg_ref[...])
    ak = _head_rms(ak, seg, kg_ref[...])
    if rope:
        cos, sinn, sinp = cos_ref[...], sinn_ref[...], sinp_ref[...]
        rq = _rope(rq, cos, sinn, sinp)
        rk = _rope(rk, cos, sinn, sinp)
        aq = _rope(aq, cos, sinn, sinp)
        ak = _rope(ak, cos, sinn, sinp)
    rq_ref[0] = rq
    rk_ref[0] = rk * (RET_DK ** -0.5)
    rv_ref[0] = rv.astype(BF16)
    rg_ref[0] = rg.astype(BF16)
    aq_ref[0] = (aq * (HEAD_DIM ** -0.5)).astype(BF16)
    akt_ref[0] = ak.T.astype(BF16)
    av_ref[0] = av.astype(BF16)


def _in_proj(x, mod, lng, lnb, w_bf, seg, qg, kg, tables):
    b, t, d = x.shape
    tt = min(TOKEN_TILE, t)
    rope = tables is not None
    mod_b = mod.shape[0]
    mod_map = (lambda bi, i: (bi, 0, 0)) if mod_b > 1 else (lambda bi, i: (0, 0, 0))
    const2 = lambda bi, i: (0, 0)
    tok3 = lambda bi, i: (bi, i, 0)
    in_specs = [pl.BlockSpec((1, tt, d), tok3),
                pl.BlockSpec((1, 6, d), mod_map),
                pl.BlockSpec((1, d), const2),
                pl.BlockSpec((1, d), const2),
                pl.BlockSpec(w_bf.shape, const2),
                pl.BlockSpec(seg.shape, const2),
                pl.BlockSpec(qg.shape, const2),
                pl.BlockSpec(kg.shape, const2)]
    args = [x, mod, lng, lnb, w_bf, seg, qg, kg]
    if rope:
        in_specs += [pl.BlockSpec((tt, LANES), lambda bi, i: (i, 0))] * 3
        args += list(tables)
    out_shape = [jax.ShapeDtypeStruct((b, t, RET_Q_W), F32),
                 jax.ShapeDtypeStruct((b, t, RET_Q_W), F32),
                 jax.ShapeDtypeStruct((b, t, RET_V_W), BF16),
                 jax.ShapeDtypeStruct((b, t, RET_V_W), BF16),
                 jax.ShapeDtypeStruct((b, t, ATT_Q_W), BF16),
                 jax.ShapeDtypeStruct((b, ATT_K_W, t), BF16),
                 jax.ShapeDtypeStruct((b, t, ATT_K_W), BF16)]
    out_specs = [pl.BlockSpec((1, tt, RET_Q_W), tok3),
                 pl.BlockSpec((1, tt, RET_Q_W), tok3),
                 pl.BlockSpec((1, tt, RET_V_W), tok3),
                 pl.BlockSpec((1, tt, RET_V_W), tok3),
                 pl.BlockSpec((1, tt, ATT_Q_W), tok3),
                 pl.BlockSpec((1, ATT_K_W, tt), lambda bi, i: (bi, 0, i)),
                 pl.BlockSpec((1, tt, ATT_K_W), tok3)]
    return pl.pallas_call(
        functools.partial(_in_proj_kernel, rope=rope),
        grid=(b, t // tt),
        in_specs=in_specs,
        out_specs=out_specs,
        out_shape=out_shape,
        compiler_params=_cparams("parallel", "parallel"),
        name="in_proj_rope" if rope else "in_proj_ctx",
    )(*args)


def _retention_kernel(lgf_ref, lgb_ref, q_ref, k_ref, v_ref, g_ref, kc_ref, vc_ref, gng_ref, gnb_ref,
                      o_ref, kvf_ref, kvb_ref, sf_ref, sb_ref):
    pair = pl.program_id(1)
    t = q_ref.shape[1]
    n_chunks = t // RET_CHUNK
    ctx_len = kc_ref.shape[1]
    c = RET_CHUNK
    pos = lax.broadcasted_iota(jnp.int32, (c, 1), 0).astype(F32)
    row = lax.broadcasted_iota(jnp.int32, (c, c), 0)
    col = lax.broadcasted_iota(jnp.int32, (c, c), 1)
    diff = (row - col).astype(F32)
    cpos = lax.broadcasted_iota(jnp.int32, (ctx_len, 1), 0).astype(F32)
    tn = (((0,), (0,)), ((), ()))
    nt = (((1,), (1,)), ((), ()))

    heads = []
    for hh in range(2):
        head = pair * 2 + hh
        lgf = lgf_ref[0, head]
        lgb = lgb_ref[0, head]
        heads.append(dict(
            ql=slice(hh * RET_DK, (hh + 1) * RET_DK),
            vl=slice(hh * RET_DV, (hh + 1) * RET_DV),
            lgf=lgf, lgb=lgb,
            kdf=jnp.exp(lgf * (c - 1.0 - pos)),
            kdb=jnp.exp(lgb * pos),
            qdf=jnp.exp(lgf * (pos + 1.0)),
            qdb=jnp.exp(lgb * (c - pos)),
            decay=(jnp.where(diff >= 0, jnp.exp(lgf * jnp.maximum(diff, 0.0)), 0.0)
                   + jnp.where(diff <= 0, jnp.exp(lgb * jnp.maximum(-diff, 0.0)), 0.0))))

    def kv_body(n, carry):
        r = pl.ds(pl.multiple_of(n * c, c), c)
        for hh, hd in enumerate(heads):
            kc = k_ref[0, r, hd["ql"]]
            vc = v_ref[0, r, hd["vl"]]
            kvf_ref[hh, n] = lax.dot_general((kc * hd["kdf"]).astype(BF16), vc, tn, preferred_element_type=F32)
            kvb_ref[hh, n] = lax.dot_general((kc * hd["kdb"]).astype(BF16), vc, tn, preferred_element_type=F32)
        return carry

    lax.fori_loop(0, n_chunks, kv_body, 0, unroll=2)

    for hh, hd in enumerate(heads):
        lgf, lgb = hd["lgf"], hd["lgb"]
        kctx = kc_ref[0, :, hd["ql"]]
        vctx = vc_ref[0, :, hd["vl"]]
        s_f = lax.dot_general((kctx * jnp.exp(lgf * (ctx_len - 1.0 - cpos))).astype(BF16), vctx, tn,
                              preferred_element_type=F32)
        s_b = lax.dot_general((kctx * jnp.exp(lgb * cpos)).astype(BF16), vctx, tn,
                              preferred_element_type=F32)
        gfc = jnp.exp(lgf * c)
        gbc = jnp.exp(lgb * c)
        for n in range(n_chunks):
            sf_ref[hh, n] = s_f.astype(BF16)
            s_f = gfc * s_f + kvf_ref[hh, n]
        for n in range(n_chunks - 1, -1, -1):
            sb_ref[hh, n] = s_b.astype(BF16)
            s_b = gbc * s_b + kvb_ref[hh, n]

    def out_body(n, carry):
        r = pl.ds(pl.multiple_of(n * c, c), c)
        for hh, hd in enumerate(heads):
            ql, vl = hd["ql"], hd["vl"]
            qc = q_ref[0, r, ql]
            kc = k_ref[0, r, ql]
            vc = v_ref[0, r, vl]
            s = lax.dot_general(qc.astype(BF16), kc.astype(BF16), nt, preferred_element_type=F32)
            y = jnp.dot((s * hd["decay"]).astype(BF16), vc, preferred_element_type=F32)
            y += jnp.dot((qc * hd["qdf"]).astype(BF16), sf_ref[hh, n], preferred_element_type=F32)
            y += jnp.dot((qc * hd["qdb"]).astype(BF16), sb_ref[hh, n], preferred_element_type=F32)
            mu = jnp.mean(y, -1, keepdims=True)
            yc = y - mu
            var = jnp.mean(yc * yc, -1, keepdims=True)
            yn = yc * lax.rsqrt(var + LN_EPS) * gng_ref[:, vl] + gnb_ref[:, vl]
            o_ref[0, r, vl] = (yn * _silu(g_ref[0, r, vl].astype(F32))).astype(o_ref.dtype)
        return carry

    lax.fori_loop(0, n_chunks, out_body, 0, unroll=2)


def _retention(lgf, lgb, rq, rk, rv, rg, rkc, rvc, gng, gnb):
    b, t, _ = rq.shape
    ctx_len = rkc.shape[1]
    n_chunks = t // RET_CHUNK
    pair3 = lambda bi, p: (bi, 0, p)
    smem = pl.BlockSpec(memory_space=pltpu.SMEM)
    state = (2, n_chunks, RET_DK, RET_DV)
    return pl.pallas_call(
        _retention_kernel,
        grid=(b, RET_HEADS // 2),
        in_specs=[smem, smem,
                  pl.BlockSpec((1, t, 2 * RET_DK), pair3),
                  pl.BlockSpec((1, t, 2 * RET_DK), pair3),
                  pl.BlockSpec((1, t, 2 * RET_DV), pair3),
                  pl.BlockSpec((1, t, 2 * RET_DV), pair3),
                  pl.BlockSpec((1, ctx_len, 2 * RET_DK), pair3),
                  pl.BlockSpec((1, ctx_len, 2 * RET_DV), pair3),
                  pl.BlockSpec((1, 2 * RET_DV), lambda bi, p: (0, p)),
                  pl.BlockSpec((1, 2 * RET_DV), lambda bi, p: (0, p))],
        out_specs=pl.BlockSpec((1, t, 2 * RET_DV), pair3),
        out_shape=jax.ShapeDtypeStruct((b, t, RET_V_W), BF16),
        scratch_shapes=[pltpu.VMEM(state, F32), pltpu.VMEM(state, F32),
                        pltpu.VMEM(state, BF16), pltpu.VMEM(state, BF16)],
        compiler_params=_cparams("parallel", "parallel"),
        name="retention",
    )(lgf, lgb, rq, rk, rv, rg, rkc, rvc, gng, gnb)


def _attention_kernel(q_ref, kt_ref, v_ref, o_ref):
    tq = q_ref.shape[1]
    lane = lax.broadcasted_iota(jnp.int32, (tq, LANES), 1)
    v = v_ref[0]
    for kvh in range(ATT_KV_HEADS):
        kt = kt_ref[0, kvh * HEAD_DIM:(kvh + 1) * HEAD_DIM, :]
        q = jnp.concatenate(
            [q_ref[0, :, (kvh * ATT_GROUP + g) * HEAD_DIM:(kvh * ATT_GROUP + g + 1) * HEAD_DIM]
             for g in range(ATT_GROUP)], axis=0)
        s = jnp.dot(q, kt, preferred_element_type=F32)
        p = jnp.exp(s - jnp.max(s, -1, keepdims=True))
        ov = jnp.dot(p.astype(BF16), v, preferred_element_type=F32)
        o = ov[:, :LANES] / ov[:, LANES:LANES + 1]
        for pair in range(ATT_GROUP // 2):
            halves = []
            for sub in range(2):
                og = o[(pair * 2 + sub) * tq:(pair * 2 + sub + 1) * tq, :]
                if kvh != sub:
                    og = pltpu.roll(og, HEAD_DIM, 1)
                halves.append(og)
            col = (kvh * ATT_GROUP // 2 + pair) * LANES
            o_ref[0, :, col:col + LANES] = jnp.where(lane < HEAD_DIM, halves[0], halves[1]).astype(o_ref.dtype)


def _attention(aq, akt, av_ones):
    b, t, _ = aq.shape
    tk = akt.shape[2]
    tq = ATT_Q_TILE
    return pl.pallas_call(
        _attention_kernel,
        grid=(b, t // tq),
        in_specs=[pl.BlockSpec((1, tq, ATT_Q_W), lambda bi, i: (bi, i, 0)),
                  pl.BlockSpec((1, ATT_K_W, tk), lambda bi, i: (bi, 0, 0)),
                  pl.BlockSpec((1, tk, av_ones.shape[2]), lambda bi, i: (bi, 0, 0))],
        out_specs=pl.BlockSpec((1, tq, ATT_Q_W), lambda bi, i: (bi, i, 0)),
        out_shape=jax.ShapeDtypeStruct((b, t, ATT_Q_W), BF16),
        compiler_params=_cparams("parallel", "parallel"),
        name="attention",
    )(aq, akt, av_ones)


def _post_mixer_kernel(ret_ref, att_ref, x_ref, mod_ref, lng_ref, lnb_ref, wo_ref, l1g_ref, l1b_ref,
                       rwh_ref, rwl_ref, rb_ref, tri_ref, triu_ref,
                       x1_ref, h2_ref, gate_ref, lrow_ref, cnt_ref, lstart_ref, tbase_ref, base_acc):
    first = (pl.program_id(0) == 0) & (pl.program_id(1) == 0)

    @pl.when(first)
    def _():
        base_acc[...] = jnp.zeros_like(base_acc)

    half = ret_ref.shape[2]
    y = (jnp.dot(ret_ref[0], wo_ref[0:half, :], preferred_element_type=F32)
         + jnp.dot(att_ref[0], wo_ref[half:, :], preferred_element_type=F32))
    xn = _layer_norm(x_ref[0], lng_ref[...], lnb_ref[...])
    x1 = _layer_norm(DN_ALPHA * xn + mod_ref[0, 2:3, :] * y, l1g_ref[...], l1b_ref[...])
    x1_ref[...] = x1
    h2 = x1 * (1.0 + mod_ref[0, 4:5, :]) + mod_ref[0, 3:4, :]
    h_hi = h2.astype(BF16)
    h2_ref[...] = h_hi

    h_lo = (h2 - h_hi.astype(F32)).astype(BF16)
    logits = (jnp.dot(h_hi, rwh_ref[...], preferred_element_type=F32)
              + jnp.dot(h_lo, rwh_ref[...], preferred_element_type=F32)
              + jnp.dot(h_hi, rwl_ref[...], preferred_element_type=F32)) + rb_ref[...]

    tt = logits.shape[0]
    lane = lax.broadcasted_iota(jnp.int32, (tt, N_EXPERTS), 1)
    sels, vals = [], []
    onehot = jnp.zeros((tt, N_EXPERTS), F32)
    work = logits
    for _ in range(TOP_K):
        m = jnp.max(work, -1, keepdims=True)
        idx = jnp.min(jnp.where(work == m, lane, N_EXPERTS), -1, keepdims=True)
        sel = lane == idx
        sels.append(sel)
        vals.append(m)
        onehot = onehot + sel.astype(F32)
        work = jnp.where(sel, -jnp.inf, work)
    exps = [jnp.exp(v - vals[0]) for v in vals]
    denom = exps[0] + exps[1] + exps[2] + exps[3]

    count = jnp.sum(onehot, 0, keepdims=True)
    units = jnp.floor((count + (SEG_ALIGN - 1.0)) * (1.0 / SEG_ALIGN))
    lstart = SEG_ALIGN * jnp.dot(jnp.broadcast_to(units, (8, N_EXPERTS)).astype(BF16), triu_ref[...],
                                 preferred_element_type=F32)[0:1, :]
    rank = jnp.dot(tri_ref[...], onehot.astype(BF16), preferred_element_type=F32) + lstart

    lane4 = lax.broadcasted_iota(jnp.int32, (tt, TOP_K), 1)
    gate_out = jnp.zeros((tt, TOP_K), F32)
    lrow_out = jnp.zeros((tt, TOP_K), F32)
    for k in range(TOP_K):
        pk = jnp.sum(jnp.where(sels[k], rank, 0.0), -1, keepdims=True)
        gate_out = jnp.where(lane4 == k, exps[k] / denom, gate_out)
        lrow_out = jnp.where(lane4 == k, pk, lrow_out)
    gate_ref[...] = gate_out
    lrow_ref[...] = lrow_out.astype(jnp.int32)
    cnt_ref[0] = (units * SEG_ALIGN).astype(jnp.int32)
    lstart_ref[0] = lstart.astype(jnp.int32)
    tbase_ref[0] = base_acc[...].astype(jnp.int32)
    base_acc[...] += units * SEG_ALIGN


def _post_mixer(ret, att, x, mod, lng, lnb, wo_bf, l1g, l1b, rw_hi, rw_lo, rb, tri, triu):
    b, t, d = x.shape
    tt = TOKEN_TILE
    nt = t // tt
    n = b * t
    const2 = lambda bi, i: (0, 0)
    tok3 = lambda bi, i: (bi, i, 0)
    flat = lambda bi, i: (bi * nt + i, 0)
    tile3 = lambda bi, i: (bi * nt + i, 0, 0)
    vec = pl.BlockSpec((1, d), const2)
    table = jax.ShapeDtypeStruct((b * nt, 1, N_EXPERTS), jnp.int32)
    return pl.pallas_call(
        _post_mixer_kernel,
        grid=(b, nt),
        in_specs=[pl.BlockSpec((1, tt, ret.shape[2]), tok3),
                  pl.BlockSpec((1, tt, att.shape[2]), tok3),
                  pl.BlockSpec((1, tt, d), tok3),
                  pl.BlockSpec((1, 6, d), lambda bi, i: (bi, 0, 0)),
                  vec, vec,
                  pl.BlockSpec(wo_bf.shape, const2),
                  vec, vec,
                  pl.BlockSpec(rw_hi.shape, const2),
                  pl.BlockSpec(rw_lo.shape, const2),
                  pl.BlockSpec((1, N_EXPERTS), const2),
                  pl.BlockSpec((tt, tt), const2),
                  pl.BlockSpec((N_EXPERTS, N_EXPERTS), const2)],
        out_specs=[pl.BlockSpec((tt, d), flat),
                   pl.BlockSpec((tt, d), flat),
                   pl.BlockSpec((tt, TOP_K), flat),
                   pl.BlockSpec((tt, TOP_K), flat),
                   pl.BlockSpec((1, 1, N_EXPERTS), tile3),
                   pl.BlockSpec((1, 1, N_EXPERTS), tile3),
                   pl.BlockSpec((1, 1, N_EXPERTS), tile3)],
        out_shape=[jax.ShapeDtypeStruct((n, d), F32),
                   jax.ShapeDtypeStruct((n, d), BF16),
                   jax.ShapeDtypeStruct((n, TOP_K), F32),
                   jax.ShapeDtypeStruct((n, TOP_K), jnp.int32),
                   table, table, table],
        scratch_shapes=[pltpu.VMEM((1, N_EXPERTS), F32)],
        compiler_params=_cparams("arbitrary", "arbitrary"),
        name="post_mixer",
    )(ret, att, x, mod, lng, lnb, wo_bf, l1g, l1b, rw_hi, rw_lo, rb, tri, triu)


def _segment_copies(cnt_ref, lstart_ref, dst_ref, tile, make_copy, act):
    def body(e, carry):
        t = tile * N_EXPERTS + e
        n = cnt_ref[t]
        ls = lstart_ref[t]
        ds = dst_ref[t]
        off = jnp.int32(0)
        for size in SEG_SIZES:
            bit = n & size

            @pl.when(bit != 0)
            def _():
                copy = make_copy(pl.multiple_of(ls + off, SEG_ALIGN), pl.multiple_of(ds + off, SEG_ALIGN), size)
                getattr(copy, act)()
            off = off + bit
        return carry
    lax.fori_loop(0, N_EXPERTS, body, 0)


def _zero_fill_copies(tail_len_ref, tail_dst_ref, nact_ref, zbuf, x_hbm, zsem, act):
    tm = zbuf.shape[0]

    def tail_body(e, carry):
        n = tail_len_ref[e]
        ds = tail_dst_ref[e]
        off = jnp.int32(0)
        for size in TAIL_SIZES:
            bit = n & size

            @pl.when(bit != 0)
            def _():
                copy = pltpu.make_async_copy(zbuf.at[pl.ds(0, size)],
                                             x_hbm.at[pl.ds(pl.multiple_of(ds + off, SEG_ALIGN), size)], zsem)
                getattr(copy, act)()
            off = off + bit
        return carry
    lax.fori_loop(0, N_EXPERTS, tail_body, 0)

    def block_body(j, carry):
        copy = pltpu.make_async_copy(zbuf, x_hbm.at[pl.ds(pl.multiple_of(j * tm, tm), tm)], zsem)
        getattr(copy, act)()
        return carry
    lax.fori_loop(nact_ref[0], x_hbm.shape[0] // tm, block_body, 0)


def _dispatch_kernel(cnt_ref, lstart_ref, dst_ref, tail_len_ref, tail_dst_ref, nact_ref,
                     h_ref, lrow_ref, x_hbm, sbuf, sem, zbuf, zsem):
    i = pl.program_id(0)
    last = pl.num_programs(0) - 1
    slot = i % 2
    rows = sbuf.shape[1]
    tt = h_ref.shape[0]

    def make_copy(tile_slot):
        def mk(ls, ds, size):
            return pltpu.make_async_copy(sbuf.at[tile_slot, pl.ds(ls, size)], x_hbm.at[pl.ds(ds, size)],
                                         sem.at[tile_slot])
        return mk

    @pl.when(i == 0)
    def _():
        zbuf[...] = jnp.zeros_like(zbuf)
        _zero_fill_copies(tail_len_ref, tail_dst_ref, nact_ref, zbuf, x_hbm, zsem, "start")

    r_iota = lax.broadcasted_iota(jnp.int32, (rows, tt), 0)
    perm = r_iota == lrow_ref[0, 0:1, :]
    for k in range(1, TOP_K):
        perm = perm | (r_iota == lrow_ref[0, k:k + 1, :])
    sbuf[slot] = jnp.dot(jnp.where(perm, 1.0, 0.0).astype(BF16), h_ref[...],
                         preferred_element_type=F32).astype(BF16)
    _segment_copies(cnt_ref, lstart_ref, dst_ref, i, make_copy(slot), "start")

    @pl.when(i > 0)
    def _():
        _segment_copies(cnt_ref, lstart_ref, dst_ref, i - 1, make_copy(1 - slot), "wait")

    @pl.when(i == last)
    def _():
        _segment_copies(cnt_ref, lstart_ref, dst_ref, i, make_copy(slot), "wait")
        _zero_fill_copies(tail_len_ref, tail_dst_ref, nact_ref, zbuf, x_hbm, zsem, "wait")


def _dispatch(cnt, lstart, dst, tail_len, tail_dst, n_active, h2, lrow_t, n_rows):
    n, d = h2.shape
    tt = TOKEN_TILE
    grid_spec = pltpu.PrefetchScalarGridSpec(
        num_scalar_prefetch=6,
        grid=(n // tt,),
        in_specs=[pl.BlockSpec((tt, d), lambda i, *_: (i, 0)),
                  pl.BlockSpec((1, TOP_K, tt), lambda i, *_: (i, 0, 0))],
        out_specs=pl.BlockSpec(memory_space=pl.ANY),
        scratch_shapes=[pltpu.VMEM((2, TILE_SORTED_ROWS, d), BF16),
                        pltpu.SemaphoreType.DMA((2,)),
                        pltpu.VMEM((EXPERT_ROWS, d), BF16),
                        pltpu.SemaphoreType.DMA(())])
    return pl.pallas_call(
        _dispatch_kernel,
        grid_spec=grid_spec,
        out_shape=jax.ShapeDtypeStruct((n_rows, d), BF16),
        compiler_params=_cparams("arbitrary"),
        name="dispatch",
    )(cnt, lstart, dst, tail_len, tail_dst, n_active, h2, lrow_t)


def _experts_kernel(be_ref, nact_ref, x_ref, wup_ref, bup_ref, wdn_ref, bdn_ref, y_ref,
                    wup_bf, wdn_bf):
    j = pl.program_id(0)
    dff = wdn_ref.shape[1]

    @pl.when(j < nact_ref[0])
    def _():
        prev = be_ref[jnp.maximum(j - 1, 0)]

        @pl.when((j == 0) | (be_ref[j] != prev))
        def _():
            wup_bf[...] = wup_ref[0].astype(BF16)
            wdn_bf[...] = wdn_ref[0].astype(BF16)

        u = jnp.dot(x_ref[...], wup_bf[...], preferred_element_type=F32) + bup_ref[0]
        glu = jnp.minimum(u[:, :dff], SWIGLU_LIMIT)
        lin = jnp.clip(u[:, dff:], -SWIGLU_LIMIT, SWIGLU_LIMIT)
        act = glu * (1.0 / (1.0 + jnp.exp(-SWIGLU_ALPHA * glu))) * (lin + 1.0)
        y = jnp.dot(act.astype(BF16), wdn_bf[...], preferred_element_type=F32) + bdn_ref[0]
        y_ref[...] = y.astype(y_ref.dtype)

    @pl.when(j >= nact_ref[0])
    def _():
        y_ref[...] = jnp.zeros_like(y_ref)


def _experts(block_expert, n_active, x_sorted, w_up, b_up, w_down, b_down):
    n_blocks = block_expert.shape[0]
    tm = EXPERT_ROWS
    d = x_sorted.shape[1]
    e, _, up_w = w_up.shape
    dff = w_down.shape[1]
    by_expert = lambda j, be, na: (be[j], 0, 0)
    x_map = lambda j, be, na: (jnp.minimum(j, na[0] - 1), 0)
    grid_spec = pltpu.PrefetchScalarGridSpec(
        num_scalar_prefetch=2,
        grid=(n_blocks,),
        in_specs=[pl.BlockSpec((tm, d), x_map),
                  pl.BlockSpec((1, d, up_w), by_expert),
                  pl.BlockSpec((1, 1, up_w), by_expert),
                  pl.BlockSpec((1, dff, d), by_expert),
                  pl.BlockSpec((1, 1, d), by_expert)],
        out_specs=pl.BlockSpec((tm, d), lambda j, be, na: (j, 0)),
        scratch_shapes=[pltpu.VMEM((d, up_w), BF16),
                        pltpu.VMEM((dff, d), BF16)])
    return pl.pallas_call(
        _experts_kernel,
        grid_spec=grid_spec,
        out_shape=jax.ShapeDtypeStruct((n_blocks * tm, d), BF16),
        compiler_params=_cparams("arbitrary"),
        name="experts",
    )(block_expert, n_active, x_sorted, w_up, b_up.reshape(e, 1, up_w), w_down,
      b_down.reshape(e, 1, d))


def _combine_kernel(cnt_ref, lstart_ref, dst_ref, lrow_ref, gate_ref, x1_ref, mod_ref, g_ref, b_ref,
                    y_hbm, o_ref, ybuf, sem):
    i = pl.program_id(0)
    last = pl.num_programs(0) - 1
    slot = i % 2
    rows = ybuf.shape[1]
    tt = x1_ref.shape[0]

    def make_copy(tile_slot):
        def mk(ls, ds, size):
            return pltpu.make_async_copy(y_hbm.at[pl.ds(ds, size)], ybuf.at[tile_slot, pl.ds(ls, size)],
                                         sem.at[tile_slot])
        return mk

    @pl.when(i == 0)
    def _():
        ybuf[...] = jnp.zeros_like(ybuf)
        _segment_copies(cnt_ref, lstart_ref, dst_ref, i, make_copy(slot), "start")

    @pl.when(i < last)
    def _():
        _segment_copies(cnt_ref, lstart_ref, dst_ref, i + 1, make_copy(1 - slot), "start")

    c_iota = lax.broadcasted_iota(jnp.int32, (tt, rows), 1)
    weights = jnp.zeros((tt, rows), F32)
    for k in range(TOP_K):
        weights = weights + jnp.where(c_iota == lrow_ref[:, k:k + 1], gate_ref[:, k:k + 1], 0.0)
    _segment_copies(cnt_ref, lstart_ref, dst_ref, i, make_copy(slot), "wait")
    f = jnp.dot(weights.astype(BF16), ybuf[slot], preferred_element_type=F32)
    o_ref[...] = _layer_norm(DN_ALPHA * x1_ref[...] + mod_ref[0, 5:6, :] * f, g_ref[...], b_ref[...])


def _combine(cnt, lstart, dst, lrow, gates, x1, mod, g, b, y_sorted, tokens_per_sample):
    n, d = x1.shape
    tt = TOKEN_TILE
    tiles_per_sample = tokens_per_sample // tt
    const2 = lambda i, *_: (0, 0)
    tok = lambda i, *_: (i, 0)
    grid_spec = pltpu.PrefetchScalarGridSpec(
        num_scalar_prefetch=3,
        grid=(n // tt,),
        in_specs=[pl.BlockSpec((tt, TOP_K), tok),
                  pl.BlockSpec((tt, TOP_K), tok),
                  pl.BlockSpec((tt, d), tok),
                  pl.BlockSpec((1, 6, d), lambda i, *_: (i // tiles_per_sample, 0, 0)),
                  pl.BlockSpec((1, d), const2),
                  pl.BlockSpec((1, d), const2),
                  pl.BlockSpec(memory_space=pl.ANY)],
        out_specs=pl.BlockSpec((tt, d), tok),
        scratch_shapes=[pltpu.VMEM((2, TILE_SORTED_ROWS, d), BF16),
                        pltpu.SemaphoreType.DMA((2,))])
    return pl.pallas_call(
        _combine_kernel,
        grid_spec=grid_spec,
        out_shape=jax.ShapeDtypeStruct((n, d), F32),
        compiler_params=_cparams("arbitrary"),
        name="combine",
    )(cnt, lstart, dst, lrow, gates, x1, mod, g, b, y_sorted)


def _rope_tables(t):
    rows = jnp.repeat(jnp.arange(t // GRID_W, dtype=jnp.int32), GRID_W).astype(F32)
    cols = jnp.tile(jnp.arange(GRID_W, dtype=jnp.int32), t // GRID_W).astype(F32)
    n_freq = HEAD_DIM // 4
    inv = ROPE_BASE ** (-jnp.arange(n_freq, dtype=F32) / n_freq)
    ang_r = rows[:, None] * inv
    ang_c = cols[:, None] * inv
    ang = jnp.concatenate([ang_r, ang_r, ang_c, ang_c], -1)
    ang = jnp.concatenate([ang, ang], -1)
    cos, sin = jnp.cos(ang), jnp.sin(ang)
    first_half = (jnp.arange(LANES) % 32) < 16
    return cos, jnp.where(first_half, -sin, 0.0), jnp.where(first_half, 0.0, sin)


def kernel(x, c, ctx, c_ctx, ln_in_g, ln_in_b, w_ada, b_ada, w_in, ret_log_decay_f, ret_log_decay_b,
           ret_gn_g, ret_gn_b, q_norm_g, k_norm_g, w_o, ln1_g, ln1_b, router_w, router_b,
           w_up, b_up, w_down, b_down, ln2_g, ln2_b):
    b, t, d = x.shape
    n = b * t
    row = lambda v: v.reshape(1, -1)

    cc = jnp.concatenate([c, c_ctx[None]], 0)
    cc = jnp.pad(cc, ((0, (-cc.shape[0]) % 8), (0, 0)))
    mod_all = _ada_mod(cc, w_ada[0], row(b_ada[0]))
    mod = mod_all[:b].reshape(b, 6, d)
    mod_ctx = mod_all[b:b + 1].reshape(1, 6, d)

    w_in_bf = w_in[0].astype(BF16)
    lane = jnp.arange(LANES)
    seg = (lane[:, None] // HEAD_DIM == lane[None, :] // HEAD_DIM).astype(BF16)
    qg = jnp.tile(q_norm_g[0], ATT_HEADS).reshape(1, -1)
    kg = jnp.tile(k_norm_g[0], ATT_KV_HEADS).reshape(1, -1)
    lng, lnb = row(ln_in_g), row(ln_in_b)

    rq, rk, rv, rg, aq, akt, av = _in_proj(x, mod, lng, lnb, w_in_bf, seg, qg, kg, _rope_tables(t))
    _, rkc, rvc, _, _, aktc, avc = _in_proj(ctx, mod_ctx, lng, lnb, w_in_bf, seg, qg, kg, None)

    ret = _retention(row(ret_log_decay_f[0]), row(ret_log_decay_b[0]), rq, rk, rv, rg, rkc, rvc,
                     row(ret_gn_g[0]), row(ret_gn_b[0]))
    tk = ctx.shape[1] + t
    av_ones = jnp.concatenate([jnp.concatenate([avc, av], 1), jnp.ones((b, tk, LANES), BF16)], 2)
    att = _attention(aq, jnp.concatenate([aktc, akt], 2), av_ones)

    rw = router_w[0]
    rw_hi = rw.astype(BF16)
    rw_lo = (rw - rw_hi.astype(F32)).astype(BF16)
    tt = TOKEN_TILE
    n_tiles = n // tt
    tri = jnp.tril(jnp.ones((tt, tt), BF16), -1)
    triu = jnp.triu(jnp.ones((N_EXPERTS, N_EXPERTS), BF16), 1)
    x1, h2, gates, lrow, cnt, lstart, tbase = _post_mixer(
        ret, att, x, mod, lng, lnb, w_o[0].astype(BF16), row(ln1_g[0]), row(ln1_b[0]),
        rw_hi, rw_lo, row(router_b[0]), tri, triu)

    tm = EXPERT_ROWS
    cnt, lstart, tbase = cnt[:, 0], lstart[:, 0], tbase[:, 0]
    totals = tbase[-1] + cnt[-1]
    exp_rows = (totals + tm - 1) // tm * tm
    exp_end = jnp.cumsum(exp_rows)
    exp_start = exp_end - exp_rows
    dst = (exp_start[None, :] + tbase).reshape(-1)
    max_rows = n * TOP_K + n_tiles * N_EXPERTS * (SEG_ALIGN - 1) + N_EXPERTS * (tm - 1)
    n_blocks = -(-max_rows // tm)
    block_start = jnp.arange(n_blocks, dtype=jnp.int32) * tm
    block_expert = jnp.minimum(jnp.sum(exp_end[None, :] <= block_start[:, None], axis=1),
                               N_EXPERTS - 1).astype(jnp.int32)
    n_active = (exp_end[-1:] // tm).astype(jnp.int32)
    cnt_flat, lstart_flat = cnt.reshape(-1), lstart.reshape(-1)
    lrow_t = lrow.reshape(n_tiles, tt, TOP_K).transpose(0, 2, 1)

    x_sorted = _dispatch(cnt_flat, lstart_flat, dst, exp_rows - totals, exp_start + totals, n_active,
                         h2, lrow_t, n_blocks * tm)
    y_sorted = _experts(block_expert, n_active, x_sorted, w_up[0], b_up[0], w_down[0], b_down[0])
    out = _combine(cnt_flat, lstart_flat, dst, lrow, gates, x1, mod, row(ln2_g[0]), row(ln2_b[0]),
                   y_sorted, t)
    return out.reshape(b, t, d)
```

```python
import functools

import jax
import jax.numpy as jnp
from jax import lax
from jax.experimental import pallas as pl
from jax.experimental.pallas import tpu as pltpu

GRID_W = 64
HEAD_DIM = 64
ROPE_BASE = 10000.0
LN_EPS = 1e-6
RET_HEADS = 4
RET_DK = 64
RET_DV = 128
RET_CHUNK = 128
ATT_HEADS = 8
ATT_KV_HEADS = 2
ATT_GROUP = ATT_HEADS // ATT_KV_HEADS
N_EXPERTS = 32
TOP_K = 4
SWIGLU_LIMIT = 7.0
SWIGLU_ALPHA = 1.702
DEPTH = 1
DN_ALPHA = (2.0 * DEPTH) ** 0.25
LOG2_E = 1.4426950408889634

RET_Q_W = RET_HEADS * RET_DK
RET_V_W = RET_HEADS * RET_DV
ATT_Q_W = ATT_HEADS * HEAD_DIM
ATT_K_W = ATT_KV_HEADS * HEAD_DIM

LANES = 128
VMEM_LIMIT_BYTES = 56 * 1024 * 1024

TOKEN_TILE = 512
ATT_Q_TILE = 256
EXPERT_ROWS = 512
SEG_ALIGN = 16
SEG_SIZES = (512, 256, 128, 64, 32, 16)
TAIL_SIZES = tuple(s for s in SEG_SIZES if s < EXPERT_ROWS)
TILE_SORTED_ROWS = 2560
PERM_CHUNK = 512

F32 = jnp.float32
BF16 = jnp.bfloat16


def _cparams(*sem):
    return pltpu.CompilerParams(dimension_semantics=sem, vmem_limit_bytes=VMEM_LIMIT_BYTES)


def _layer_norm(x, g, b):
    mu = jnp.mean(x, -1, keepdims=True)
    xc = x - mu
    var = jnp.mean(xc * xc, -1, keepdims=True)
    return xc * lax.rsqrt(var + LN_EPS) * g + b


def _silu(x):
    return x * (1.0 / (1.0 + jnp.exp(-x)))


def _ada_kernel(c_ref, w_ref, b_ref, o_ref):
    a = _silu(c_ref[...])
    o_ref[...] = jnp.dot(a, w_ref[...], preferred_element_type=F32,
                         precision=lax.Precision.HIGHEST) + b_ref[...]


def _ada_mod(cc, w, b):
    m, d = cc.shape
    n = w.shape[1]
    tn = 1536
    return pl.pallas_call(
        _ada_kernel,
        grid=(n // tn,),
        in_specs=[pl.BlockSpec((m, d), lambda j: (0, 0)),
                  pl.BlockSpec((d, tn), lambda j: (0, j)),
                  pl.BlockSpec((1, tn), lambda j: (0, j))],
        out_specs=pl.BlockSpec((m, tn), lambda j: (0, j)),
        out_shape=jax.ShapeDtypeStruct((m, n), F32),
        compiler_params=_cparams("arbitrary"),
        name="ada_mod",
    )(cc, w, b)


def _rope(x, cos, sinn, sinp):
    outs = []
    for j in range(x.shape[1] // LANES):
        xg = x[:, j * LANES:(j + 1) * LANES]
        outs.append(xg * cos + pltpu.roll(xg, LANES - 16, 1) * sinn + pltpu.roll(xg, 16, 1) * sinp)
    return outs[0] if len(outs) == 1 else jnp.concatenate(outs, axis=1)


def _head_rms(x, seg, gain):
    outs = []
    for j in range(x.shape[1] // LANES):
        xg = x[:, j * LANES:(j + 1) * LANES]
        sq = xg * xg
        hi = sq.astype(BF16)
        lo = (sq - hi.astype(F32)).astype(BF16)
        ms = (jnp.dot(hi, seg, preferred_element_type=F32)
              + jnp.dot(lo, seg, preferred_element_type=F32)) * (1.0 / HEAD_DIM)
        outs.append(xg * lax.rsqrt(ms + LN_EPS) * gain[:, j * LANES:(j + 1) * LANES])
    return outs[0] if len(outs) == 1 else jnp.concatenate(outs, axis=1)


def _in_proj_kernel(*refs, rope):
    if rope:
        (x_ref, mod_ref, lng_ref, lnb_ref, w_ref, seg_ref, qg_ref, kg_ref, cos_ref, sinn_ref, sinp_ref,
         rq_ref, rk_ref, rv_ref, rg_ref, aq_ref, akt_ref, av_ref) = refs
    else:
        (x_ref, mod_ref, lng_ref, lnb_ref, w_ref, seg_ref, qg_ref, kg_ref,
         rq_ref, rk_ref, rv_ref, rg_ref, aq_ref, akt_ref, av_ref) = refs
    x = x_ref[0]
    xn = _layer_norm(x, lng_ref[...], lnb_ref[...])
    h = xn * (1.0 + mod_ref[0, 1:2, :]) + mod_ref[0, 0:1, :]
    u = jnp.dot(h.astype(BF16), w_ref[...], preferred_element_type=F32)
    o = 0
    rq = u[:, o:o + RET_Q_W]; o += RET_Q_W
    rk = u[:, o:o + RET_Q_W]; o += RET_Q_W
    rv = u[:, o:o + RET_V_W]; o += RET_V_W
    rg = u[:, o:o + RET_V_W]; o += RET_V_W
    aq = u[:, o:o + ATT_Q_W]; o += ATT_Q_W
    ak = u[:, o:o + ATT_K_W]; o += ATT_K_W
    av = u[:, o:o + ATT_K_W]
    seg = seg_ref[...]
    aq = _head_rms(aq, seg, qg_ref[...])
    ak = _head_rms(ak, seg, kg_ref[...])
    if rope:
        cos, sinn, sinp = cos_ref[...], sinn_ref[...], sinp_ref[...]
        rq = _rope(rq, cos, sinn, sinp)
        rk = _rope(rk, cos, sinn, sinp)
        aq = _rope(aq, cos, sinn, sinp)
        ak = _rope(ak, cos, sinn, sinp)
    rq_ref[0] = rq
    rk_ref[0] = rk * (RET_DK ** -0.5)
    rv_ref[0] = rv.astype(BF16)
    rg_ref[0] = rg.astype(BF16)
    aq_ref[0] = (aq * (HEAD_DIM ** -0.5 * LOG2_E)).astype(BF16)
    akt_ref[0] = ak.T.astype(BF16)
    av_ref[0] = av.astype(BF16)


def _in_proj(x, mod, lng, lnb, w_bf, seg, qg, kg, tables):
    b, t, d = x.shape
    tt = min(TOKEN_TILE, t)
    rope = tables is not None
    mod_b = mod.shape[0]
    mod_map = (lambda bi, i: (bi, 0, 0)) if mod_b > 1 else (lambda bi, i: (0, 0, 0))
    const2 = lambda bi, i: (0, 0)
    tok3 = lambda bi, i: (bi, i, 0)
    in_specs = [pl.BlockSpec((1, tt, d), tok3),
                pl.BlockSpec((1, 6, d), mod_map),
                pl.BlockSpec((1, d), const2),
                pl.BlockSpec((1, d), const2),
                pl.BlockSpec(w_bf.shape, const2),
                pl.BlockSpec(seg.shape, const2),
                pl.BlockSpec(qg.shape, const2),
                pl.BlockSpec(kg.shape, const2)]
    args = [x, mod, lng, lnb, w_bf, seg, qg, kg]
    if rope:
        in_specs += [pl.BlockSpec((tt, LANES), lambda bi, i: (i, 0))] * 3
        args += list(tables)
    out_shape = [jax.ShapeDtypeStruct((b, t, RET_Q_W), F32),
                 jax.ShapeDtypeStruct((b, t, RET_Q_W), F32),
                 jax.ShapeDtypeStruct((b, t, RET_V_W), BF16),
                 jax.ShapeDtypeStruct((b, t, RET_V_W), BF16),
                 jax.ShapeDtypeStruct((b, t, ATT_Q_W), BF16),
                 jax.ShapeDtypeStruct((b, ATT_K_W, t), BF16),
                 jax.ShapeDtypeStruct((b, t, ATT_K_W), BF16)]
    out_specs = [pl.BlockSpec((1, tt, RET_Q_W), tok3),
                 pl.BlockSpec((1, tt, RET_Q_W), tok3),
                 pl.BlockSpec((1, tt, RET_V_W), tok3),
                 pl.BlockSpec((1, tt, RET_V_W), tok3),
                 pl.BlockSpec((1, tt, ATT_Q_W), tok3),
                 pl.BlockSpec((1, ATT_K_W, tt), lambda bi, i: (bi, 0, i)),
                 pl.BlockSpec((1, tt, ATT_K_W), tok3)]
    return pl.pallas_call(
        functools.partial(_in_proj_kernel, rope=rope),
        grid=(b, t // tt),
        in_specs=in_specs,
        out_specs=out_specs,
        out_shape=out_shape,
        compiler_params=_cparams("parallel", "parallel"),
        name="in_proj_rope" if rope else "in_proj_ctx",
    )(*args)


def _retention_kernel(lgf_ref, lgb_ref, q_ref, k_ref, v_ref, g_ref, kc_ref, vc_ref, gng_ref, gnb_ref,
                      o_ref, kvf_ref, kvb_ref, sf_ref, sb_ref):
    pair = pl.program_id(1)
    t = q_ref.shape[1]
    n_chunks = t // RET_CHUNK
    ctx_len = kc_ref.shape[1]
    c = RET_CHUNK
    pos = lax.broadcasted_iota(jnp.int32, (c, 1), 0).astype(F32)
    row = lax.broadcasted_iota(jnp.int32, (c, c), 0)
    col = lax.broadcasted_iota(jnp.int32, (c, c), 1)
    diff = (row - col).astype(F32)
    cpos = lax.broadcasted_iota(jnp.int32, (ctx_len, 1), 0).astype(F32)
    tn = (((0,), (0,)), ((), ()))
    nt = (((1,), (1,)), ((), ()))

    heads = []
    for hh in range(2):
        head = pair * 2 + hh
        lgf = lgf_ref[0, head]
        lgb = lgb_ref[0, head]
        heads.append(dict(
            ql=slice(hh * RET_DK, (hh + 1) * RET_DK),
            vl=slice(hh * RET_DV, (hh + 1) * RET_DV),
            lgf=lgf, lgb=lgb,
            kdf=jnp.exp(lgf * (c - 1.0 - pos)),
            kdb=jnp.exp(lgb * pos),
            qdf=jnp.exp(lgf * (pos + 1.0)),
            qdb=jnp.exp(lgb * (c - pos)),
            decay=(jnp.where(diff >= 0, jnp.exp(lgf * jnp.maximum(diff, 0.0)), 0.0)
                   + jnp.where(diff <= 0, jnp.exp(lgb * jnp.maximum(-diff, 0.0)), 0.0))))

    def kv_body(n, carry):
        r = pl.ds(pl.multiple_of(n * c, c), c)
        for hh, hd in enumerate(heads):
            kc = k_ref[0, r, hd["ql"]]
            vc = v_ref[0, r, hd["vl"]]
            kvf_ref[hh, n] = lax.dot_general((kc * hd["kdf"]).astype(BF16), vc, tn, preferred_element_type=F32)
            kvb_ref[hh, n] = lax.dot_general((kc * hd["kdb"]).astype(BF16), vc, tn, preferred_element_type=F32)
        return carry

    lax.fori_loop(0, n_chunks, kv_body, 0, unroll=2)

    for hh, hd in enumerate(heads):
        lgf, lgb = hd["lgf"], hd["lgb"]
        kctx = kc_ref[0, :, hd["ql"]]
        vctx = vc_ref[0, :, hd["vl"]]
        s_f = lax.dot_general((kctx * jnp.exp(lgf * (ctx_len - 1.0 - cpos))).astype(BF16), vctx, tn,
                              preferred_element_type=F32)
        s_b = lax.dot_general((kctx * jnp.exp(lgb * cpos)).astype(BF16), vctx, tn,
                              preferred_element_type=F32)
        gfc = jnp.exp(lgf * c)
        gbc = jnp.exp(lgb * c)
        for n in range(n_chunks):
            sf_ref[hh, n] = s_f.astype(BF16)
            s_f = gfc * s_f + kvf_ref[hh, n]
        for n in range(n_chunks - 1, -1, -1):
            sb_ref[hh, n] = s_b.astype(BF16)
            s_b = gbc * s_b + kvb_ref[hh, n]

    def out_body(n, carry):
        r = pl.ds(pl.multiple_of(n * c, c), c)
        for hh, hd in enumerate(heads):
            ql, vl = hd["ql"], hd["vl"]
            qc = q_ref[0, r, ql]
            kc = k_ref[0, r, ql]
            vc = v_ref[0, r, vl]
            s = lax.dot_general(qc.astype(BF16), kc.astype(BF16), nt, preferred_element_type=F32)
            y = jnp.dot((s * hd["decay"]).astype(BF16), vc, preferred_element_type=F32)
            y += jnp.dot((qc * hd["qdf"]).astype(BF16), sf_ref[hh, n], preferred_element_type=F32)
            y += jnp.dot((qc * hd["qdb"]).astype(BF16), sb_ref[hh, n], preferred_element_type=F32)
            mu = jnp.mean(y, -1, keepdims=True)
            yc = y - mu
            var = jnp.mean(yc * yc, -1, keepdims=True)
            yn = yc * lax.rsqrt(var + LN_EPS) * gng_ref[:, vl] + gnb_ref[:, vl]
            o_ref[0, r, vl] = (yn * _silu(g_ref[0, r, vl].astype(F32))).astype(o_ref.dtype)
        return carry

    lax.fori_loop(0, n_chunks, out_body, 0, unroll=2)


def _retention(lgf, lgb, rq, rk, rv, rg, rkc, rvc, gng, gnb):
    b, t, _ = rq.shape
    ctx_len = rkc.shape[1]
    n_chunks = t // RET_CHUNK
    pair3 = lambda bi, p: (bi, 0, p)
    smem = pl.BlockSpec(memory_space=pltpu.SMEM)
    state = (2, n_chunks, RET_DK, RET_DV)
    return pl.pallas_call(
        _retention_kernel,
        grid=(b, RET_HEADS // 2),
        in_specs=[smem, smem,
                  pl.BlockSpec((1, t, 2 * RET_DK), pair3),
                  pl.BlockSpec((1, t, 2 * RET_DK), pair3),
                  pl.BlockSpec((1, t, 2 * RET_DV), pair3),
                  pl.BlockSpec((1, t, 2 * RET_DV), pair3),
                  pl.BlockSpec((1, ctx_len, 2 * RET_DK), pair3),
                  pl.BlockSpec((1, ctx_len, 2 * RET_DV), pair3),
                  pl.BlockSpec((1, 2 * RET_DV), lambda bi, p: (0, p)),
                  pl.BlockSpec((1, 2 * RET_DV), lambda bi, p: (0, p))],
        out_specs=pl.BlockSpec((1, t, 2 * RET_DV), pair3),
        out_shape=jax.ShapeDtypeStruct((b, t, RET_V_W), BF16),
        scratch_shapes=[pltpu.VMEM(state, F32), pltpu.VMEM(state, F32),
                        pltpu.VMEM(state, BF16), pltpu.VMEM(state, BF16)],
        compiler_params=_cparams("parallel", "parallel"),
        name="retention",
    )(lgf, lgb, rq, rk, rv, rg, rkc, rvc, gng, gnb)


def _attention_kernel(q_ref, kt_ref, v_ref, o_ref, s_buf, p_buf):
    tq = q_ref.shape[1]
    lane = lax.broadcasted_iota(jnp.int32, (tq, LANES), 1)

    def scores(h):
        kvh = h // ATT_GROUP
        s_buf[h % 2] = jnp.dot(q_ref[0, :, h * HEAD_DIM:(h + 1) * HEAD_DIM],
                               kt_ref[0, kvh * HEAD_DIM:(kvh + 1) * HEAD_DIM, :], preferred_element_type=F32)

    def numerators(h):
        s = s_buf[h % 2]
        p_buf[h % 2] = jnp.exp2(s - jnp.max(s, -1, keepdims=True)).astype(BF16)

    def values(h):
        ov = jnp.dot(p_buf[h % 2], v_ref[0], preferred_element_type=F32)
        o = ov[:, :LANES] / ov[:, LANES:LANES + 1]
        if h // ATT_GROUP != h % 2:
            o = pltpu.roll(o, HEAD_DIM, 1)
        return o

    scores(0)
    scores(1)
    numerators(0)
    even = None
    for h in range(ATT_HEADS):
        if h + 2 < ATT_HEADS:
            scores(h + 2)
        if h + 1 < ATT_HEADS:
            numerators(h + 1)
        o = values(h)
        if h % 2 == 0:
            even = o
        else:
            col = (h // 2) * LANES
            o_ref[0, :, col:col + LANES] = jnp.where(lane < HEAD_DIM, even, o).astype(o_ref.dtype)


def _attention(aq, akt, av_ones):
    b, t, _ = aq.shape
    tk = akt.shape[2]
    tq = ATT_Q_TILE
    return pl.pallas_call(
        _attention_kernel,
        grid=(b, t // tq),
        in_specs=[pl.BlockSpec((1, tq, ATT_Q_W), lambda bi, i: (bi, i, 0)),
                  pl.BlockSpec((1, ATT_K_W, tk), lambda bi, i: (bi, 0, 0)),
                  pl.BlockSpec((1, tk, av_ones.shape[2]), lambda bi, i: (bi, 0, 0))],
        out_specs=pl.BlockSpec((1, tq, ATT_Q_W), lambda bi, i: (bi, i, 0)),
        out_shape=jax.ShapeDtypeStruct((b, t, ATT_Q_W), BF16),
        scratch_shapes=[pltpu.VMEM((2, tq, tk), F32), pltpu.VMEM((2, tq, tk), BF16)],
        compiler_params=_cparams("parallel", "parallel"),
        name="attention",
    )(aq, akt, av_ones)


def _post_mixer_kernel(ret_ref, att_ref, x_ref, mod_ref, lng_ref, lnb_ref, wo_ref, l1g_ref, l1b_ref,
                       rwh_ref, rwl_ref, rb_ref, tri_ref, triu_ref,
                       x1_ref, h2_ref, gate_ref, lrow_ref, cnt_ref, lstart_ref, tbase_ref, base_acc):
    first = (pl.program_id(0) == 0) & (pl.program_id(1) == 0)

    @pl.when(first)
    def _():
        base_acc[...] = jnp.zeros_like(base_acc)

    half = ret_ref.shape[2]
    y = (jnp.dot(ret_ref[0], wo_ref[0:half, :], preferred_element_type=F32)
         + jnp.dot(att_ref[0], wo_ref[half:, :], preferred_element_type=F32))
    xn = _layer_norm(x_ref[0], lng_ref[...], lnb_ref[...])
    x1 = _layer_norm(DN_ALPHA * xn + mod_ref[0, 2:3, :] * y, l1g_ref[...], l1b_ref[...])
    x1_ref[...] = x1
    h2 = x1 * (1.0 + mod_ref[0, 4:5, :]) + mod_ref[0, 3:4, :]
    h_hi = h2.astype(BF16)
    h2_ref[...] = h_hi

    h_lo = (h2 - h_hi.astype(F32)).astype(BF16)
    logits = (jnp.dot(h_hi, rwh_ref[...], preferred_element_type=F32)
              + jnp.dot(h_lo, rwh_ref[...], preferred_element_type=F32)
              + jnp.dot(h_hi, rwl_ref[...], preferred_element_type=F32)) + rb_ref[...]

    tt = logits.shape[0]
    lane = lax.broadcasted_iota(jnp.int32, (tt, N_EXPERTS), 1)
    sels, vals = [], []
    onehot = jnp.zeros((tt, N_EXPERTS), F32)
    work = logits
    for _ in range(TOP_K):
        m = jnp.max(work, -1, keepdims=True)
        idx = jnp.min(jnp.where(work == m, lane, N_EXPERTS), -1, keepdims=True)
        sel = lane == idx
        sels.append(sel)
        vals.append(m)
        onehot = onehot + sel.astype(F32)
        work = jnp.where(sel, -jnp.inf, work)
    exps = [jnp.exp(v - vals[0]) for v in vals]
    denom = exps[0] + exps[1] + exps[2] + exps[3]

    count = jnp.sum(onehot, 0, keepdims=True)
    units = jnp.floor((count + (SEG_ALIGN - 1.0)) * (1.0 / SEG_ALIGN))
    lstart = SEG_ALIGN * jnp.dot(jnp.broadcast_to(units, (8, N_EXPERTS)).astype(BF16), triu_ref[...],
                                 preferred_element_type=F32)[0:1, :]
    rank = jnp.dot(tri_ref[...], onehot.astype(BF16), preferred_element_type=F32) + lstart

    lane4 = lax.broadcasted_iota(jnp.int32, (tt, TOP_K), 1)
    gate_out = jnp.zeros((tt, TOP_K), F32)
    lrow_out = jnp.zeros((tt, TOP_K), F32)
    for k in range(TOP_K):
        pk = jnp.sum(jnp.where(sels[k], rank, 0.0), -1, keepdims=True)
        gate_out = jnp.where(lane4 == k, exps[k] / denom, gate_out)
        lrow_out = jnp.where(lane4 == k, pk, lrow_out)
    gate_ref[...] = gate_out
    lrow_ref[...] = lrow_out.astype(jnp.int32)
    cnt_ref[0] = (units * SEG_ALIGN).astype(jnp.int32)
    lstart_ref[0] = lstart.astype(jnp.int32)
    tbase_ref[0] = base_acc[...].astype(jnp.int32)
    base_acc[...] += units * SEG_ALIGN


def _post_mixer(ret, att, x, mod, lng, lnb, wo_bf, l1g, l1b, rw_hi, rw_lo, rb, tri, triu):
    b, t, d = x.shape
    tt = TOKEN_TILE
    nt = t // tt
    n = b * t
    const2 = lambda bi, i: (0, 0)
    tok3 = lambda bi, i: (bi, i, 0)
    flat = lambda bi, i: (bi * nt + i, 0)
    tile3 = lambda bi, i: (bi * nt + i, 0, 0)
    vec = pl.BlockSpec((1, d), const2)
    table = jax.ShapeDtypeStruct((b * nt, 1, N_EXPERTS), jnp.int32)
    return pl.pallas_call(
        _post_mixer_kernel,
        grid=(b, nt),
        in_specs=[pl.BlockSpec((1, tt, ret.shape[2]), tok3),
                  pl.BlockSpec((1, tt, att.shape[2]), tok3),
                  pl.BlockSpec((1, tt, d), tok3),
                  pl.BlockSpec((1, 6, d), lambda bi, i: (bi, 0, 0)),
                  vec, vec,
                  pl.BlockSpec(wo_bf.shape, const2),
                  vec, vec,
                  pl.BlockSpec(rw_hi.shape, const2),
                  pl.BlockSpec(rw_lo.shape, const2),
                  pl.BlockSpec((1, N_EXPERTS), const2),
                  pl.BlockSpec((tt, tt), const2),
                  pl.BlockSpec((N_EXPERTS, N_EXPERTS), const2)],
        out_specs=[pl.BlockSpec((tt, d), flat),
                   pl.BlockSpec((tt, d), flat),
                   pl.BlockSpec((tt, TOP_K), flat),
                   pl.BlockSpec((tt, TOP_K), flat),
                   pl.BlockSpec((1, 1, N_EXPERTS), tile3),
                   pl.BlockSpec((1, 1, N_EXPERTS), tile3),
                   pl.BlockSpec((1, 1, N_EXPERTS), tile3)],
        out_shape=[jax.ShapeDtypeStruct((n, d), F32),
                   jax.ShapeDtypeStruct((n, d), BF16),
                   jax.ShapeDtypeStruct((n, TOP_K), F32),
                   jax.ShapeDtypeStruct((n, TOP_K), jnp.int32),
                   table, table, table],
        scratch_shapes=[pltpu.VMEM((1, N_EXPERTS), F32)],
        compiler_params=_cparams("arbitrary", "arbitrary"),
        name="post_mixer",
    )(ret, att, x, mod, lng, lnb, wo_bf, l1g, l1b, rw_hi, rw_lo, rb, tri, triu)


def _segment_copies(cnt_ref, lstart_ref, dst_ref, tile, make_copy, act):
    def body(e, carry):
        t = tile * N_EXPERTS + e
        n = cnt_ref[t]
        ls = lstart_ref[t]
        ds = dst_ref[t]
        off = jnp.int32(0)
        for size in SEG_SIZES:
            bit = n & size

            @pl.when(bit != 0)
            def _():
                copy = make_copy(pl.multiple_of(ls + off, SEG_ALIGN), pl.multiple_of(ds + off, SEG_ALIGN), size)
                getattr(copy, act)()
            off = off + bit
        return carry
    lax.fori_loop(0, N_EXPERTS, body, 0)


def _zero_fill_copies(tail_len_ref, tail_dst_ref, nact_ref, zbuf, x_hbm, zsem, act):
    tm = zbuf.shape[0]

    def tail_body(e, carry):
        n = tail_len_ref[e]
        ds = tail_dst_ref[e]
        off = jnp.int32(0)
        for size in TAIL_SIZES:
            bit = n & size

            @pl.when(bit != 0)
            def _():
                copy = pltpu.make_async_copy(zbuf.at[pl.ds(0, size)],
                                             x_hbm.at[pl.ds(pl.multiple_of(ds + off, SEG_ALIGN), size)], zsem)
                getattr(copy, act)()
            off = off + bit
        return carry
    lax.fori_loop(0, N_EXPERTS, tail_body, 0)

    def block_body(j, carry):
        copy = pltpu.make_async_copy(zbuf, x_hbm.at[pl.ds(pl.multiple_of(j * tm, tm), tm)], zsem)
        getattr(copy, act)()
        return carry
    lax.fori_loop(nact_ref[0], x_hbm.shape[0] // tm, block_body, 0)


def _dispatch_kernel(cnt_ref, lstart_ref, dst_ref, tail_len_ref, tail_dst_ref, nact_ref,
                     h_ref, lrow_ref, x_hbm, sbuf, sem, zbuf, zsem):
    i = pl.program_id(0)
    last = pl.num_programs(0) - 1
    slot = i % 2
    rows = sbuf.shape[1]
    tt = h_ref.shape[0]

    def make_copy(tile_slot):
        def mk(ls, ds, size):
            return pltpu.make_async_copy(sbuf.at[tile_slot, pl.ds(ls, size)], x_hbm.at[pl.ds(ds, size)],
                                         sem.at[tile_slot])
        return mk

    @pl.when(i == 0)
    def _():
        zbuf[...] = jnp.zeros_like(zbuf)
        _zero_fill_copies(tail_len_ref, tail_dst_ref, nact_ref, zbuf, x_hbm, zsem, "start")

    def perm(j):
        r_iota = lax.broadcasted_iota(jnp.int32, (PERM_CHUNK, tt), 0) + j * PERM_CHUNK
        hit = r_iota == lrow_ref[0, 0:1, :]
        for k in range(1, TOP_K):
            hit = hit | (r_iota == lrow_ref[0, k:k + 1, :])
        return jnp.where(hit, 1.0, 0.0).astype(BF16)

    p_next = perm(0)
    for j in range(rows // PERM_CHUNK):
        p_cur = p_next
        if (j + 1) * PERM_CHUNK < rows:
            p_next = perm(j + 1)
        sbuf[slot, j * PERM_CHUNK:(j + 1) * PERM_CHUNK, :] = jnp.dot(
            p_cur, h_ref[...], preferred_element_type=F32).astype(BF16)
    _segment_copies(cnt_ref, lstart_ref, dst_ref, i, make_copy(slot), "start")

    @pl.when(i > 0)
    def _():
        _segment_copies(cnt_ref, lstart_ref, dst_ref, i - 1, make_copy(1 - slot), "wait")

    @pl.when(i == last)
    def _():
        _segment_copies(cnt_ref, lstart_ref, dst_ref, i, make_copy(slot), "wait")
        _zero_fill_copies(tail_len_ref, tail_dst_ref, nact_ref, zbuf, x_hbm, zsem, "wait")


def _dispatch(cnt, lstart, dst, tail_len, tail_dst, n_active, h2, lrow_t, n_rows):
    n, d = h2.shape
    tt = TOKEN_TILE
    grid_spec = pltpu.PrefetchScalarGridSpec(
        num_scalar_prefetch=6,
        grid=(n // tt,),
        in_specs=[pl.BlockSpec((tt, d), lambda i, *_: (i, 0)),
                  pl.BlockSpec((1, TOP_K, tt), lambda i, *_: (i, 0, 0))],
        out_specs=pl.BlockSpec(memory_space=pl.ANY),
        scratch_shapes=[pltpu.VMEM((2, TILE_SORTED_ROWS, d), BF16),
                        pltpu.SemaphoreType.DMA((2,)),
                        pltpu.VMEM((EXPERT_ROWS, d), BF16),
                        pltpu.SemaphoreType.DMA(())])
    return pl.pallas_call(
        _dispatch_kernel,
        grid_spec=grid_spec,
        out_shape=jax.ShapeDtypeStruct((n_rows, d), BF16),
        compiler_params=_cparams("arbitrary"),
        name="dispatch",
    )(cnt, lstart, dst, tail_len, tail_dst, n_active, h2, lrow_t)


def _experts_kernel(be_ref, nact_ref, x_ref, wup_ref, bup_ref, wdn_ref, bdn_ref, y_ref,
                    wup_bf, wdn_bf):
    j = pl.program_id(0)
    dff = wdn_ref.shape[1]

    @pl.when(j < nact_ref[0])
    def _():
        prev = be_ref[jnp.maximum(j - 1, 0)]

        @pl.when((j == 0) | (be_ref[j] != prev))
        def _():
            wup_bf[...] = wup_ref[0].astype(BF16)
            wdn_bf[...] = wdn_ref[0].astype(BF16)

        u = jnp.dot(x_ref[...], wup_bf[...], preferred_element_type=F32) + bup_ref[0]
        glu = jnp.minimum(u[:, :dff], SWIGLU_LIMIT)
        lin = jnp.clip(u[:, dff:], -SWIGLU_LIMIT, SWIGLU_LIMIT)
        act = glu * (1.0 / (1.0 + jnp.exp(-SWIGLU_ALPHA * glu))) * (lin + 1.0)
        y = jnp.dot(act.astype(BF16), wdn_bf[...], preferred_element_type=F32) + bdn_ref[0]
        y_ref[...] = y.astype(y_ref.dtype)

    @pl.when(j >= nact_ref[0])
    def _():
        y_ref[...] = jnp.zeros_like(y_ref)


def _experts(block_expert, n_active, x_sorted, w_up, b_up, w_down, b_down):
    n_blocks = block_expert.shape[0]
    tm = EXPERT_ROWS
    d = x_sorted.shape[1]
    e, _, up_w = w_up.shape
    dff = w_down.shape[1]
    by_expert = lambda j, be, na: (be[j], 0, 0)
    x_map = lambda j, be, na: (jnp.minimum(j, na[0] - 1), 0)
    grid_spec = pltpu.PrefetchScalarGridSpec(
        num_scalar_prefetch=2,
        grid=(n_blocks,),
        in_specs=[pl.BlockSpec((tm, d), x_map),
                  pl.BlockSpec((1, d, up_w), by_expert),
                  pl.BlockSpec((1, 1, up_w), by_expert),
                  pl.BlockSpec((1, dff, d), by_expert),
                  pl.BlockSpec((1, 1, d), by_expert)],
        out_specs=pl.BlockSpec((tm, d), lambda j, be, na: (j, 0)),
        scratch_shapes=[pltpu.VMEM((d, up_w), BF16),
                        pltpu.VMEM((dff, d), BF16)])
    return pl.pallas_call(
        _experts_kernel,
        grid_spec=grid_spec,
        out_shape=jax.ShapeDtypeStruct((n_blocks * tm, d), BF16),
        compiler_params=_cparams("arbitrary"),
        name="experts",
    )(block_expert, n_active, x_sorted, w_up, b_up.reshape(e, 1, up_w), w_down,
      b_down.reshape(e, 1, d))


def _combine_kernel(cnt_ref, lstart_ref, dst_ref, lrow_ref, gate_ref, x1_ref, mod_ref, g_ref, b_ref,
                    y_hbm, o_ref, ybuf, sem):
    i = pl.program_id(0)
    last = pl.num_programs(0) - 1
    slot = i % 2
    rows = ybuf.shape[1]
    tt = x1_ref.shape[0]

    def make_copy(tile_slot):
        def mk(ls, ds, size):
            return pltpu.make_async_copy(y_hbm.at[pl.ds(ds, size)], ybuf.at[tile_slot, pl.ds(ls, size)],
                                         sem.at[tile_slot])
        return mk

    @pl.when(i == 0)
    def _():
        ybuf[...] = jnp.zeros_like(ybuf)
        _segment_copies(cnt_ref, lstart_ref, dst_ref, i, make_copy(slot), "start")

    @pl.when(i < last)
    def _():
        _segment_copies(cnt_ref, lstart_ref, dst_ref, i + 1, make_copy(1 - slot), "start")

    def weights(j):
        c_iota = lax.broadcasted_iota(jnp.int32, (tt, PERM_CHUNK), 1) + j * PERM_CHUNK
        w = jnp.zeros((tt, PERM_CHUNK), F32)
        for k in range(TOP_K):
            w = w + jnp.where(c_iota == lrow_ref[:, k:k + 1], gate_ref[:, k:k + 1], 0.0)
        return w.astype(BF16)

    w_next = weights(0)
    _segment_copies(cnt_ref, lstart_ref, dst_ref, i, make_copy(slot), "wait")
    f = None
    for j in range(rows // PERM_CHUNK):
        w_cur = w_next
        if (j + 1) * PERM_CHUNK < rows:
            w_next = weights(j + 1)
        part = jnp.dot(w_cur, ybuf[slot, j * PERM_CHUNK:(j + 1) * PERM_CHUNK, :], preferred_element_type=F32)
        f = part if f is None else f + part
    o_ref[...] = _layer_norm(DN_ALPHA * x1_ref[...] + mod_ref[0, 5:6, :] * f, g_ref[...], b_ref[...])


def _combine(cnt, lstart, dst, lrow, gates, x1, mod, g, b, y_sorted, tokens_per_sample):
    n, d = x1.shape
    tt = TOKEN_TILE
    tiles_per_sample = tokens_per_sample // tt
    const2 = lambda i, *_: (0, 0)
    tok = lambda i, *_: (i, 0)
    grid_spec = pltpu.PrefetchScalarGridSpec(
        num_scalar_prefetch=3,
        grid=(n // tt,),
        in_specs=[pl.BlockSpec((tt, TOP_K), tok),
                  pl.BlockSpec((tt, TOP_K), tok),
                  pl.BlockSpec((tt, d), tok),
                  pl.BlockSpec((1, 6, d), lambda i, *_: (i // tiles_per_sample, 0, 0)),
                  pl.BlockSpec((1, d), const2),
                  pl.BlockSpec((1, d), const2),
                  pl.BlockSpec(memory_space=pl.ANY)],
        out_specs=pl.BlockSpec((tt, d), tok),
        scratch_shapes=[pltpu.VMEM((2, TILE_SORTED_ROWS, d), BF16),
                        pltpu.SemaphoreType.DMA((2,))])
    return pl.pallas_call(
        _combine_kernel,
        grid_spec=grid_spec,
        out_shape=jax.ShapeDtypeStruct((n, d), F32),
        compiler_params=_cparams("arbitrary"),
        name="combine",
    )(cnt, lstart, dst, lrow, gates, x1, mod, g, b, y_sorted)


def _rope_tables(t):
    rows = jnp.repeat(jnp.arange(t // GRID_W, dtype=jnp.int32), GRID_W).astype(F32)
    cols = jnp.tile(jnp.arange(GRID_W, dtype=jnp.int32), t // GRID_W).astype(F32)
    n_freq = HEAD_DIM // 4
    inv = ROPE_BASE ** (-jnp.arange(n_freq, dtype=F32) / n_freq)
    ang_r = rows[:, None] * inv
    ang_c = cols[:, None] * inv
    ang = jnp.concatenate([ang_r, ang_r, ang_c, ang_c], -1)
    ang = jnp.concatenate([ang, ang], -1)
    cos, sin = jnp.cos(ang), jnp.sin(ang)
    first_half = (jnp.arange(LANES) % 32) < 16
    return cos, jnp.where(first_half, -sin, 0.0), jnp.where(first_half, 0.0, sin)


def kernel(x, c, ctx, c_ctx, ln_in_g, ln_in_b, w_ada, b_ada, w_in, ret_log_decay_f, ret_log_decay_b,
           ret_gn_g, ret_gn_b, q_norm_g, k_norm_g, w_o, ln1_g, ln1_b, router_w, router_b,
           w_up, b_up, w_down, b_down, ln2_g, ln2_b):
    b, t, d = x.shape
    n = b * t
    row = lambda v: v.reshape(1, -1)

    cc = jnp.concatenate([c, c_ctx[None]], 0)
    cc = jnp.pad(cc, ((0, (-cc.shape[0]) % 8), (0, 0)))
    mod_all = _ada_mod(cc, w_ada[0], row(b_ada[0]))
    mod = mod_all[:b].reshape(b, 6, d)
    mod_ctx = mod_all[b:b + 1].reshape(1, 6, d)

    w_in_bf = w_in[0].astype(BF16)
    lane = jnp.arange(LANES)
    seg = (lane[:, None] // HEAD_DIM == lane[None, :] // HEAD_DIM).astype(BF16)
    qg = jnp.tile(q_norm_g[0], ATT_HEADS).reshape(1, -1)
    kg = jnp.tile(k_norm_g[0], ATT_KV_HEADS).reshape(1, -1)
    lng, lnb = row(ln_in_g), row(ln_in_b)

    rq, rk, rv, rg, aq, akt, av = _in_proj(x, mod, lng, lnb, w_in_bf, seg, qg, kg, _rope_tables(t))
    _, rkc, rvc, _, _, aktc, avc = _in_proj(ctx, mod_ctx, lng, lnb, w_in_bf, seg, qg, kg, None)

    ret = _retention(row(ret_log_decay_f[0]), row(ret_log_decay_b[0]), rq, rk, rv, rg, rkc, rvc,
                     row(ret_gn_g[0]), row(ret_gn_b[0]))
    tk = ctx.shape[1] + t
    av_ones = jnp.concatenate([jnp.concatenate([avc, av], 1), jnp.ones((b, tk, LANES), BF16)], 2)
    att = _attention(aq, jnp.concatenate([aktc, akt], 2), av_ones)

    rw = router_w[0]
    rw_hi = rw.astype(BF16)
    rw_lo = (rw - rw_hi.astype(F32)).astype(BF16)
    tt = TOKEN_TILE
    n_tiles = n // tt
    tri = jnp.tril(jnp.ones((tt, tt), BF16), -1)
    triu = jnp.triu(jnp.ones((N_EXPERTS, N_EXPERTS), BF16), 1)
    x1, h2, gates, lrow, cnt, lstart, tbase = _post_mixer(
        ret, att, x, mod, lng, lnb, w_o[0].astype(BF16), row(ln1_g[0]), row(ln1_b[0]),
        rw_hi, rw_lo, row(router_b[0]), tri, triu)

    tm = EXPERT_ROWS
    cnt, lstart, tbase = cnt[:, 0], lstart[:, 0], tbase[:, 0]
    totals = tbase[-1] + cnt[-1]
    exp_rows = (totals + tm - 1) // tm * tm
    exp_end = jnp.cumsum(exp_rows)
    exp_start = exp_end - exp_rows
    dst = (exp_start[None, :] + tbase).reshape(-1)
    max_rows = n * TOP_K + n_tiles * N_EXPERTS * (SEG_ALIGN - 1) + N_EXPERTS * (tm - 1)
    n_blocks = -(-max_rows // tm)
    block_start = jnp.arange(n_blocks, dtype=jnp.int32) * tm
    block_expert = jnp.minimum(jnp.sum(exp_end[None, :] <= block_start[:, None], axis=1),
                               N_EXPERTS - 1).astype(jnp.int32)
    n_active = (exp_end[-1:] // tm).astype(jnp.int32)
    cnt_flat, lstart_flat = cnt.reshape(-1), lstart.reshape(-1)
    lrow_t = lrow.reshape(n_tiles, tt, TOP_K).transpose(0, 2, 1)

    x_sorted = _dispatch(cnt_flat, lstart_flat, dst, exp_rows - totals, exp_start + totals, n_active,
                         h2, lrow_t, n_blocks * tm)
    y_sorted = _experts(block_expert, n_active, x_sorted, w_up[0], b_up[0], w_down[0], b_down[0])
    out = _combine(cnt_flat, lstart_flat, dst, lrow, gates, x1, mod, row(ln2_g[0]), row(ln2_b[0]),
                   y_sorted, t)
    return out.reshape(b, t, d)
```

```python
import functools

import jax
import jax.numpy as jnp
from jax import lax
from jax.experimental import pallas as pl
from jax.experimental.pallas import tpu as pltpu

GRID_W = 64
HEAD_DIM = 64
ROPE_BASE = 10000.0
LN_EPS = 1e-6
RET_HEADS = 4
RET_DK = 64
RET_DV = 128
RET_CHUNK = 128
ATT_HEADS = 8
ATT_KV_HEADS = 2
ATT_GROUP = ATT_HEADS // ATT_KV_HEADS
N_EXPERTS = 32
TOP_K = 4
SWIGLU_LIMIT = 7.0
SWIGLU_ALPHA = 1.702
DEPTH = 1
DN_ALPHA = (2.0 * DEPTH) ** 0.25
LOG2_E = 1.4426950408889634

RET_Q_W = RET_HEADS * RET_DK
RET_V_W = RET_HEADS * RET_DV
ATT_Q_W = ATT_HEADS * HEAD_DIM
ATT_K_W = ATT_KV_HEADS * HEAD_DIM

LANES = 128
VMEM_LIMIT_BYTES = 56 * 1024 * 1024

TOKEN_TILE = 512
ATT_Q_TILE = 256
EXPERT_ROWS = 512
SEG_ALIGN = 16
TILE_SORTED_ROWS = 2560
PERM_CHUNK = 512

F32 = jnp.float32
BF16 = jnp.bfloat16


def _cparams(*sem):
    return pltpu.CompilerParams(dimension_semantics=sem, vmem_limit_bytes=VMEM_LIMIT_BYTES)


def _layer_norm(x, g, b):
    mu = jnp.mean(x, -1, keepdims=True)
    xc = x - mu
    var = jnp.mean(xc * xc, -1, keepdims=True)
    return xc * lax.rsqrt(var + LN_EPS) * g + b


def _silu(x):
    return x * (1.0 / (1.0 + jnp.exp(-x)))


def _ada_kernel(c_ref, w_ref, b_ref, o_ref):
    a = _silu(c_ref[...])
    o_ref[...] = jnp.dot(a, w_ref[...], preferred_element_type=F32,
                         precision=lax.Precision.HIGHEST) + b_ref[...]


def _ada_mod(cc, w, b):
    m, d = cc.shape
    n = w.shape[1]
    tn = 1536
    return pl.pallas_call(
        _ada_kernel,
        grid=(n // tn,),
        in_specs=[pl.BlockSpec((m, d), lambda j: (0, 0)),
                  pl.BlockSpec((d, tn), lambda j: (0, j)),
                  pl.BlockSpec((1, tn), lambda j: (0, j))],
        out_specs=pl.BlockSpec((m, tn), lambda j: (0, j)),
        out_shape=jax.ShapeDtypeStruct((m, n), F32),
        compiler_params=_cparams("arbitrary"),
        name="ada_mod",
    )(cc, w, b)


def _rope(x, cos, sinn, sinp):
    outs = []
    for j in range(x.shape[1] // LANES):
        xg = x[:, j * LANES:(j + 1) * LANES]
        outs.append(xg * cos + pltpu.roll(xg, LANES - 16, 1) * sinn + pltpu.roll(xg, 16, 1) * sinp)
    return outs[0] if len(outs) == 1 else jnp.concatenate(outs, axis=1)


def _head_rms(x, seg, gain):
    outs = []
    for j in range(x.shape[1] // LANES):
        xg = x[:, j * LANES:(j + 1) * LANES]
        sq = xg * xg
        hi = sq.astype(BF16)
        lo = (sq - hi.astype(F32)).astype(BF16)
        ms = (jnp.dot(hi, seg, preferred_element_type=F32)
              + jnp.dot(lo, seg, preferred_element_type=F32)) * (1.0 / HEAD_DIM)
        outs.append(xg * lax.rsqrt(ms + LN_EPS) * gain[:, j * LANES:(j + 1) * LANES])
    return outs[0] if len(outs) == 1 else jnp.concatenate(outs, axis=1)


def _in_proj_kernel(*refs, rope):
    if rope:
        (x_ref, mod_ref, lng_ref, lnb_ref, w_ref, seg_ref, qg_ref, kg_ref, cos_ref, sinn_ref, sinp_ref,
         rq_ref, rk_ref, rv_ref, rg_ref, aq_ref, akt_ref, av_ref) = refs
    else:
        (x_ref, mod_ref, lng_ref, lnb_ref, w_ref, seg_ref, qg_ref, kg_ref,
         rq_ref, rk_ref, rv_ref, rg_ref, aq_ref, akt_ref, av_ref) = refs
    x = x_ref[0]
    xn = _layer_norm(x, lng_ref[...], lnb_ref[...])
    h = xn * (1.0 + mod_ref[0, 1:2, :]) + mod_ref[0, 0:1, :]
    u = jnp.dot(h.astype(BF16), w_ref[...], preferred_element_type=F32)
    o = 0
    rq = u[:, o:o + RET_Q_W]; o += RET_Q_W
    rk = u[:, o:o + RET_Q_W]; o += RET_Q_W
    rv = u[:, o:o + RET_V_W]; o += RET_V_W
    rg = u[:, o:o + RET_V_W]; o += RET_V_W
    aq = u[:, o:o + ATT_Q_W]; o += ATT_Q_W
    ak = u[:, o:o + ATT_K_W]; o += ATT_K_W
    av = u[:, o:o + ATT_K_W]
    seg = seg_ref[...]
    aq = _head_rms(aq, seg, qg_ref[...])
    ak = _head_rms(ak, seg, kg_ref[...])
    if rope:
        cos, sinn, sinp = cos_ref[...], sinn_ref[...], sinp_ref[...]
        rq = _rope(rq, cos, sinn, sinp)
        rk = _rope(rk, cos, sinn, sinp)
        aq = _rope(aq, cos, sinn, sinp)
        ak = _rope(ak, cos, sinn, sinp)
    rq_ref[0] = rq
    rk_ref[0] = rk * (RET_DK ** -0.5)
    rv_ref[0] = rv.astype(BF16)
    rg_ref[0] = rg.astype(BF16)
    aq_ref[0] = (aq * (HEAD_DIM ** -0.5 * LOG2_E)).astype(BF16)
    akt_ref[0] = ak.T.astype(BF16)
    av_ref[0] = av.astype(BF16)


def _in_proj(x, mod, lng, lnb, w_bf, seg, qg, kg, tables):
    b, t, d = x.shape
    tt = min(TOKEN_TILE, t)
    rope = tables is not None
    mod_b = mod.shape[0]
    mod_map = (lambda bi, i: (bi, 0, 0)) if mod_b > 1 else (lambda bi, i: (0, 0, 0))
    const2 = lambda bi, i: (0, 0)
    tok3 = lambda bi, i: (bi, i, 0)
    in_specs = [pl.BlockSpec((1, tt, d), tok3),
                pl.BlockSpec((1, 6, d), mod_map),
                pl.BlockSpec((1, d), const2),
                pl.BlockSpec((1, d), const2),
                pl.BlockSpec(w_bf.shape, const2),
                pl.BlockSpec(seg.shape, const2),
                pl.BlockSpec(qg.shape, const2),
                pl.BlockSpec(kg.shape, const2)]
    args = [x, mod, lng, lnb, w_bf, seg, qg, kg]
    if rope:
        in_specs += [pl.BlockSpec((tt, LANES), lambda bi, i: (i, 0))] * 3
        args += list(tables)
    out_shape = [jax.ShapeDtypeStruct((b, t, RET_Q_W), F32),
                 jax.ShapeDtypeStruct((b, t, RET_Q_W), F32),
                 jax.ShapeDtypeStruct((b, t, RET_V_W), BF16),
                 jax.ShapeDtypeStruct((b, t, RET_V_W), BF16),
                 jax.ShapeDtypeStruct((b, t, ATT_Q_W), BF16),
                 jax.ShapeDtypeStruct((b, ATT_K_W, t), BF16),
                 jax.ShapeDtypeStruct((b, t, ATT_K_W), BF16)]
    out_specs = [pl.BlockSpec((1, tt, RET_Q_W), tok3),
                 pl.BlockSpec((1, tt, RET_Q_W), tok3),
                 pl.BlockSpec((1, tt, RET_V_W), tok3),
                 pl.BlockSpec((1, tt, RET_V_W), tok3),
                 pl.BlockSpec((1, tt, ATT_Q_W), tok3),
                 pl.BlockSpec((1, ATT_K_W, tt), lambda bi, i: (bi, 0, i)),
                 pl.BlockSpec((1, tt, ATT_K_W), tok3)]
    return pl.pallas_call(
        functools.partial(_in_proj_kernel, rope=rope),
        grid=(b, t // tt),
        in_specs=in_specs,
        out_specs=out_specs,
        out_shape=out_shape,
        compiler_params=_cparams("parallel", "parallel"),
        name="in_proj_rope" if rope else "in_proj_ctx",
    )(*args)


def _retention_kernel(lgf_ref, lgb_ref, q_ref, k_ref, v_ref, g_ref, kc_ref, vc_ref, gng_ref, gnb_ref,
                      o_ref, kvf_ref, kvb_ref, sf_ref, sb_ref):
    pair = pl.program_id(1)
    t = q_ref.shape[1]
    n_chunks = t // RET_CHUNK
    ctx_len = kc_ref.shape[1]
    c = RET_CHUNK
    pos = lax.broadcasted_iota(jnp.int32, (c, 1), 0).astype(F32)
    row = lax.broadcasted_iota(jnp.int32, (c, c), 0)
    col = lax.broadcasted_iota(jnp.int32, (c, c), 1)
    diff = (row - col).astype(F32)
    cpos = lax.broadcasted_iota(jnp.int32, (ctx_len, 1), 0).astype(F32)
    tn = (((0,), (0,)), ((), ()))
    nt = (((1,), (1,)), ((), ()))

    heads = []
    for hh in range(2):
        head = pair * 2 + hh
        lgf = lgf_ref[0, head]
        lgb = lgb_ref[0, head]
        heads.append(dict(
            ql=slice(hh * RET_DK, (hh + 1) * RET_DK),
            vl=slice(hh * RET_DV, (hh + 1) * RET_DV),
            lgf=lgf, lgb=lgb,
            kdf=jnp.exp(lgf * (c - 1.0 - pos)),
            kdb=jnp.exp(lgb * pos),
            qdf=jnp.exp(lgf * (pos + 1.0)),
            qdb=jnp.exp(lgb * (c - pos)),
            decay=(jnp.where(diff >= 0, jnp.exp(lgf * jnp.maximum(diff, 0.0)), 0.0)
                   + jnp.where(diff <= 0, jnp.exp(lgb * jnp.maximum(-diff, 0.0)), 0.0))))

    def kv_body(n, carry):
        r = pl.ds(pl.multiple_of(n * c, c), c)
        for hh, hd in enumerate(heads):
            kc = k_ref[0, r, hd["ql"]]
            vc = v_ref[0, r, hd["vl"]]
            kvf_ref[hh, n] = lax.dot_general((kc * hd["kdf"]).astype(BF16), vc, tn, preferred_element_type=F32)
            kvb_ref[hh, n] = lax.dot_general((kc * hd["kdb"]).astype(BF16), vc, tn, preferred_element_type=F32)
        return carry

    lax.fori_loop(0, n_chunks, kv_body, 0, unroll=2)

    for hh, hd in enumerate(heads):
        lgf, lgb = hd["lgf"], hd["lgb"]
        kctx = kc_ref[0, :, hd["ql"]]
        vctx = vc_ref[0, :, hd["vl"]]
        s_f = lax.dot_general((kctx * jnp.exp(lgf * (ctx_len - 1.0 - cpos))).astype(BF16), vctx, tn,
                              preferred_element_type=F32)
        s_b = lax.dot_general((kctx * jnp.exp(lgb * cpos)).astype(BF16), vctx, tn,
                              preferred_element_type=F32)
        gfc = jnp.exp(lgf * c)
        gbc = jnp.exp(lgb * c)
        for n in range(n_chunks):
            sf_ref[hh, n] = s_f.astype(BF16)
            s_f = gfc * s_f + kvf_ref[hh, n]
        for n in range(n_chunks - 1, -1, -1):
            sb_ref[hh, n] = s_b.astype(BF16)
            s_b = gbc * s_b + kvb_ref[hh, n]

    def out_body(n, carry):
        r = pl.ds(pl.multiple_of(n * c, c), c)
        for hh, hd in enumerate(heads):
            ql, vl = hd["ql"], hd["vl"]
            qc = q_ref[0, r, ql]
            kc = k_ref[0, r, ql]
            vc = v_ref[0, r, vl]
            s = lax.dot_general(qc.astype(BF16), kc.astype(BF16), nt, preferred_element_type=F32)
            y = jnp.dot((s * hd["decay"]).astype(BF16), vc, preferred_element_type=F32)
            y += jnp.dot((qc * hd["qdf"]).astype(BF16), sf_ref[hh, n], preferred_element_type=F32)
            y += jnp.dot((qc * hd["qdb"]).astype(BF16), sb_ref[hh, n], preferred_element_type=F32)
            mu = jnp.mean(y, -1, keepdims=True)
            yc = y - mu
            var = jnp.mean(yc * yc, -1, keepdims=True)
            yn = yc * lax.rsqrt(var + LN_EPS) * gng_ref[:, vl] + gnb_ref[:, vl]
            o_ref[0, r, vl] = (yn * _silu(g_ref[0, r, vl].astype(F32))).astype(o_ref.dtype)
        return carry

    lax.fori_loop(0, n_chunks, out_body, 0, unroll=2)


def _retention(lgf, lgb, rq, rk, rv, rg, rkc, rvc, gng, gnb):
    b, t, _ = rq.shape
    ctx_len = rkc.shape[1]
    n_chunks = t // RET_CHUNK
    pair3 = lambda bi, p: (bi, 0, p)
    smem = pl.BlockSpec(memory_space=pltpu.SMEM)
    state = (2, n_chunks, RET_DK, RET_DV)
    return pl.pallas_call(
        _retention_kernel,
        grid=(b, RET_HEADS // 2),
        in_specs=[smem, smem,
                  pl.BlockSpec((1, t, 2 * RET_DK), pair3),
                  pl.BlockSpec((1, t, 2 * RET_DK), pair3),
                  pl.BlockSpec((1, t, 2 * RET_DV), pair3),
                  pl.BlockSpec((1, t, 2 * RET_DV), pair3),
                  pl.BlockSpec((1, ctx_len, 2 * RET_DK), pair3),
                  pl.BlockSpec((1, ctx_len, 2 * RET_DV), pair3),
                  pl.BlockSpec((1, 2 * RET_DV), lambda bi, p: (0, p)),
                  pl.BlockSpec((1, 2 * RET_DV), lambda bi, p: (0, p))],
        out_specs=pl.BlockSpec((1, t, 2 * RET_DV), pair3),
        out_shape=jax.ShapeDtypeStruct((b, t, RET_V_W), BF16),
        scratch_shapes=[pltpu.VMEM(state, F32), pltpu.VMEM(state, F32),
                        pltpu.VMEM(state, BF16), pltpu.VMEM(state, BF16)],
        compiler_params=_cparams("parallel", "parallel"),
        name="retention",
    )(lgf, lgb, rq, rk, rv, rg, rkc, rvc, gng, gnb)


def _attention_kernel(q_ref, kt_ref, v_ref, o_ref, s_buf, p_buf):
    tq = q_ref.shape[1]
    lane = lax.broadcasted_iota(jnp.int32, (tq, LANES), 1)

    def scores(h):
        kvh = h // ATT_GROUP
        s_buf[h % 2] = jnp.dot(q_ref[0, :, h * HEAD_DIM:(h + 1) * HEAD_DIM],
                               kt_ref[0, kvh * HEAD_DIM:(kvh + 1) * HEAD_DIM, :], preferred_element_type=F32)

    def numerators(h):
        s = s_buf[h % 2]
        p_buf[h % 2] = jnp.exp2(s - jnp.max(s, -1, keepdims=True)).astype(BF16)

    def values(h):
        ov = jnp.dot(p_buf[h % 2], v_ref[0], preferred_element_type=F32)
        o = ov[:, :LANES] / ov[:, LANES:LANES + 1]
        if h // ATT_GROUP != h % 2:
            o = pltpu.roll(o, HEAD_DIM, 1)
        return o

    scores(0)
    scores(1)
    numerators(0)
    even = None
    for h in range(ATT_HEADS):
        if h + 2 < ATT_HEADS:
            scores(h + 2)
        if h + 1 < ATT_HEADS:
            numerators(h + 1)
        o = values(h)
        if h % 2 == 0:
            even = o
        else:
            col = (h // 2) * LANES
            o_ref[0, :, col:col + LANES] = jnp.where(lane < HEAD_DIM, even, o).astype(o_ref.dtype)


def _attention(aq, akt, av_ones):
    b, t, _ = aq.shape
    tk = akt.shape[2]
    tq = ATT_Q_TILE
    return pl.pallas_call(
        _attention_kernel,
        grid=(b, t // tq),
        in_specs=[pl.BlockSpec((1, tq, ATT_Q_W), lambda bi, i: (bi, i, 0)),
                  pl.BlockSpec((1, ATT_K_W, tk), lambda bi, i: (bi, 0, 0)),
                  pl.BlockSpec((1, tk, av_ones.shape[2]), lambda bi, i: (bi, 0, 0))],
        out_specs=pl.BlockSpec((1, tq, ATT_Q_W), lambda bi, i: (bi, i, 0)),
        out_shape=jax.ShapeDtypeStruct((b, t, ATT_Q_W), BF16),
        scratch_shapes=[pltpu.VMEM((2, tq, tk), F32), pltpu.VMEM((2, tq, tk), BF16)],
        compiler_params=_cparams("parallel", "parallel"),
        name="attention",
    )(aq, akt, av_ones)


def _post_mixer_kernel(ret_ref, att_ref, x_ref, mod_ref, lng_ref, lnb_ref, wo_ref, l1g_ref, l1b_ref,
                       rwh_ref, rwl_ref, rb_ref, tri_ref, triu_ref,
                       x1_ref, h2_ref, gate_ref, lrow_ref, cnt_ref, lstart_ref, tbase_ref, base_acc):
    first = (pl.program_id(0) == 0) & (pl.program_id(1) == 0)

    @pl.when(first)
    def _():
        base_acc[...] = jnp.zeros_like(base_acc)

    half = ret_ref.shape[2]
    y = (jnp.dot(ret_ref[0], wo_ref[0:half, :], preferred_element_type=F32)
         + jnp.dot(att_ref[0], wo_ref[half:, :], preferred_element_type=F32))
    xn = _layer_norm(x_ref[0], lng_ref[...], lnb_ref[...])
    x1 = _layer_norm(DN_ALPHA * xn + mod_ref[0, 2:3, :] * y, l1g_ref[...], l1b_ref[...])
    x1_ref[...] = x1
    h2 = x1 * (1.0 + mod_ref[0, 4:5, :]) + mod_ref[0, 3:4, :]
    h_hi = h2.astype(BF16)
    h2_ref[...] = h_hi

    h_lo = (h2 - h_hi.astype(F32)).astype(BF16)
    logits = (jnp.dot(h_hi, rwh_ref[...], preferred_element_type=F32)
              + jnp.dot(h_lo, rwh_ref[...], preferred_element_type=F32)
              + jnp.dot(h_hi, rwl_ref[...], preferred_element_type=F32)) + rb_ref[...]

    tt = logits.shape[0]
    lane = lax.broadcasted_iota(jnp.int32, (tt, N_EXPERTS), 1)
    sels, vals = [], []
    onehot = jnp.zeros((tt, N_EXPERTS), F32)
    work = logits
    for _ in range(TOP_K):
        m = jnp.max(work, -1, keepdims=True)
        idx = jnp.min(jnp.where(work == m, lane, N_EXPERTS), -1, keepdims=True)
        sel = lane == idx
        sels.append(sel)
        vals.append(m)
        onehot = onehot + sel.astype(F32)
        work = jnp.where(sel, -jnp.inf, work)
    exps = [jnp.exp(v - vals[0]) for v in vals]
    denom = exps[0] + exps[1] + exps[2] + exps[3]

    count = jnp.sum(onehot, 0, keepdims=True)
    units = jnp.floor((count + (SEG_ALIGN - 1.0)) * (1.0 / SEG_ALIGN))
    lstart = SEG_ALIGN * jnp.dot(jnp.broadcast_to(units, (8, N_EXPERTS)).astype(BF16), triu_ref[...],
                                 preferred_element_type=F32)[0:1, :]
    rank = jnp.dot(tri_ref[...], onehot.astype(BF16), preferred_element_type=F32) + lstart

    lane4 = lax.broadcasted_iota(jnp.int32, (tt, TOP_K), 1)
    gate_out = jnp.zeros((tt, TOP_K), F32)
    lrow_out = jnp.zeros((tt, TOP_K), F32)
    for k in range(TOP_K):
        pk = jnp.sum(jnp.where(sels[k], rank, 0.0), -1, keepdims=True)
        gate_out = jnp.where(lane4 == k, exps[k] / denom, gate_out)
        lrow_out = jnp.where(lane4 == k, pk, lrow_out)
    gate_ref[...] = gate_out
    lrow_ref[...] = lrow_out.astype(jnp.int32)
    cnt_ref[0] = (units * SEG_ALIGN).astype(jnp.int32)
    lstart_ref[0] = lstart.astype(jnp.int32)
    tbase_ref[0] = base_acc[...].astype(jnp.int32)
    base_acc[...] += units * SEG_ALIGN


def _post_mixer(ret, att, x, mod, lng, lnb, wo_bf, l1g, l1b, rw_hi, rw_lo, rb, tri, triu):
    b, t, d = x.shape
    tt = TOKEN_TILE
    nt = t // tt
    n = b * t
    const2 = lambda bi, i: (0, 0)
    tok3 = lambda bi, i: (bi, i, 0)
    flat = lambda bi, i: (bi * nt + i, 0)
    tile3 = lambda bi, i: (bi * nt + i, 0, 0)
    vec = pl.BlockSpec((1, d), const2)
    table = jax.ShapeDtypeStruct((b * nt, 1, N_EXPERTS), jnp.int32)
    return pl.pallas_call(
        _post_mixer_kernel,
        grid=(b, nt),
        in_specs=[pl.BlockSpec((1, tt, ret.shape[2]), tok3),
                  pl.BlockSpec((1, tt, att.shape[2]), tok3),
                  pl.BlockSpec((1, tt, d), tok3),
                  pl.BlockSpec((1, 6, d), lambda bi, i: (bi, 0, 0)),
                  vec, vec,
                  pl.BlockSpec(wo_bf.shape, const2),
                  vec, vec,
                  pl.BlockSpec(rw_hi.shape, const2),
                  pl.BlockSpec(rw_lo.shape, const2),
                  pl.BlockSpec((1, N_EXPERTS), const2),
                  pl.BlockSpec((tt, tt), const2),
                  pl.BlockSpec((N_EXPERTS, N_EXPERTS), const2)],
        out_specs=[pl.BlockSpec((tt, d), flat),
                   pl.BlockSpec((tt, d), flat),
                   pl.BlockSpec((tt, TOP_K), flat),
                   pl.BlockSpec((tt, TOP_K), flat),
                   pl.BlockSpec((1, 1, N_EXPERTS), tile3),
                   pl.BlockSpec((1, 1, N_EXPERTS), tile3),
                   pl.BlockSpec((1, 1, N_EXPERTS), tile3)],
        out_shape=[jax.ShapeDtypeStruct((n, d), F32),
                   jax.ShapeDtypeStruct((n, d), BF16),
                   jax.ShapeDtypeStruct((n, TOP_K), F32),
                   jax.ShapeDtypeStruct((n, TOP_K), jnp.int32),
                   table, table, table],
        scratch_shapes=[pltpu.VMEM((1, N_EXPERTS), F32)],
        compiler_params=_cparams("arbitrary", "arbitrary"),
        name="post_mixer",
    )(ret, att, x, mod, lng, lnb, wo_bf, l1g, l1b, rw_hi, rw_lo, rb, tri, triu)


def _segment_starts(cnt_ref, lstart_ref, dst_ref, tile, make_copy):
    def body(e, carry):
        t = tile * N_EXPERTS + e
        n = cnt_ref[t]

        @pl.when(n > 0)
        def _():
            make_copy(lstart_ref[t], dst_ref[t], n).start()
        return carry
    lax.fori_loop(0, N_EXPERTS, body, 0, unroll=4)


def _segment_waits(total_ref, tile, make_copy):
    make_copy(0, 0, total_ref[tile]).wait()


def _zero_fill_copies(tail_len_ref, tail_dst_ref, nact_ref, zbuf, x_hbm, zsem, act):
    block = zbuf.shape[0]

    def tail_body(e, carry):
        n = tail_len_ref[e]

        @pl.when(n > 0)
        def _():
            copy = pltpu.make_async_copy(zbuf.at[pl.ds(0, n)], x_hbm.at[pl.ds(tail_dst_ref[e], n)], zsem)
            getattr(copy, act)()
        return carry
    lax.fori_loop(0, N_EXPERTS, tail_body, 0)

    def block_body(j, carry):
        copy = pltpu.make_async_copy(zbuf, x_hbm.at[pl.ds(j * block, block)], zsem)
        getattr(copy, act)()
        return carry
    lax.fori_loop(nact_ref[0], x_hbm.shape[0] // block, block_body, 0)


def _dispatch_kernel(cnt_ref, lstart_ref, dst_ref, total_ref, tail_len_ref, tail_dst_ref, nact_ref,
                     h_ref, lrow_ref, x_hbm, sbuf, sem, zbuf, zsem):
    i = pl.program_id(0)
    last = pl.num_programs(0) - 1
    slot = i % 2
    rows = sbuf.shape[1] * SEG_ALIGN
    tt = h_ref.shape[0]
    d = h_ref.shape[1]
    chunk_groups = PERM_CHUNK // SEG_ALIGN

    def make_copy(tile_slot):
        def mk(ls, ds, size):
            return pltpu.make_async_copy(sbuf.at[tile_slot, pl.ds(ls, size)], x_hbm.at[pl.ds(ds, size)],
                                         sem.at[tile_slot])
        return mk

    @pl.when(i == 0)
    def _():
        zbuf[...] = jnp.zeros_like(zbuf)
        _zero_fill_copies(tail_len_ref, tail_dst_ref, nact_ref, zbuf, x_hbm, zsem, "start")

    def perm(j):
        r_iota = lax.broadcasted_iota(jnp.int32, (PERM_CHUNK, tt), 0) + j * PERM_CHUNK
        hit = r_iota == lrow_ref[0, 0:1, :]
        for k in range(1, TOP_K):
            hit = hit | (r_iota == lrow_ref[0, k:k + 1, :])
        return jnp.where(hit, 1.0, 0.0).astype(BF16)

    p_next = perm(0)
    for j in range(rows // PERM_CHUNK):
        p_cur = p_next
        if (j + 1) * PERM_CHUNK < rows:
            p_next = perm(j + 1)
        sbuf[slot, j * chunk_groups:(j + 1) * chunk_groups] = jnp.dot(
            p_cur, h_ref[...], preferred_element_type=F32).astype(BF16).reshape(chunk_groups, SEG_ALIGN, d)
    _segment_starts(cnt_ref, lstart_ref, dst_ref, i, make_copy(slot))

    @pl.when(i > 0)
    def _():
        _segment_waits(total_ref, i - 1, make_copy(1 - slot))

    @pl.when(i == last)
    def _():
        _segment_waits(total_ref, i, make_copy(slot))
        _zero_fill_copies(tail_len_ref, tail_dst_ref, nact_ref, zbuf, x_hbm, zsem, "wait")


def _dispatch(cnt, lstart, dst, total, tail_len, tail_dst, n_active, h2, lrow_t, n_rows):
    n, d = h2.shape
    tt = TOKEN_TILE
    grid_spec = pltpu.PrefetchScalarGridSpec(
        num_scalar_prefetch=7,
        grid=(n // tt,),
        in_specs=[pl.BlockSpec((tt, d), lambda i, *_: (i, 0)),
                  pl.BlockSpec((1, TOP_K, tt), lambda i, *_: (i, 0, 0))],
        out_specs=pl.BlockSpec(memory_space=pl.ANY),
        scratch_shapes=[pltpu.VMEM((2, TILE_SORTED_ROWS // SEG_ALIGN, SEG_ALIGN, d), BF16),
                        pltpu.SemaphoreType.DMA((2,)),
                        pltpu.VMEM((EXPERT_ROWS // SEG_ALIGN, SEG_ALIGN, d), BF16),
                        pltpu.SemaphoreType.DMA(())])
    return pl.pallas_call(
        _dispatch_kernel,
        grid_spec=grid_spec,
        out_shape=jax.ShapeDtypeStruct((n_rows // SEG_ALIGN, SEG_ALIGN, d), BF16),
        compiler_params=_cparams("arbitrary"),
        name="dispatch",
    )(cnt, lstart, dst, total, tail_len, tail_dst, n_active, h2, lrow_t)


def _experts_kernel(be_ref, nact_ref, x_ref, wup_ref, bup_ref, wdn_ref, bdn_ref, y_ref,
                    wup_bf, wdn_bf):
    j = pl.program_id(0)
    dff = wdn_ref.shape[1]

    @pl.when(j < nact_ref[0])
    def _():
        prev = be_ref[jnp.maximum(j - 1, 0)]

        @pl.when((j == 0) | (be_ref[j] != prev))
        def _():
            wup_bf[...] = wup_ref[0].astype(BF16)
            wdn_bf[...] = wdn_ref[0].astype(BF16)

        x = x_ref[...].reshape(-1, x_ref.shape[2])
        u = jnp.dot(x, wup_bf[...], preferred_element_type=F32) + bup_ref[0]
        glu = jnp.minimum(u[:, :dff], SWIGLU_LIMIT)
        lin = jnp.clip(u[:, dff:], -SWIGLU_LIMIT, SWIGLU_LIMIT)
        act = glu * (1.0 / (1.0 + jnp.exp(-SWIGLU_ALPHA * glu))) * (lin + 1.0)
        y = jnp.dot(act.astype(BF16), wdn_bf[...], preferred_element_type=F32) + bdn_ref[0]
        y_ref[...] = y.astype(y_ref.dtype).reshape(y_ref.shape)

    @pl.when(j >= nact_ref[0])
    def _():
        y_ref[...] = jnp.zeros_like(y_ref)


def _experts(block_expert, n_active, x_sorted, w_up, b_up, w_down, b_down):
    n_blocks = block_expert.shape[0]
    tm = EXPERT_ROWS
    d = x_sorted.shape[2]
    e, _, up_w = w_up.shape
    dff = w_down.shape[1]
    by_expert = lambda j, be, na: (be[j], 0, 0)
    x_map = lambda j, be, na: (jnp.minimum(j, na[0] - 1), 0, 0)
    rows3 = (tm // SEG_ALIGN, SEG_ALIGN, d)
    grid_spec = pltpu.PrefetchScalarGridSpec(
        num_scalar_prefetch=2,
        grid=(n_blocks,),
        in_specs=[pl.BlockSpec(rows3, x_map),
                  pl.BlockSpec((1, d, up_w), by_expert),
                  pl.BlockSpec((1, 1, up_w), by_expert),
                  pl.BlockSpec((1, dff, d), by_expert),
                  pl.BlockSpec((1, 1, d), by_expert)],
        out_specs=pl.BlockSpec(rows3, lambda j, be, na: (j, 0, 0)),
        scratch_shapes=[pltpu.VMEM((d, up_w), BF16),
                        pltpu.VMEM((dff, d), BF16)])
    return pl.pallas_call(
        _experts_kernel,
        grid_spec=grid_spec,
        out_shape=jax.ShapeDtypeStruct(x_sorted.shape, BF16),
        compiler_params=_cparams("arbitrary"),
        name="experts",
    )(block_expert, n_active, x_sorted, w_up, b_up.reshape(e, 1, up_w), w_down,
      b_down.reshape(e, 1, d))


def _combine_kernel(cnt_ref, lstart_ref, dst_ref, total_ref, lrow_ref, gate_ref, x1_ref, mod_ref, g_ref, b_ref,
                    y_hbm, o_ref, ybuf, sem):
    i = pl.program_id(0)
    last = pl.num_programs(0) - 1
    slot = i % 2
    rows = ybuf.shape[1] * SEG_ALIGN
    tt = x1_ref.shape[0]
    d = x1_ref.shape[1]
    chunk_groups = PERM_CHUNK // SEG_ALIGN

    def make_copy(tile_slot):
        def mk(ls, ds, size):
            return pltpu.make_async_copy(y_hbm.at[pl.ds(ds, size)], ybuf.at[tile_slot, pl.ds(ls, size)],
                                         sem.at[tile_slot])
        return mk

    @pl.when(i == 0)
    def _():
        ybuf[...] = jnp.zeros_like(ybuf)
        _segment_starts(cnt_ref, lstart_ref, dst_ref, i, make_copy(slot))

    @pl.when(i < last)
    def _():
        _segment_starts(cnt_ref, lstart_ref, dst_ref, i + 1, make_copy(1 - slot))

    def weights(j):
        c_iota = lax.broadcasted_iota(jnp.int32, (tt, PERM_CHUNK), 1) + j * PERM_CHUNK
        w = jnp.zeros((tt, PERM_CHUNK), F32)
        for k in range(TOP_K):
            w = w + jnp.where(c_iota == lrow_ref[:, k:k + 1], gate_ref[:, k:k + 1], 0.0)
        return w.astype(BF16)

    w_next = weights(0)
    _segment_waits(total_ref, i, make_copy(slot))
    f = None
    for j in range(rows // PERM_CHUNK):
        w_cur = w_next
        if (j + 1) * PERM_CHUNK < rows:
            w_next = weights(j + 1)
        y_chunk = ybuf[slot, j * chunk_groups:(j + 1) * chunk_groups].reshape(PERM_CHUNK, d)
        part = jnp.dot(w_cur, y_chunk, preferred_element_type=F32)
        f = part if f is None else f + part
    o_ref[...] = _layer_norm(DN_ALPHA * x1_ref[...] + mod_ref[0, 5:6, :] * f, g_ref[...], b_ref[...])


def _combine(cnt, lstart, dst, total, lrow, gates, x1, mod, g, b, y_sorted, tokens_per_sample):
    n, d = x1.shape
    tt = TOKEN_TILE
    tiles_per_sample = tokens_per_sample // tt
    const2 = lambda i, *_: (0, 0)
    tok = lambda i, *_: (i, 0)
    grid_spec = pltpu.PrefetchScalarGridSpec(
        num_scalar_prefetch=4,
        grid=(n // tt,),
        in_specs=[pl.BlockSpec((tt, TOP_K), tok),
                  pl.BlockSpec((tt, TOP_K), tok),
                  pl.BlockSpec((tt, d), tok),
                  pl.BlockSpec((1, 6, d), lambda i, *_: (i // tiles_per_sample, 0, 0)),
                  pl.BlockSpec((1, d), const2),
                  pl.BlockSpec((1, d), const2),
                  pl.BlockSpec(memory_space=pl.ANY)],
        out_specs=pl.BlockSpec((tt, d), tok),
        scratch_shapes=[pltpu.VMEM((2, TILE_SORTED_ROWS // SEG_ALIGN, SEG_ALIGN, d), BF16),
                        pltpu.SemaphoreType.DMA((2,))])
    return pl.pallas_call(
        _combine_kernel,
        grid_spec=grid_spec,
        out_shape=jax.ShapeDtypeStruct((n, d), F32),
        compiler_params=_cparams("arbitrary"),
        name="combine",
    )(cnt, lstart, dst, total, lrow, gates, x1, mod, g, b, y_sorted)


def _rope_tables(t):
    rows = jnp.repeat(jnp.arange(t // GRID_W, dtype=jnp.int32), GRID_W).astype(F32)
    cols = jnp.tile(jnp.arange(GRID_W, dtype=jnp.int32), t // GRID_W).astype(F32)
    n_freq = HEAD_DIM // 4
    inv = ROPE_BASE ** (-jnp.arange(n_freq, dtype=F32) / n_freq)
    ang_r = rows[:, None] * inv
    ang_c = cols[:, None] * inv
    ang = jnp.concatenate([ang_r, ang_r, ang_c, ang_c], -1)
    ang = jnp.concatenate([ang, ang], -1)
    cos, sin = jnp.cos(ang), jnp.sin(ang)
    first_half = (jnp.arange(LANES) % 32) < 16
    return cos, jnp.where(first_half, -sin, 0.0), jnp.where(first_half, 0.0, sin)


def kernel(x, c, ctx, c_ctx, ln_in_g, ln_in_b, w_ada, b_ada, w_in, ret_log_decay_f, ret_log_decay_b,
           ret_gn_g, ret_gn_b, q_norm_g, k_norm_g, w_o, ln1_g, ln1_b, router_w, router_b,
           w_up, b_up, w_down, b_down, ln2_g, ln2_b):
    b, t, d = x.shape
    n = b * t
    row = lambda v: v.reshape(1, -1)

    cc = jnp.concatenate([c, c_ctx[None]], 0)
    cc = jnp.pad(cc, ((0, (-cc.shape[0]) % 8), (0, 0)))
    mod_all = _ada_mod(cc, w_ada[0], row(b_ada[0]))
    mod = mod_all[:b].reshape(b, 6, d)
    mod_ctx = mod_all[b:b + 1].reshape(1, 6, d)

    w_in_bf = w_in[0].astype(BF16)
    lane = jnp.arange(LANES)
    seg = (lane[:, None] // HEAD_DIM == lane[None, :] // HEAD_DIM).astype(BF16)
    qg = jnp.tile(q_norm_g[0], ATT_HEADS).reshape(1, -1)
    kg = jnp.tile(k_norm_g[0], ATT_KV_HEADS).reshape(1, -1)
    lng, lnb = row(ln_in_g), row(ln_in_b)

    rq, rk, rv, rg, aq, akt, av = _in_proj(x, mod, lng, lnb, w_in_bf, seg, qg, kg, _rope_tables(t))
    _, rkc, rvc, _, _, aktc, avc = _in_proj(ctx, mod_ctx, lng, lnb, w_in_bf, seg, qg, kg, None)

    ret = _retention(row(ret_log_decay_f[0]), row(ret_log_decay_b[0]), rq, rk, rv, rg, rkc, rvc,
                     row(ret_gn_g[0]), row(ret_gn_b[0]))
    tk = ctx.shape[1] + t
    av_ones = jnp.concatenate([jnp.concatenate([avc, av], 1), jnp.ones((b, tk, LANES), BF16)], 2)
    att = _attention(aq, jnp.concatenate([aktc, akt], 2), av_ones)

    rw = router_w[0]
    rw_hi = rw.astype(BF16)
    rw_lo = (rw - rw_hi.astype(F32)).astype(BF16)
    tt = TOKEN_TILE
    n_tiles = n // tt
    tri = jnp.tril(jnp.ones((tt, tt), BF16), -1)
    triu = jnp.triu(jnp.ones((N_EXPERTS, N_EXPERTS), BF16), 1)
    x1, h2, gates, lrow, cnt, lstart, tbase = _post_mixer(
        ret, att, x, mod, lng, lnb, w_o[0].astype(BF16), row(ln1_g[0]), row(ln1_b[0]),
        rw_hi, rw_lo, row(router_b[0]), tri, triu)

    tm = EXPERT_ROWS
    cnt, lstart, tbase = cnt[:, 0], lstart[:, 0], tbase[:, 0]
    totals = tbase[-1] + cnt[-1]
    exp_rows = (totals + tm - 1) // tm * tm
    exp_end = jnp.cumsum(exp_rows)
    exp_start = exp_end - exp_rows
    dst = (exp_start[None, :] + tbase).reshape(-1)
    max_rows = n * TOP_K + n_tiles * N_EXPERTS * (SEG_ALIGN - 1) + N_EXPERTS * (tm - 1)
    n_blocks = -(-max_rows // tm)
    block_start = jnp.arange(n_blocks, dtype=jnp.int32) * tm
    block_expert = jnp.minimum(jnp.sum(exp_end[None, :] <= block_start[:, None], axis=1),
                               N_EXPERTS - 1).astype(jnp.int32)
    n_active = (exp_end[-1:] // tm).astype(jnp.int32)
    lrow_t = lrow.reshape(n_tiles, tt, TOP_K).transpose(0, 2, 1)
    groups = lambda v: (v // SEG_ALIGN).astype(jnp.int32).reshape(-1)
    cnt_g, lstart_g, dst_g, total_g = groups(cnt), groups(lstart), groups(dst), groups(jnp.sum(cnt, axis=1))

    x_sorted = _dispatch(cnt_g, lstart_g, dst_g, total_g, groups(exp_rows - totals), groups(exp_start + totals),
                         n_active, h2, lrow_t, n_blocks * tm)
    y_sorted = _experts(block_expert, n_active, x_sorted, w_up[0], b_up[0], w_down[0], b_down[0])
    out = _combine(cnt_g, lstart_g, dst_g, total_g, lrow, gates, x1, mod, row(ln2_g[0]), row(ln2_b[0]),
                   y_sorted, t)
    return out.reshape(b, t, d)
```

```python
import functools

import jax
import jax.numpy as jnp
from jax import lax
from jax.experimental import pallas as pl
from jax.experimental.pallas import tpu as pltpu

GRID_W = 64
HEAD_DIM = 64
ROPE_BASE = 10000.0
LN_EPS = 1e-6
RET_HEADS = 4
RET_DK = 64
RET_DV = 128
RET_CHUNK = 128
ATT_HEADS = 8
ATT_KV_HEADS = 2
ATT_GROUP = ATT_HEADS // ATT_KV_HEADS
N_EXPERTS = 32
TOP_K = 4
SWIGLU_LIMIT = 7.0
SWIGLU_ALPHA = 1.702
DEPTH = 1
DN_ALPHA = (2.0 * DEPTH) ** 0.25
LOG2_E = 1.4426950408889634

RET_Q_W = RET_HEADS * RET_DK
RET_V_W = RET_HEADS * RET_DV
ATT_Q_W = ATT_HEADS * HEAD_DIM
ATT_K_W = ATT_KV_HEADS * HEAD_DIM

LANES = 128
VMEM_LIMIT_BYTES = 56 * 1024 * 1024

TOKEN_TILE = 512
ATT_Q_TILE = 256
EXPERT_ROWS = 512
SEG_ALIGN = 16
TILE_SORTED_ROWS = 2560
PERM_CHUNK = 512

F32 = jnp.float32
BF16 = jnp.bfloat16


def _cparams(*sem):
    return pltpu.CompilerParams(dimension_semantics=sem, vmem_limit_bytes=VMEM_LIMIT_BYTES)


def _layer_norm(x, g, b):
    mu = jnp.mean(x, -1, keepdims=True)
    xc = x - mu
    var = jnp.mean(xc * xc, -1, keepdims=True)
    return xc * lax.rsqrt(var + LN_EPS) * g + b


def _silu(x):
    return x * (1.0 / (1.0 + jnp.exp(-x)))


def _ada_kernel(c_ref, w_ref, b_ref, o_ref):
    a = _silu(c_ref[...])
    o_ref[...] = jnp.dot(a, w_ref[...], preferred_element_type=F32,
                         precision=lax.Precision.HIGHEST) + b_ref[...]


def _ada_mod(cc, w, b):
    m, d = cc.shape
    n = w.shape[1]
    tn = 1536
    return pl.pallas_call(
        _ada_kernel,
        grid=(n // tn,),
        in_specs=[pl.BlockSpec((m, d), lambda j: (0, 0)),
                  pl.BlockSpec((d, tn), lambda j: (0, j)),
                  pl.BlockSpec((1, tn), lambda j: (0, j))],
        out_specs=pl.BlockSpec((m, tn), lambda j: (0, j)),
        out_shape=jax.ShapeDtypeStruct((m, n), F32),
        compiler_params=_cparams("arbitrary"),
        name="ada_mod",
    )(cc, w, b)


def _rope(x, cos, sinn, sinp):
    outs = []
    for j in range(x.shape[1] // LANES):
        xg = x[:, j * LANES:(j + 1) * LANES]
        outs.append(xg * cos + pltpu.roll(xg, LANES - 16, 1) * sinn + pltpu.roll(xg, 16, 1) * sinp)
    return outs[0] if len(outs) == 1 else jnp.concatenate(outs, axis=1)


def _head_rms(x, seg, gain):
    outs = []
    for j in range(x.shape[1] // LANES):
        xg = x[:, j * LANES:(j + 1) * LANES]
        sq = xg * xg
        hi = sq.astype(BF16)
        lo = (sq - hi.astype(F32)).astype(BF16)
        ms = (jnp.dot(hi, seg, preferred_element_type=F32)
              + jnp.dot(lo, seg, preferred_element_type=F32)) * (1.0 / HEAD_DIM)
        outs.append(xg * lax.rsqrt(ms + LN_EPS) * gain[:, j * LANES:(j + 1) * LANES])
    return outs[0] if len(outs) == 1 else jnp.concatenate(outs, axis=1)


def _in_proj_kernel(*refs, rope):
    if rope:
        (x_ref, mod_ref, lng_ref, lnb_ref, w_ref, seg_ref, qg_ref, kg_ref, cos_ref, sinn_ref, sinp_ref,
         rq_ref, rk_ref, rv_ref, rg_ref, aq_ref, akt_ref, av_ref) = refs
    else:
        (x_ref, mod_ref, lng_ref, lnb_ref, w_ref, seg_ref, qg_ref, kg_ref,
         rq_ref, rk_ref, rv_ref, rg_ref, aq_ref, akt_ref, av_ref) = refs
    x = x_ref[0]
    xn = _layer_norm(x, lng_ref[...], lnb_ref[...])
    h = (xn * (1.0 + mod_ref[0, 1:2, :]) + mod_ref[0, 0:1, :]).astype(BF16)
    seg = seg_ref[...]

    def rotate(v):
        return _rope(v, cos_ref[...], sinn_ref[...], sinp_ref[...]) if rope else v

    o_rv = 2 * RET_Q_W
    o_aq = o_rv + 2 * RET_V_W
    o_ak = o_aq + ATT_Q_W
    o_end = o_ak + 2 * ATT_K_W

    def project(lo, hi):
        return jnp.dot(h, w_ref[:, lo:hi], preferred_element_type=F32)

    def finish_aq(u):
        aq = rotate(_head_rms(u, seg, qg_ref[...]))
        aq_ref[0] = (aq * (HEAD_DIM ** -0.5 * LOG2_E)).astype(BF16)

    def finish_akv(u):
        ak = rotate(_head_rms(u[:, :ATT_K_W], seg, kg_ref[...]))
        akt_ref[0] = ak.T.astype(BF16)
        av_ref[0] = u[:, ATT_K_W:].astype(BF16)

    def finish_rqk(u):
        rq_ref[0] = rotate(u[:, :RET_Q_W])
        rk_ref[0] = rotate(u[:, RET_Q_W:]) * (RET_DK ** -0.5)

    def finish_rvg(u):
        rv_ref[0] = u[:, :RET_V_W].astype(BF16)
        rg_ref[0] = u[:, RET_V_W:].astype(BF16)

    stages = [((o_aq, o_ak), finish_aq), ((o_ak, o_end), finish_akv),
              ((0, o_rv), finish_rqk), ((o_rv, o_aq), finish_rvg)]
    u_next = project(*stages[0][0])
    for idx, (_, finish) in enumerate(stages):
        u_cur = u_next
        if idx + 1 < len(stages):
            u_next = project(*stages[idx + 1][0])
        finish(u_cur)


def _in_proj(x, mod, lng, lnb, w_bf, seg, qg, kg, tables):
    b, t, d = x.shape
    tt = min(TOKEN_TILE, t)
    rope = tables is not None
    mod_b = mod.shape[0]
    mod_map = (lambda bi, i: (bi, 0, 0)) if mod_b > 1 else (lambda bi, i: (0, 0, 0))
    const2 = lambda bi, i: (0, 0)
    tok3 = lambda bi, i: (bi, i, 0)
    in_specs = [pl.BlockSpec((1, tt, d), tok3),
                pl.BlockSpec((1, 6, d), mod_map),
                pl.BlockSpec((1, d), const2),
                pl.BlockSpec((1, d), const2),
                pl.BlockSpec(w_bf.shape, const2),
                pl.BlockSpec(seg.shape, const2),
                pl.BlockSpec(qg.shape, const2),
                pl.BlockSpec(kg.shape, const2)]
    args = [x, mod, lng, lnb, w_bf, seg, qg, kg]
    if rope:
        in_specs += [pl.BlockSpec((tt, LANES), lambda bi, i: (i, 0))] * 3
        args += list(tables)
    out_shape = [jax.ShapeDtypeStruct((b, t, RET_Q_W), F32),
                 jax.ShapeDtypeStruct((b, t, RET_Q_W), F32),
                 jax.ShapeDtypeStruct((b, t, RET_V_W), BF16),
                 jax.ShapeDtypeStruct((b, t, RET_V_W), BF16),
                 jax.ShapeDtypeStruct((b, t, ATT_Q_W), BF16),
                 jax.ShapeDtypeStruct((b, ATT_K_W, t), BF16),
                 jax.ShapeDtypeStruct((b, t, ATT_K_W), BF16)]
    out_specs = [pl.BlockSpec((1, tt, RET_Q_W), tok3),
                 pl.BlockSpec((1, tt, RET_Q_W), tok3),
                 pl.BlockSpec((1, tt, RET_V_W), tok3),
                 pl.BlockSpec((1, tt, RET_V_W), tok3),
                 pl.BlockSpec((1, tt, ATT_Q_W), tok3),
                 pl.BlockSpec((1, ATT_K_W, tt), lambda bi, i: (bi, 0, i)),
                 pl.BlockSpec((1, tt, ATT_K_W), tok3)]
    return pl.pallas_call(
        functools.partial(_in_proj_kernel, rope=rope),
        grid=(b, t // tt),
        in_specs=in_specs,
        out_specs=out_specs,
        out_shape=out_shape,
        compiler_params=_cparams("parallel", "parallel"),
        name="in_proj_rope" if rope else "in_proj_ctx",
    )(*args)


def _retention_kernel(lgf_ref, lgb_ref, q_ref, k_ref, v_ref, g_ref, kc_ref, vc_ref, gng_ref, gnb_ref,
                      o_ref, kvf_ref, kvb_ref, sf_ref, sb_ref):
    pair = pl.program_id(1)
    t = q_ref.shape[1]
    n_chunks = t // RET_CHUNK
    ctx_len = kc_ref.shape[1]
    c = RET_CHUNK
    pos = lax.broadcasted_iota(jnp.int32, (c, 1), 0).astype(F32)
    row = lax.broadcasted_iota(jnp.int32, (c, c), 0)
    col = lax.broadcasted_iota(jnp.int32, (c, c), 1)
    diff = (row - col).astype(F32)
    cpos = lax.broadcasted_iota(jnp.int32, (ctx_len, 1), 0).astype(F32)
    tn = (((0,), (0,)), ((), ()))
    nt = (((1,), (1,)), ((), ()))

    heads = []
    for hh in range(2):
        head = pair * 2 + hh
        lgf = lgf_ref[0, head]
        lgb = lgb_ref[0, head]
        heads.append(dict(
            ql=slice(hh * RET_DK, (hh + 1) * RET_DK),
            vl=slice(hh * RET_DV, (hh + 1) * RET_DV),
            lgf=lgf, lgb=lgb,
            kdf=jnp.exp(lgf * (c - 1.0 - pos)),
            kdb=jnp.exp(lgb * pos),
            qdf=jnp.exp(lgf * (pos + 1.0)),
            qdb=jnp.exp(lgb * (c - pos)),
            decay=(jnp.where(diff >= 0, jnp.exp(lgf * jnp.maximum(diff, 0.0)), 0.0)
                   + jnp.where(diff <= 0, jnp.exp(lgb * jnp.maximum(-diff, 0.0)), 0.0))))

    def kv_body(n, carry):
        r = pl.ds(pl.multiple_of(n * c, c), c)
        for hh, hd in enumerate(heads):
            kc = k_ref[0, r, hd["ql"]]
            vc = v_ref[0, r, hd["vl"]]
            kvf_ref[hh, n] = lax.dot_general((kc * hd["kdf"]).astype(BF16), vc, tn, preferred_element_type=F32)
            kvb_ref[hh, n] = lax.dot_general((kc * hd["kdb"]).astype(BF16), vc, tn, preferred_element_type=F32)
        return carry

    lax.fori_loop(0, n_chunks, kv_body, 0, unroll=2)

    for hh, hd in enumerate(heads):
        lgf, lgb = hd["lgf"], hd["lgb"]
        kctx = kc_ref[0, :, hd["ql"]]
        vctx = vc_ref[0, :, hd["vl"]]
        s_f = lax.dot_general((kctx * jnp.exp(lgf * (ctx_len - 1.0 - cpos))).astype(BF16), vctx, tn,
                              preferred_element_type=F32)
        s_b = lax.dot_general((kctx * jnp.exp(lgb * cpos)).astype(BF16), vctx, tn,
                              preferred_element_type=F32)
        gfc = jnp.exp(lgf * c)
        gbc = jnp.exp(lgb * c)
        for n in range(n_chunks):
            sf_ref[hh, n] = s_f.astype(BF16)
            s_f = gfc * s_f + kvf_ref[hh, n]
        for n in range(n_chunks - 1, -1, -1):
            sb_ref[hh, n] = s_b.astype(BF16)
            s_b = gbc * s_b + kvb_ref[hh, n]

    def out_body(n, carry):
        r = pl.ds(pl.multiple_of(n * c, c), c)
        for hh, hd in enumerate(heads):
            ql, vl = hd["ql"], hd["vl"]
            qc = q_ref[0, r, ql]
            kc = k_ref[0, r, ql]
            vc = v_ref[0, r, vl]
            s = lax.dot_general(qc.astype(BF16), kc.astype(BF16), nt, preferred_element_type=F32)
            y = jnp.dot((s * hd["decay"]).astype(BF16), vc, preferred_element_type=F32)
            y += jnp.dot((qc * hd["qdf"]).astype(BF16), sf_ref[hh, n], preferred_element_type=F32)
            y += jnp.dot((qc * hd["qdb"]).astype(BF16), sb_ref[hh, n], preferred_element_type=F32)
            mu = jnp.mean(y, -1, keepdims=True)
            yc = y - mu
            var = jnp.mean(yc * yc, -1, keepdims=True)
            yn = yc * lax.rsqrt(var + LN_EPS) * gng_ref[:, vl] + gnb_ref[:, vl]
            o_ref[0, r, vl] = (yn * _silu(g_ref[0, r, vl].astype(F32))).astype(o_ref.dtype)
        return carry

    lax.fori_loop(0, n_chunks, out_body, 0, unroll=2)


def _retention(lgf, lgb, rq, rk, rv, rg, rkc, rvc, gng, gnb):
    b, t, _ = rq.shape
    ctx_len = rkc.shape[1]
    n_chunks = t // RET_CHUNK
    pair3 = lambda bi, p: (bi, 0, p)
    smem = pl.BlockSpec(memory_space=pltpu.SMEM)
    state = (2, n_chunks, RET_DK, RET_DV)
    return pl.pallas_call(
        _retention_kernel,
        grid=(b, RET_HEADS // 2),
        in_specs=[smem, smem,
                  pl.BlockSpec((1, t, 2 * RET_DK), pair3),
                  pl.BlockSpec((1, t, 2 * RET_DK), pair3),
                  pl.BlockSpec((1, t, 2 * RET_DV), pair3),
                  pl.BlockSpec((1, t, 2 * RET_DV), pair3),
                  pl.BlockSpec((1, ctx_len, 2 * RET_DK), pair3),
                  pl.BlockSpec((1, ctx_len, 2 * RET_DV), pair3),
                  pl.BlockSpec((1, 2 * RET_DV), lambda bi, p: (0, p)),
                  pl.BlockSpec((1, 2 * RET_DV), lambda bi, p: (0, p))],
        out_specs=pl.BlockSpec((1, t, 2 * RET_DV), pair3),
        out_shape=jax.ShapeDtypeStruct((b, t, RET_V_W), BF16),
        scratch_shapes=[pltpu.VMEM(state, F32), pltpu.VMEM(state, F32),
                        pltpu.VMEM(state, BF16), pltpu.VMEM(state, BF16)],
        compiler_params=_cparams("parallel", "parallel"),
        name="retention",
    )(lgf, lgb, rq, rk, rv, rg, rkc, rvc, gng, gnb)


def _attention_kernel(q_ref, kt_ref, v_ref, o_ref, s_buf, p_buf):
    tq = q_ref.shape[1]
    lane = lax.broadcasted_iota(jnp.int32, (tq, LANES), 1)

    def scores(h):
        kvh = h // ATT_GROUP
        s_buf[h % 2] = jnp.dot(q_ref[0, :, h * HEAD_DIM:(h + 1) * HEAD_DIM],
                               kt_ref[0, kvh * HEAD_DIM:(kvh + 1) * HEAD_DIM, :], preferred_element_type=F32)

    def numerators(h):
        s = s_buf[h % 2]
        p_buf[h % 2] = jnp.exp2(s - jnp.max(s, -1, keepdims=True)).astype(BF16)

    def values(h):
        ov = jnp.dot(p_buf[h % 2], v_ref[0], preferred_element_type=F32)
        o = ov[:, :LANES] / ov[:, LANES:LANES + 1]
        if h // ATT_GROUP != h % 2:
            o = pltpu.roll(o, HEAD_DIM, 1)
        return o

    scores(0)
    scores(1)
    numerators(0)
    even = None
    for h in range(ATT_HEADS):
        if h + 2 < ATT_HEADS:
            scores(h + 2)
        if h + 1 < ATT_HEADS:
            numerators(h + 1)
        o = values(h)
        if h % 2 == 0:
            even = o
        else:
            col = (h // 2) * LANES
            o_ref[0, :, col:col + LANES] = jnp.where(lane < HEAD_DIM, even, o).astype(o_ref.dtype)


def _attention(aq, akt, av_ones):
    b, t, _ = aq.shape
    tk = akt.shape[2]
    tq = ATT_Q_TILE
    return pl.pallas_call(
        _attention_kernel,
        grid=(b, t // tq),
        in_specs=[pl.BlockSpec((1, tq, ATT_Q_W), lambda bi, i: (bi, i, 0)),
                  pl.BlockSpec((1, ATT_K_W, tk), lambda bi, i: (bi, 0, 0)),
                  pl.BlockSpec((1, tk, av_ones.shape[2]), lambda bi, i: (bi, 0, 0))],
        out_specs=pl.BlockSpec((1, tq, ATT_Q_W), lambda bi, i: (bi, i, 0)),
        out_shape=jax.ShapeDtypeStruct((b, t, ATT_Q_W), BF16),
        scratch_shapes=[pltpu.VMEM((2, tq, tk), F32), pltpu.VMEM((2, tq, tk), BF16)],
        compiler_params=_cparams("parallel", "parallel"),
        name="attention",
    )(aq, akt, av_ones)


def _post_mixer_kernel(ret_ref, att_ref, x_ref, mod_ref, lng_ref, lnb_ref, wo_ref, l1g_ref, l1b_ref,
                       rwh_ref, rwl_ref, rb_ref, before_ref, below_ref,
                       x1_ref, h2_ref, gate_ref, lrow_ref, cnt_ref, lstart_ref, tbase_ref, base_acc):
    first = (pl.program_id(0) == 0) & (pl.program_id(1) == 0)

    @pl.when(first)
    def _():
        base_acc[...] = jnp.zeros_like(base_acc)

    half = ret_ref.shape[2]
    y = (jnp.dot(ret_ref[0], wo_ref[0:half, :], preferred_element_type=F32)
         + jnp.dot(att_ref[0], wo_ref[half:, :], preferred_element_type=F32))
    xn = _layer_norm(x_ref[0], lng_ref[...], lnb_ref[...])
    x1 = _layer_norm(DN_ALPHA * xn + mod_ref[0, 2:3, :] * y, l1g_ref[...], l1b_ref[...])
    x1_ref[...] = x1
    h2 = x1 * (1.0 + mod_ref[0, 4:5, :]) + mod_ref[0, 3:4, :]
    h_hi = h2.astype(BF16)
    h2_ref[...] = h_hi

    h_lo = (h2 - h_hi.astype(F32)).astype(BF16)
    logits = (jnp.dot(h_hi, rwh_ref[...], preferred_element_type=F32)
              + jnp.dot(h_lo, rwh_ref[...], preferred_element_type=F32)
              + jnp.dot(h_hi, rwl_ref[...], preferred_element_type=F32)) + rb_ref[...]

    tt = logits.shape[0]
    work = logits.T[0:N_EXPERTS, :]
    e_iota = lax.broadcasted_iota(jnp.int32, (N_EXPERTS, tt), 0)
    sels, vals = [], []
    onehot = jnp.zeros((N_EXPERTS, tt), F32)
    for _ in range(TOP_K):
        m = jnp.max(work, 0, keepdims=True)
        idx = jnp.min(jnp.where(work == m, e_iota, N_EXPERTS), 0, keepdims=True)
        sel = e_iota == idx
        sels.append(sel)
        vals.append(m)
        onehot = onehot + sel.astype(F32)
        work = jnp.where(sel, -jnp.inf, work)
    exps = [jnp.exp(v - vals[0]) for v in vals]
    denom = exps[0] + exps[1] + exps[2] + exps[3]

    count = jnp.sum(onehot, 1, keepdims=True)
    units = jnp.floor((count + (SEG_ALIGN - 1.0)) * (1.0 / SEG_ALIGN))
    lstart = SEG_ALIGN * jnp.dot(below_ref[...], jnp.broadcast_to(units, (N_EXPERTS, LANES)).astype(BF16),
                                 preferred_element_type=F32)[:, 0:1]
    rank = jnp.dot(onehot.astype(BF16), before_ref[...], preferred_element_type=F32) + lstart

    k_iota = lax.broadcasted_iota(jnp.int32, (TOP_K, tt), 0)
    gate_out = jnp.zeros((TOP_K, tt), F32)
    lrow_out = jnp.zeros((TOP_K, tt), F32)
    for k in range(TOP_K):
        pk = jnp.sum(jnp.where(sels[k], rank, 0.0), 0, keepdims=True)
        gate_out = jnp.where(k_iota == k, exps[k] / denom, gate_out)
        lrow_out = jnp.where(k_iota == k, pk, lrow_out)
    gate_ref[0] = gate_out
    lrow_ref[0] = lrow_out.astype(jnp.int32)
    cnt_ref[0] = (units * SEG_ALIGN).astype(jnp.int32)
    lstart_ref[0] = lstart.astype(jnp.int32)
    tbase_ref[0] = base_acc[...].astype(jnp.int32)
    base_acc[...] += units * SEG_ALIGN


def _post_mixer(ret, att, x, mod, lng, lnb, wo_bf, l1g, l1b, rw_hi, rw_lo, rb, before, below):
    b, t, d = x.shape
    tt = TOKEN_TILE
    nt = t // tt
    n = b * t
    const2 = lambda bi, i: (0, 0)
    tok3 = lambda bi, i: (bi, i, 0)
    flat = lambda bi, i: (bi * nt + i, 0)
    tile3 = lambda bi, i: (bi * nt + i, 0, 0)
    vec = pl.BlockSpec((1, d), const2)
    table = jax.ShapeDtypeStruct((b * nt, N_EXPERTS, 1), jnp.int32)
    table_spec = pl.BlockSpec((1, N_EXPERTS, 1), tile3)
    per_k = pl.BlockSpec((1, TOP_K, tt), tile3)
    return pl.pallas_call(
        _post_mixer_kernel,
        grid=(b, nt),
        in_specs=[pl.BlockSpec((1, tt, ret.shape[2]), tok3),
                  pl.BlockSpec((1, tt, att.shape[2]), tok3),
                  pl.BlockSpec((1, tt, d), tok3),
                  pl.BlockSpec((1, 6, d), lambda bi, i: (bi, 0, 0)),
                  vec, vec,
                  pl.BlockSpec(wo_bf.shape, const2),
                  vec, vec,
                  pl.BlockSpec(rw_hi.shape, const2),
                  pl.BlockSpec(rw_lo.shape, const2),
                  pl.BlockSpec(rb.shape, const2),
                  pl.BlockSpec((tt, tt), const2),
                  pl.BlockSpec((N_EXPERTS, N_EXPERTS), const2)],
        out_specs=[pl.BlockSpec((tt, d), flat),
                   pl.BlockSpec((tt, d), flat),
                   per_k, per_k, table_spec, table_spec, table_spec],
        out_shape=[jax.ShapeDtypeStruct((n, d), F32),
                   jax.ShapeDtypeStruct((n, d), BF16),
                   jax.ShapeDtypeStruct((b * nt, TOP_K, tt), F32),
                   jax.ShapeDtypeStruct((b * nt, TOP_K, tt), jnp.int32),
                   table, table, table],
        scratch_shapes=[pltpu.VMEM((N_EXPERTS, 1), F32)],
        compiler_params=_cparams("arbitrary", "arbitrary"),
        name="post_mixer",
    )(ret, att, x, mod, lng, lnb, wo_bf, l1g, l1b, rw_hi, rw_lo, rb, before, below)


def _segment_starts(cnt_ref, lstart_ref, dst_ref, tile, make_copy):
    def body(e, carry):
        t = tile * N_EXPERTS + e
        n = cnt_ref[t]

        @pl.when(n > 0)
        def _():
            make_copy(lstart_ref[t], dst_ref[t], n).start()
        return carry
    lax.fori_loop(0, N_EXPERTS, body, 0, unroll=4)


def _segment_waits(total_ref, tile, make_copy):
    make_copy(0, 0, total_ref[tile]).wait()


def _zero_fill_copies(tail_len_ref, tail_dst_ref, nact_ref, zbuf, x_hbm, zsem, act):
    block = zbuf.shape[0]

    def tail_body(e, carry):
        n = tail_len_ref[e]

        @pl.when(n > 0)
        def _():
            copy = pltpu.make_async_copy(zbuf.at[pl.ds(0, n)], x_hbm.at[pl.ds(tail_dst_ref[e], n)], zsem)
            getattr(copy, act)()
        return carry
    lax.fori_loop(0, N_EXPERTS, tail_body, 0)

    def block_body(j, carry):
        copy = pltpu.make_async_copy(zbuf, x_hbm.at[pl.ds(j * block, block)], zsem)
        getattr(copy, act)()
        return carry
    lax.fori_loop(nact_ref[0], x_hbm.shape[0] // block, block_body, 0)


def _dispatch_kernel(cnt_ref, lstart_ref, dst_ref, total_ref, tail_len_ref, tail_dst_ref, nact_ref,
                     h_ref, lrow_ref, x_hbm, sbuf, sem, zbuf, zsem):
    i = pl.program_id(0)
    last = pl.num_programs(0) - 1
    slot = i % 2
    rows = sbuf.shape[1] * SEG_ALIGN
    tt = h_ref.shape[0]
    d = h_ref.shape[1]
    chunk_groups = PERM_CHUNK // SEG_ALIGN

    def make_copy(tile_slot):
        def mk(ls, ds, size):
            return pltpu.make_async_copy(sbuf.at[tile_slot, pl.ds(ls, size)], x_hbm.at[pl.ds(ds, size)],
                                         sem.at[tile_slot])
        return mk

    @pl.when(i == 0)
    def _():
        zbuf[...] = jnp.zeros_like(zbuf)
        _zero_fill_copies(tail_len_ref, tail_dst_ref, nact_ref, zbuf, x_hbm, zsem, "start")

    def perm(j):
        r_iota = lax.broadcasted_iota(jnp.int32, (PERM_CHUNK, tt), 0) + j * PERM_CHUNK
        hit = r_iota == lrow_ref[0, 0:1, :]
        for k in range(1, TOP_K):
            hit = hit | (r_iota == lrow_ref[0, k:k + 1, :])
        return jnp.where(hit, 1.0, 0.0).astype(BF16)

    p_next = perm(0)
    for j in range(rows // PERM_CHUNK):
        p_cur = p_next
        if (j + 1) * PERM_CHUNK < rows:
            p_next = perm(j + 1)
        sbuf[slot, j * chunk_groups:(j + 1) * chunk_groups] = jnp.dot(
            p_cur, h_ref[...], preferred_element_type=F32).astype(BF16).reshape(chunk_groups, SEG_ALIGN, d)
    _segment_starts(cnt_ref, lstart_ref, dst_ref, i, make_copy(slot))

    @pl.when(i > 0)
    def _():
        _segment_waits(total_ref, i - 1, make_copy(1 - slot))

    @pl.when(i == last)
    def _():
        _segment_waits(total_ref, i, make_copy(slot))
        _zero_fill_copies(tail_len_ref, tail_dst_ref, nact_ref, zbuf, x_hbm, zsem, "wait")


def _dispatch(cnt, lstart, dst, total, tail_len, tail_dst, n_active, h2, lrow_t, n_rows):
    n, d = h2.shape
    tt = TOKEN_TILE
    grid_spec = pltpu.PrefetchScalarGridSpec(
        num_scalar_prefetch=7,
        grid=(n // tt,),
        in_specs=[pl.BlockSpec((tt, d), lambda i, *_: (i, 0)),
                  pl.BlockSpec((1, TOP_K, tt), lambda i, *_: (i, 0, 0))],
        out_specs=pl.BlockSpec(memory_space=pl.ANY),
        scratch_shapes=[pltpu.VMEM((2, TILE_SORTED_ROWS // SEG_ALIGN, SEG_ALIGN, d), BF16),
                        pltpu.SemaphoreType.DMA((2,)),
                        pltpu.VMEM((EXPERT_ROWS // SEG_ALIGN, SEG_ALIGN, d), BF16),
                        pltpu.SemaphoreType.DMA(())])
    return pl.pallas_call(
        _dispatch_kernel,
        grid_spec=grid_spec,
        out_shape=jax.ShapeDtypeStruct((n_rows // SEG_ALIGN, SEG_ALIGN, d), BF16),
        compiler_params=_cparams("arbitrary"),
        name="dispatch",
    )(cnt, lstart, dst, total, tail_len, tail_dst, n_active, h2, lrow_t)


def _experts_kernel(be_ref, nact_ref, x_ref, wup_ref, bup_ref, wdn_ref, bdn_ref, y_ref,
                    wup_bf, wdn_bf):
    j = pl.program_id(0)
    dff = wdn_ref.shape[1]

    @pl.when(j < nact_ref[0])
    def _():
        prev = be_ref[jnp.maximum(j - 1, 0)]

        @pl.when((j == 0) | (be_ref[j] != prev))
        def _():
            wup_bf[...] = wup_ref[0].astype(BF16)
            wdn_bf[...] = wdn_ref[0].astype(BF16)

        x = x_ref[...].reshape(-1, x_ref.shape[2])
        u = jnp.dot(x, wup_bf[...], preferred_element_type=F32) + bup_ref[0]
        glu = jnp.minimum(u[:, :dff], SWIGLU_LIMIT)
        lin = jnp.clip(u[:, dff:], -SWIGLU_LIMIT, SWIGLU_LIMIT)
        act = glu * (1.0 / (1.0 + jnp.exp(-SWIGLU_ALPHA * glu))) * (lin + 1.0)
        y = jnp.dot(act.astype(BF16), wdn_bf[...], preferred_element_type=F32) + bdn_ref[0]
        y_ref[...] = y.astype(y_ref.dtype).reshape(y_ref.shape)

    @pl.when(j >= nact_ref[0])
    def _():
        y_ref[...] = jnp.zeros_like(y_ref)


def _experts(block_expert, n_active, x_sorted, w_up, b_up, w_down, b_down):
    n_blocks = block_expert.shape[0]
    tm = EXPERT_ROWS
    d = x_sorted.shape[2]
    e, _, up_w = w_up.shape
    dff = w_down.shape[1]
    by_expert = lambda j, be, na: (be[j], 0, 0)
    x_map = lambda j, be, na: (jnp.minimum(j, na[0] - 1), 0, 0)
    rows3 = (tm // SEG_ALIGN, SEG_ALIGN, d)
    grid_spec = pltpu.PrefetchScalarGridSpec(
        num_scalar_prefetch=2,
        grid=(n_blocks,),
        in_specs=[pl.BlockSpec(rows3, x_map),
                  pl.BlockSpec((1, d, up_w), by_expert),
                  pl.BlockSpec((1, 1, up_w), by_expert),
                  pl.BlockSpec((1, dff, d), by_expert),
                  pl.BlockSpec((1, 1, d), by_expert)],
        out_specs=pl.BlockSpec(rows3, lambda j, be, na: (j, 0, 0)),
        scratch_shapes=[pltpu.VMEM((d, up_w), BF16),
                        pltpu.VMEM((dff, d), BF16)])
    return pl.pallas_call(
        _experts_kernel,
        grid_spec=grid_spec,
        out_shape=jax.ShapeDtypeStruct(x_sorted.shape, BF16),
        compiler_params=_cparams("arbitrary"),
        name="experts",
    )(block_expert, n_active, x_sorted, w_up, b_up.reshape(e, 1, up_w), w_down,
      b_down.reshape(e, 1, d))


def _combine_kernel(cnt_ref, lstart_ref, dst_ref, total_ref, lrow_ref, gate_ref, x1_ref, mod_ref, g_ref, b_ref,
                    y_hbm, o_ref, ybuf, sem):
    i = pl.program_id(0)
    last = pl.num_programs(0) - 1
    slot = i % 2
    rows = ybuf.shape[1] * SEG_ALIGN
    tt = x1_ref.shape[0]
    d = x1_ref.shape[1]
    chunk_groups = PERM_CHUNK // SEG_ALIGN

    def make_copy(tile_slot):
        def mk(ls, ds, size):
            return pltpu.make_async_copy(y_hbm.at[pl.ds(ds, size)], ybuf.at[tile_slot, pl.ds(ls, size)],
                                         sem.at[tile_slot])
        return mk

    @pl.when(i == 0)
    def _():
        ybuf[...] = jnp.zeros_like(ybuf)
        _segment_starts(cnt_ref, lstart_ref, dst_ref, i, make_copy(slot))

    @pl.when(i < last)
    def _():
        _segment_starts(cnt_ref, lstart_ref, dst_ref, i + 1, make_copy(1 - slot))

    def weights(j):
        c_iota = lax.broadcasted_iota(jnp.int32, (tt, PERM_CHUNK), 1) + j * PERM_CHUNK
        w = jnp.zeros((tt, PERM_CHUNK), F32)
        for k in range(TOP_K):
            w = w + jnp.where(c_iota == lrow_ref[:, k:k + 1], gate_ref[:, k:k + 1], 0.0)
        return w.astype(BF16)

    w_next = weights(0)
    _segment_waits(total_ref, i, make_copy(slot))
    f = None
    for j in range(rows // PERM_CHUNK):
        w_cur = w_next
        if (j + 1) * PERM_CHUNK < rows:
            w_next = weights(j + 1)
        y_chunk = ybuf[slot, j * chunk_groups:(j + 1) * chunk_groups].reshape(PERM_CHUNK, d)
        part = jnp.dot(w_cur, y_chunk, preferred_element_type=F32)
        f = part if f is None else f + part
    o_ref[...] = _layer_norm(DN_ALPHA * x1_ref[...] + mod_ref[0, 5:6, :] * f, g_ref[...], b_ref[...])


def _combine(cnt, lstart, dst, total, lrow, gates, x1, mod, g, b, y_sorted, tokens_per_sample):
    n, d = x1.shape
    tt = TOKEN_TILE
    tiles_per_sample = tokens_per_sample // tt
    const2 = lambda i, *_: (0, 0)
    tok = lambda i, *_: (i, 0)
    grid_spec = pltpu.PrefetchScalarGridSpec(
        num_scalar_prefetch=4,
        grid=(n // tt,),
        in_specs=[pl.BlockSpec((tt, TOP_K), tok),
                  pl.BlockSpec((tt, TOP_K), tok),
                  pl.BlockSpec((tt, d), tok),
                  pl.BlockSpec((1, 6, d), lambda i, *_: (i // tiles_per_sample, 0, 0)),
                  pl.BlockSpec((1, d), const2),
                  pl.BlockSpec((1, d), const2),
                  pl.BlockSpec(memory_space=pl.ANY)],
        out_specs=pl.BlockSpec((tt, d), tok),
        scratch_shapes=[pltpu.VMEM((2, TILE_SORTED_ROWS // SEG_ALIGN, SEG_ALIGN, d), BF16),
                        pltpu.SemaphoreType.DMA((2,))])
    return pl.pallas_call(
        _combine_kernel,
        grid_spec=grid_spec,
        out_shape=jax.ShapeDtypeStruct((n, d), F32),
        compiler_params=_cparams("arbitrary"),
        name="combine",
    )(cnt, lstart, dst, total, lrow, gates, x1, mod, g, b, y_sorted)


def _rope_tables(t):
    rows = jnp.repeat(jnp.arange(t // GRID_W, dtype=jnp.int32), GRID_W).astype(F32)
    cols = jnp.tile(jnp.arange(GRID_W, dtype=jnp.int32), t // GRID_W).astype(F32)
    n_freq = HEAD_DIM // 4
    inv = ROPE_BASE ** (-jnp.arange(n_freq, dtype=F32) / n_freq)
    ang_r = rows[:, None] * inv
    ang_c = cols[:, None] * inv
    ang = jnp.concatenate([ang_r, ang_r, ang_c, ang_c], -1)
    ang = jnp.concatenate([ang, ang], -1)
    cos, sin = jnp.cos(ang), jnp.sin(ang)
    first_half = (jnp.arange(LANES) % 32) < 16
    return cos, jnp.where(first_half, -sin, 0.0), jnp.where(first_half, 0.0, sin)


def kernel(x, c, ctx, c_ctx, ln_in_g, ln_in_b, w_ada, b_ada, w_in, ret_log_decay_f, ret_log_decay_b,
           ret_gn_g, ret_gn_b, q_norm_g, k_norm_g, w_o, ln1_g, ln1_b, router_w, router_b,
           w_up, b_up, w_down, b_down, ln2_g, ln2_b):
    b, t, d = x.shape
    n = b * t
    row = lambda v: v.reshape(1, -1)

    cc = jnp.concatenate([c, c_ctx[None]], 0)
    cc = jnp.pad(cc, ((0, (-cc.shape[0]) % 8), (0, 0)))
    mod_all = _ada_mod(cc, w_ada[0], row(b_ada[0]))
    mod = mod_all[:b].reshape(b, 6, d)
    mod_ctx = mod_all[b:b + 1].reshape(1, 6, d)

    w_in_bf = w_in[0].astype(BF16)
    lane = jnp.arange(LANES)
    seg = (lane[:, None] // HEAD_DIM == lane[None, :] // HEAD_DIM).astype(BF16)
    qg = jnp.tile(q_norm_g[0], ATT_HEADS).reshape(1, -1)
    kg = jnp.tile(k_norm_g[0], ATT_KV_HEADS).reshape(1, -1)
    lng, lnb = row(ln_in_g), row(ln_in_b)

    rq, rk, rv, rg, aq, akt, av = _in_proj(x, mod, lng, lnb, w_in_bf, seg, qg, kg, _rope_tables(t))
    _, rkc, rvc, _, _, aktc, avc = _in_proj(ctx, mod_ctx, lng, lnb, w_in_bf, seg, qg, kg, None)

    ret = _retention(row(ret_log_decay_f[0]), row(ret_log_decay_b[0]), rq, rk, rv, rg, rkc, rvc,
                     row(ret_gn_g[0]), row(ret_gn_b[0]))
    tk = ctx.shape[1] + t
    av_ones = jnp.concatenate([jnp.concatenate([avc, av], 1), jnp.ones((b, tk, LANES), BF16)], 2)
    att = _attention(aq, jnp.concatenate([aktc, akt], 2), av_ones)

    rw = jnp.pad(router_w[0], ((0, 0), (0, LANES - N_EXPERTS)))
    rb = jnp.pad(row(router_b[0]), ((0, 0), (0, LANES - N_EXPERTS)))
    rw_hi = rw.astype(BF16)
    rw_lo = (rw - rw_hi.astype(F32)).astype(BF16)
    tt = TOKEN_TILE
    n_tiles = n // tt
    before = jnp.triu(jnp.ones((tt, tt), BF16), 1)
    below = jnp.tril(jnp.ones((N_EXPERTS, N_EXPERTS), BF16), -1)
    x1, h2, gates_t, lrow_t, cnt, lstart, tbase = _post_mixer(
        ret, att, x, mod, lng, lnb, w_o[0].astype(BF16), row(ln1_g[0]), row(ln1_b[0]),
        rw_hi, rw_lo, rb, before, below)
    token_major = lambda v: v.transpose(0, 2, 1).reshape(n, TOP_K)
    gates, lrow = token_major(gates_t), token_major(lrow_t)

    tm = EXPERT_ROWS
    cnt, lstart, tbase = cnt[:, :, 0], lstart[:, :, 0], tbase[:, :, 0]
    totals = tbase[-1] + cnt[-1]
    exp_rows = (totals + tm - 1) // tm * tm
    exp_end = jnp.cumsum(exp_rows)
    exp_start = exp_end - exp_rows
    dst = (exp_start[None, :] + tbase).reshape(-1)
    max_rows = n * TOP_K + n_tiles * N_EXPERTS * (SEG_ALIGN - 1) + N_EXPERTS * (tm - 1)
    n_blocks = -(-max_rows // tm)
    block_start = jnp.arange(n_blocks, dtype=jnp.int32) * tm
    block_expert = jnp.minimum(jnp.sum(exp_end[None, :] <= block_start[:, None], axis=1),
                               N_EXPERTS - 1).astype(jnp.int32)
    n_active = (exp_end[-1:] // tm).astype(jnp.int32)
    groups = lambda v: (v // SEG_ALIGN).astype(jnp.int32).reshape(-1)
    cnt_g, lstart_g, dst_g, total_g = groups(cnt), groups(lstart), groups(dst), groups(jnp.sum(cnt, axis=1))

    x_sorted = _dispatch(cnt_g, lstart_g, dst_g, total_g, groups(exp_rows - totals), groups(exp_start + totals),
                         n_active, h2, lrow_t, n_blocks * tm)
    y_sorted = _experts(block_expert, n_active, x_sorted, w_up[0], b_up[0], w_down[0], b_down[0])
    out = _combine(cnt_g, lstart_g, dst_g, total_g, lrow, gates, x1, mod, row(ln2_g[0]), row(ln2_b[0]),
                   y_sorted, t)
    return out.reshape(b, t, d)
```

```python
import functools

import jax
import jax.numpy as jnp
from jax import lax
from jax.experimental import pallas as pl
from jax.experimental.pallas import tpu as pltpu

GRID_W = 64
HEAD_DIM = 64
ROPE_BASE = 10000.0
LN_EPS = 1e-6
RET_HEADS = 4
RET_DK = 64
RET_DV = 128
RET_CHUNK = 128
ATT_HEADS = 8
ATT_KV_HEADS = 2
ATT_GROUP = ATT_HEADS // ATT_KV_HEADS
N_EXPERTS = 32
TOP_K = 4
SWIGLU_LIMIT = 7.0
SWIGLU_ALPHA = 1.702
DEPTH = 1
DN_ALPHA = (2.0 * DEPTH) ** 0.25
LOG2_E = 1.4426950408889634

RET_Q_W = RET_HEADS * RET_DK
RET_V_W = RET_HEADS * RET_DV
ATT_Q_W = ATT_HEADS * HEAD_DIM
ATT_K_W = ATT_KV_HEADS * HEAD_DIM

LANES = 128
VMEM_LIMIT_BYTES = 56 * 1024 * 1024

TOKEN_TILE = 512
ATT_Q_TILE = 256
EXPERT_ROWS = 512
SEG_ALIGN = 16
TILE_SORTED_ROWS = 2560
PERM_CHUNK = 512

F32 = jnp.float32
BF16 = jnp.bfloat16


def _cparams(*sem):
    return pltpu.CompilerParams(dimension_semantics=sem, vmem_limit_bytes=VMEM_LIMIT_BYTES)


def _layer_norm(x, g, b):
    mu = jnp.mean(x, -1, keepdims=True)
    xc = x - mu
    var = jnp.mean(xc * xc, -1, keepdims=True)
    return xc * lax.rsqrt(var + LN_EPS) * g + b


def _silu(x):
    return x * (1.0 / (1.0 + jnp.exp(-x)))


def _ada_kernel(c_ref, w_ref, b_ref, o_ref):
    a = _silu(c_ref[...])
    o_ref[...] = jnp.dot(a, w_ref[...], preferred_element_type=F32,
                         precision=lax.Precision.HIGHEST) + b_ref[...]


def _ada_mod(cc, w, b):
    m, d = cc.shape
    n = w.shape[1]
    tn = 1536
    return pl.pallas_call(
        _ada_kernel,
        grid=(n // tn,),
        in_specs=[pl.BlockSpec((m, d), lambda j: (0, 0)),
                  pl.BlockSpec((d, tn), lambda j: (0, j)),
                  pl.BlockSpec((1, tn), lambda j: (0, j))],
        out_specs=pl.BlockSpec((m, tn), lambda j: (0, j)),
        out_shape=jax.ShapeDtypeStruct((m, n), F32),
        compiler_params=_cparams("arbitrary"),
        name="ada_mod",
    )(cc, w, b)


def _rope(x, cos, sinn, sinp):
    outs = []
    for j in range(x.shape[1] // LANES):
        xg = x[:, j * LANES:(j + 1) * LANES]
        outs.append(xg * cos + pltpu.roll(xg, LANES - 16, 1) * sinn + pltpu.roll(xg, 16, 1) * sinp)
    return outs[0] if len(outs) == 1 else jnp.concatenate(outs, axis=1)


def _head_rms(x, seg, gain):
    outs = []
    for j in range(x.shape[1] // LANES):
        xg = x[:, j * LANES:(j + 1) * LANES]
        sq = xg * xg
        hi = sq.astype(BF16)
        lo = (sq - hi.astype(F32)).astype(BF16)
        ms = (jnp.dot(hi, seg, preferred_element_type=F32)
              + jnp.dot(lo, seg, preferred_element_type=F32)) * (1.0 / HEAD_DIM)
        outs.append(xg * lax.rsqrt(ms + LN_EPS) * gain[:, j * LANES:(j + 1) * LANES])
    return outs[0] if len(outs) == 1 else jnp.concatenate(outs, axis=1)


def _in_proj_kernel(*refs, rope):
    if rope:
        (x_ref, mod_ref, lng_ref, lnb_ref, w_ref, seg_ref, qg_ref, kg_ref, cos_ref, sinn_ref, sinp_ref,
         rq_ref, rkt_ref, rv_ref, rg_ref, aq_ref, akt_ref, av_ref) = refs
    else:
        (x_ref, mod_ref, lng_ref, lnb_ref, w_ref, seg_ref, qg_ref, kg_ref,
         rq_ref, rkt_ref, rv_ref, rg_ref, aq_ref, akt_ref, av_ref) = refs
    x = x_ref[0]
    xn = _layer_norm(x, lng_ref[...], lnb_ref[...])
    h = (xn * (1.0 + mod_ref[0, 1:2, :]) + mod_ref[0, 0:1, :]).astype(BF16)
    seg = seg_ref[...]

    def rotate(v):
        return _rope(v, cos_ref[...], sinn_ref[...], sinp_ref[...]) if rope else v

    o_rv = 2 * RET_Q_W
    o_aq = o_rv + 2 * RET_V_W
    o_ak = o_aq + ATT_Q_W
    o_end = o_ak + 2 * ATT_K_W

    def project(lo, hi):
        return jnp.dot(h, w_ref[:, lo:hi], preferred_element_type=F32)

    def finish_aq(u):
        aq = rotate(_head_rms(u, seg, qg_ref[...]))
        aq_ref[0] = (aq * (HEAD_DIM ** -0.5 * LOG2_E)).astype(BF16)

    def finish_akv(u):
        ak = rotate(_head_rms(u[:, :ATT_K_W], seg, kg_ref[...]))
        akt_ref[0] = ak.T.astype(BF16)
        av_ref[0] = u[:, ATT_K_W:].astype(BF16)

    def finish_rqk(u):
        rq_ref[0] = rotate(u[:, :RET_Q_W])
        rkt_ref[0] = (rotate(u[:, RET_Q_W:]) * (RET_DK ** -0.5)).T

    def finish_rvg(u):
        rv_ref[0] = u[:, :RET_V_W].astype(BF16)
        rg_ref[0] = u[:, RET_V_W:].astype(BF16)

    stages = [((o_aq, o_ak), finish_aq), ((o_ak, o_end), finish_akv),
              ((0, o_rv), finish_rqk), ((o_rv, o_aq), finish_rvg)]
    u_next = project(*stages[0][0])
    for idx, (_, finish) in enumerate(stages):
        u_cur = u_next
        if idx + 1 < len(stages):
            u_next = project(*stages[idx + 1][0])
        finish(u_cur)


def _in_proj(x, mod, lng, lnb, w_bf, seg, qg, kg, tables):
    b, t, d = x.shape
    tt = min(TOKEN_TILE, t)
    rope = tables is not None
    mod_b = mod.shape[0]
    mod_map = (lambda bi, i: (bi, 0, 0)) if mod_b > 1 else (lambda bi, i: (0, 0, 0))
    const2 = lambda bi, i: (0, 0)
    tok3 = lambda bi, i: (bi, i, 0)
    in_specs = [pl.BlockSpec((1, tt, d), tok3),
                pl.BlockSpec((1, 6, d), mod_map),
                pl.BlockSpec((1, d), const2),
                pl.BlockSpec((1, d), const2),
                pl.BlockSpec(w_bf.shape, const2),
                pl.BlockSpec(seg.shape, const2),
                pl.BlockSpec(qg.shape, const2),
                pl.BlockSpec(kg.shape, const2)]
    args = [x, mod, lng, lnb, w_bf, seg, qg, kg]
    if rope:
        in_specs += [pl.BlockSpec((tt, LANES), lambda bi, i: (i, 0))] * 3
        args += list(tables)
    out_shape = [jax.ShapeDtypeStruct((b, t, RET_Q_W), F32),
                 jax.ShapeDtypeStruct((b, RET_Q_W, t), F32),
                 jax.ShapeDtypeStruct((b, t, RET_V_W), BF16),
                 jax.ShapeDtypeStruct((b, t, RET_V_W), BF16),
                 jax.ShapeDtypeStruct((b, t, ATT_Q_W), BF16),
                 jax.ShapeDtypeStruct((b, ATT_K_W, t), BF16),
                 jax.ShapeDtypeStruct((b, t, ATT_K_W), BF16)]
    out_specs = [pl.BlockSpec((1, tt, RET_Q_W), tok3),
                 pl.BlockSpec((1, RET_Q_W, tt), lambda bi, i: (bi, 0, i)),
                 pl.BlockSpec((1, tt, RET_V_W), tok3),
                 pl.BlockSpec((1, tt, RET_V_W), tok3),
                 pl.BlockSpec((1, tt, ATT_Q_W), tok3),
                 pl.BlockSpec((1, ATT_K_W, tt), lambda bi, i: (bi, 0, i)),
                 pl.BlockSpec((1, tt, ATT_K_W), tok3)]
    return pl.pallas_call(
        functools.partial(_in_proj_kernel, rope=rope),
        grid=(b, t // tt),
        in_specs=in_specs,
        out_specs=out_specs,
        out_shape=out_shape,
        compiler_params=_cparams("parallel", "parallel"),
        name="in_proj_rope" if rope else "in_proj_ctx",
    )(*args)


def _retention_kernel(lgf_ref, lgb_ref, q_ref, kt_ref, v_ref, g_ref, ktc_ref, vc_ref, gng_ref, gnb_ref,
                      o_ref, kv_ref, rhs_ref):
    pair = pl.program_id(1)
    t = q_ref.shape[1]
    n_chunks = t // RET_CHUNK
    ctx_len = ktc_ref.shape[2]
    c = RET_CHUNK
    dk = RET_DK
    pos_col = lax.broadcasted_iota(jnp.int32, (c, 1), 0).astype(F32)
    pos_row = lax.broadcasted_iota(jnp.int32, (1, c), 1).astype(F32)
    row = lax.broadcasted_iota(jnp.int32, (c, c), 0)
    col = lax.broadcasted_iota(jnp.int32, (c, c), 1)
    diff = (row - col).astype(F32)
    lane = lax.broadcasted_iota(jnp.int32, (c, 2 * dk), 1)
    cpos = lax.broadcasted_iota(jnp.int32, (1, ctx_len), 1).astype(F32)

    heads = []
    for hh in range(2):
        head = pair * 2 + hh
        lgf = lgf_ref[0, head]
        lgb = lgb_ref[0, head]
        heads.append(dict(
            kl=slice(hh * dk, (hh + 1) * dk),
            vl=slice(hh * RET_DV, (hh + 1) * RET_DV),
            lgf=lgf, lgb=lgb,
            kdf=jnp.exp(lgf * (c - 1.0 - pos_row)),
            kdb=jnp.exp(lgb * pos_row),
            qd=jnp.where(lane < dk, jnp.exp(lgf * (pos_col + 1.0)), jnp.exp(lgb * (c - pos_col))),
            decay=(jnp.where(diff >= 0, jnp.exp(lgf * jnp.maximum(diff, 0.0)), 0.0)
                   + jnp.where(diff <= 0, jnp.exp(lgb * jnp.maximum(-diff, 0.0)), 0.0))))

    def kv_body(n, carry):
        r = pl.ds(pl.multiple_of(n * c, c), c)
        for hh, hd in enumerate(heads):
            kt = kt_ref[0, hd["kl"], r]
            vc = v_ref[0, r, hd["vl"]]
            both = jnp.concatenate([kt * hd["kdf"], kt * hd["kdb"]], axis=0).astype(BF16)
            kv_ref[hh, n] = jnp.dot(both, vc, preferred_element_type=F32)
            rhs_ref[hh, n, 0:c, :] = vc
        return carry

    lax.fori_loop(0, n_chunks, kv_body, 0, unroll=4)

    for hh, hd in enumerate(heads):
        lgf, lgb = hd["lgf"], hd["lgb"]
        ktc = ktc_ref[0, hd["kl"], :]
        both = jnp.concatenate([ktc * jnp.exp(lgf * (ctx_len - 1.0 - cpos)), ktc * jnp.exp(lgb * cpos)],
                               axis=0).astype(BF16)
        s_ctx = jnp.dot(both, vc_ref[0, :, hd["vl"]], preferred_element_type=F32)
        s_f, s_b = s_ctx[0:dk], s_ctx[dk:]
        gfc = jnp.exp(lgf * c)
        gbc = jnp.exp(lgb * c)
        for n in range(n_chunks):
            rhs_ref[hh, n, c:c + dk, :] = s_f.astype(BF16)
            s_f = gfc * s_f + kv_ref[hh, n, 0:dk, :]
        for n in range(n_chunks - 1, -1, -1):
            rhs_ref[hh, n, c + dk:, :] = s_b.astype(BF16)
            s_b = gbc * s_b + kv_ref[hh, n, dk:, :]

    def out_body(n, carry):
        r = pl.ds(pl.multiple_of(n * c, c), c)
        qq = q_ref[0, r, :]
        swapped = pltpu.roll(qq, dk, 1)
        for hh, hd in enumerate(heads):
            vl = hd["vl"]
            s = jnp.dot(q_ref[0, r, hd["kl"]].astype(BF16), kt_ref[0, hd["kl"], r].astype(BF16),
                        preferred_element_type=F32)
            q2 = jnp.where(lane < dk, qq, swapped) if hh == 0 else jnp.where(lane < dk, swapped, qq)
            lhs = jnp.concatenate([(s * hd["decay"]).astype(BF16), (q2 * hd["qd"]).astype(BF16)], axis=1)
            y = jnp.dot(lhs, rhs_ref[hh, n], preferred_element_type=F32)
            mu = jnp.mean(y, -1, keepdims=True)
            yc = y - mu
            var = jnp.mean(yc * yc, -1, keepdims=True)
            yn = yc * lax.rsqrt(var + LN_EPS) * gng_ref[:, vl] + gnb_ref[:, vl]
            o_ref[0, r, vl] = (yn * _silu(g_ref[0, r, vl].astype(F32))).astype(o_ref.dtype)
        return carry

    lax.fori_loop(0, n_chunks, out_body, 0, unroll=8)


def _retention(lgf, lgb, rq, rkt, rv, rg, rktc, rvc, gng, gnb):
    b, t, _ = rq.shape
    ctx_len = rktc.shape[2]
    n_chunks = t // RET_CHUNK
    pair3 = lambda bi, p: (bi, 0, p)
    pair_rows = lambda bi, p: (bi, p, 0)
    smem = pl.BlockSpec(memory_space=pltpu.SMEM)
    return pl.pallas_call(
        _retention_kernel,
        grid=(b, RET_HEADS // 2),
        in_specs=[smem, smem,
                  pl.BlockSpec((1, t, 2 * RET_DK), pair3),
                  pl.BlockSpec((1, 2 * RET_DK, t), pair_rows),
                  pl.BlockSpec((1, t, 2 * RET_DV), pair3),
                  pl.BlockSpec((1, t, 2 * RET_DV), pair3),
                  pl.BlockSpec((1, 2 * RET_DK, ctx_len), pair_rows),
                  pl.BlockSpec((1, ctx_len, 2 * RET_DV), pair3),
                  pl.BlockSpec((1, 2 * RET_DV), lambda bi, p: (0, p)),
                  pl.BlockSpec((1, 2 * RET_DV), lambda bi, p: (0, p))],
        out_specs=pl.BlockSpec((1, t, 2 * RET_DV), pair3),
        out_shape=jax.ShapeDtypeStruct((b, t, RET_V_W), BF16),
        scratch_shapes=[pltpu.VMEM((2, n_chunks, 2 * RET_DK, RET_DV), F32),
                        pltpu.VMEM((2, n_chunks, RET_CHUNK + 2 * RET_DK, RET_DV), BF16)],
        compiler_params=_cparams("parallel", "parallel"),
        name="retention",
    )(lgf, lgb, rq, rkt, rv, rg, rktc, rvc, gng, gnb)


def _attention_kernel(q_ref, kt_ref, v_ref, o_ref, s_buf, p_buf):
    tq = q_ref.shape[1]
    lane = lax.broadcasted_iota(jnp.int32, (tq, LANES), 1)

    def scores(h):
        kvh = h // ATT_GROUP
        s_buf[h % 2] = jnp.dot(q_ref[0, :, h * HEAD_DIM:(h + 1) * HEAD_DIM],
                               kt_ref[0, kvh * HEAD_DIM:(kvh + 1) * HEAD_DIM, :], preferred_element_type=F32)

    def numerators(h):
        s = s_buf[h % 2]
        p_buf[h % 2] = jnp.exp2(s - jnp.max(s, -1, keepdims=True)).astype(BF16)

    def values(h):
        ov = jnp.dot(p_buf[h % 2], v_ref[0], preferred_element_type=F32)
        o = ov[:, :LANES] / ov[:, LANES:LANES + 1]
        if h // ATT_GROUP != h % 2:
            o = pltpu.roll(o, HEAD_DIM, 1)
        return o

    scores(0)
    scores(1)
    numerators(0)
    even = None
    for h in range(ATT_HEADS):
        if h + 2 < ATT_HEADS:
            scores(h + 2)
        if h + 1 < ATT_HEADS:
            numerators(h + 1)
        o = values(h)
        if h % 2 == 0:
            even = o
        else:
            col = (h // 2) * LANES
            o_ref[0, :, col:col + LANES] = jnp.where(lane < HEAD_DIM, even, o).astype(o_ref.dtype)


def _attention(aq, akt, av_ones):
    b, t, _ = aq.shape
    tk = akt.shape[2]
    tq = ATT_Q_TILE
    return pl.pallas_call(
        _attention_kernel,
        grid=(b, t // tq),
        in_specs=[pl.BlockSpec((1, tq, ATT_Q_W), lambda bi, i: (bi, i, 0)),
                  pl.BlockSpec((1, ATT_K_W, tk), lambda bi, i: (bi, 0, 0)),
                  pl.BlockSpec((1, tk, av_ones.shape[2]), lambda bi, i: (bi, 0, 0))],
        out_specs=pl.BlockSpec((1, tq, ATT_Q_W), lambda bi, i: (bi, i, 0)),
        out_shape=jax.ShapeDtypeStruct((b, t, ATT_Q_W), BF16),
        scratch_shapes=[pltpu.VMEM((2, tq, tk), F32), pltpu.VMEM((2, tq, tk), BF16)],
        compiler_params=_cparams("parallel", "parallel"),
        name="attention",
    )(aq, akt, av_ones)


def _post_mixer_kernel(ret_ref, att_ref, x_ref, mod_ref, lng_ref, lnb_ref, wo_ref, l1g_ref, l1b_ref,
                       rwh_ref, rwl_ref, rb_ref, before_ref, below_ref,
                       x1_ref, h2_ref, gate_ref, lrow_ref, cnt_ref, lstart_ref, tbase_ref, base_acc):
    first = (pl.program_id(0) == 0) & (pl.program_id(1) == 0)

    @pl.when(first)
    def _():
        base_acc[...] = jnp.zeros_like(base_acc)

    half = ret_ref.shape[2]
    y = (jnp.dot(ret_ref[0], wo_ref[0:half, :], preferred_element_type=F32)
         + jnp.dot(att_ref[0], wo_ref[half:, :], preferred_element_type=F32))
    xn = _layer_norm(x_ref[0], lng_ref[...], lnb_ref[...])
    x1 = _layer_norm(DN_ALPHA * xn + mod_ref[0, 2:3, :] * y, l1g_ref[...], l1b_ref[...])
    x1_ref[...] = x1
    h2 = x1 * (1.0 + mod_ref[0, 4:5, :]) + mod_ref[0, 3:4, :]
    h_hi = h2.astype(BF16)
    h2_ref[...] = h_hi

    h_lo = (h2 - h_hi.astype(F32)).astype(BF16)
    logits = (jnp.dot(h_hi, rwh_ref[...], preferred_element_type=F32)
              + jnp.dot(h_lo, rwh_ref[...], preferred_element_type=F32)
              + jnp.dot(h_hi, rwl_ref[...], preferred_element_type=F32)) + rb_ref[...]

    tt = logits.shape[0]
    work = logits.T[0:N_EXPERTS, :]
    e_iota = lax.broadcasted_iota(jnp.int32, (N_EXPERTS, tt), 0)
    sels, vals = [], []
    onehot = jnp.zeros((N_EXPERTS, tt), F32)
    for _ in range(TOP_K):
        m = jnp.max(work, 0, keepdims=True)
        idx = jnp.min(jnp.where(work == m, e_iota, N_EXPERTS), 0, keepdims=True)
        sel = e_iota == idx
        sels.append(sel)
        vals.append(m)
        onehot = onehot + sel.astype(F32)
        work = jnp.where(sel, -jnp.inf, work)
    exps = [jnp.exp(v - vals[0]) for v in vals]
    denom = exps[0] + exps[1] + exps[2] + exps[3]

    count = jnp.sum(onehot, 1, keepdims=True)
    units = jnp.floor((count + (SEG_ALIGN - 1.0)) * (1.0 / SEG_ALIGN))
    lstart = SEG_ALIGN * jnp.dot(below_ref[...], jnp.broadcast_to(units, (N_EXPERTS, LANES)).astype(BF16),
                                 preferred_element_type=F32)[:, 0:1]
    rank = jnp.dot(onehot.astype(BF16), before_ref[...], preferred_element_type=F32) + lstart

    k_iota = lax.broadcasted_iota(jnp.int32, (TOP_K, tt), 0)
    gate_out = jnp.zeros((TOP_K, tt), F32)
    lrow_out = jnp.zeros((TOP_K, tt), F32)
    for k in range(TOP_K):
        pk = jnp.sum(jnp.where(sels[k], rank, 0.0), 0, keepdims=True)
        gate_out = jnp.where(k_iota == k, exps[k] / denom, gate_out)
        lrow_out = jnp.where(k_iota == k, pk, lrow_out)
    gate_ref[0] = gate_out
    lrow_ref[0] = lrow_out.astype(jnp.int32)
    cnt_ref[0] = (units * SEG_ALIGN).astype(jnp.int32)
    lstart_ref[0] = lstart.astype(jnp.int32)
    tbase_ref[0] = base_acc[...].astype(jnp.int32)
    base_acc[...] += units * SEG_ALIGN


def _post_mixer(ret, att, x, mod, lng, lnb, wo_bf, l1g, l1b, rw_hi, rw_lo, rb, before, below):
    b, t, d = x.shape
    tt = TOKEN_TILE
    nt = t // tt
    n = b * t
    const2 = lambda bi, i: (0, 0)
    tok3 = lambda bi, i: (bi, i, 0)
    flat = lambda bi, i: (bi * nt + i, 0)
    tile3 = lambda bi, i: (bi * nt + i, 0, 0)
    vec = pl.BlockSpec((1, d), const2)
    table = jax.ShapeDtypeStruct((b * nt, N_EXPERTS, 1), jnp.int32)
    table_spec = pl.BlockSpec((1, N_EXPERTS, 1), tile3)
    per_k = pl.BlockSpec((1, TOP_K, tt), tile3)
    return pl.pallas_call(
        _post_mixer_kernel,
        grid=(b, nt),
        in_specs=[pl.BlockSpec((1, tt, ret.shape[2]), tok3),
                  pl.BlockSpec((1, tt, att.shape[2]), tok3),
                  pl.BlockSpec((1, tt, d), tok3),
                  pl.BlockSpec((1, 6, d), lambda bi, i: (bi, 0, 0)),
                  vec, vec,
                  pl.BlockSpec(wo_bf.shape, const2),
                  vec, vec,
                  pl.BlockSpec(rw_hi.shape, const2),
                  pl.BlockSpec(rw_lo.shape, const2),
                  pl.BlockSpec(rb.shape, const2),
                  pl.BlockSpec((tt, tt), const2),
                  pl.BlockSpec((N_EXPERTS, N_EXPERTS), const2)],
        out_specs=[pl.BlockSpec((tt, d), flat),
                   pl.BlockSpec((tt, d), flat),
                   per_k, per_k, table_spec, table_spec, table_spec],
        out_shape=[jax.ShapeDtypeStruct((n, d), F32),
                   jax.ShapeDtypeStruct((n, d), BF16),
                   jax.ShapeDtypeStruct((b * nt, TOP_K, tt), F32),
                   jax.ShapeDtypeStruct((b * nt, TOP_K, tt), jnp.int32),
                   table, table, table],
        scratch_shapes=[pltpu.VMEM((N_EXPERTS, 1), F32)],
        compiler_params=_cparams("arbitrary", "arbitrary"),
        name="post_mixer",
    )(ret, att, x, mod, lng, lnb, wo_bf, l1g, l1b, rw_hi, rw_lo, rb, before, below)


def _segment_starts(cnt_ref, lstart_ref, dst_ref, tile, make_copy):
    def body(e, carry):
        t = tile * N_EXPERTS + e
        n = cnt_ref[t]

        @pl.when(n > 0)
        def _():
            make_copy(lstart_ref[t], dst_ref[t], n).start()
        return carry
    lax.fori_loop(0, N_EXPERTS, body, 0, unroll=4)


def _segment_waits(total_ref, tile, make_copy):
    make_copy(0, 0, total_ref[tile]).wait()


def _zero_fill_copies(tail_len_ref, tail_dst_ref, nact_ref, zbuf, x_hbm, zsem, act):
    block = zbuf.shape[0]

    def tail_body(e, carry):
        n = tail_len_ref[e]

        @pl.when(n > 0)
        def _():
            copy = pltpu.make_async_copy(zbuf.at[pl.ds(0, n)], x_hbm.at[pl.ds(tail_dst_ref[e], n)], zsem)
            getattr(copy, act)()
        return carry
    lax.fori_loop(0, N_EXPERTS, tail_body, 0)

    def block_body(j, carry):
        copy = pltpu.make_async_copy(zbuf, x_hbm.at[pl.ds(j * block, block)], zsem)
        getattr(copy, act)()
        return carry
    lax.fori_loop(nact_ref[0], x_hbm.shape[0] // block, block_body, 0)


def _dispatch_kernel(cnt_ref, lstart_ref, dst_ref, total_ref, tail_len_ref, tail_dst_ref, nact_ref,
                     h_ref, lrow_ref, x_hbm, sbuf, sem, zbuf, zsem):
    i = pl.program_id(0)
    last = pl.num_programs(0) - 1
    slot = i % 2
    rows = sbuf.shape[1] * SEG_ALIGN
    tt = h_ref.shape[0]
    d = h_ref.shape[1]
    chunk_groups = PERM_CHUNK // SEG_ALIGN

    def make_copy(tile_slot):
        def mk(ls, ds, size):
            return pltpu.make_async_copy(sbuf.at[tile_slot, pl.ds(ls, size)], x_hbm.at[pl.ds(ds, size)],
                                         sem.at[tile_slot])
        return mk

    @pl.when(i == 0)
    def _():
        zbuf[...] = jnp.zeros_like(zbuf)
        _zero_fill_copies(tail_len_ref, tail_dst_ref, nact_ref, zbuf, x_hbm, zsem, "start")

    def perm(j):
        r_iota = lax.broadcasted_iota(jnp.int32, (PERM_CHUNK, tt), 0) + j * PERM_CHUNK
        hit = r_iota == lrow_ref[0, 0:1, :]
        for k in range(1, TOP_K):
            hit = hit | (r_iota == lrow_ref[0, k:k + 1, :])
        return jnp.where(hit, 1.0, 0.0).astype(BF16)

    p_next = perm(0)
    for j in range(rows // PERM_CHUNK):
        p_cur = p_next
        if (j + 1) * PERM_CHUNK < rows:
            p_next = perm(j + 1)
        sbuf[slot, j * chunk_groups:(j + 1) * chunk_groups] = jnp.dot(
            p_cur, h_ref[...], preferred_element_type=F32).astype(BF16).reshape(chunk_groups, SEG_ALIGN, d)
    _segment_starts(cnt_ref, lstart_ref, dst_ref, i, make_copy(slot))

    @pl.when(i > 0)
    def _():
        _segment_waits(total_ref, i - 1, make_copy(1 - slot))

    @pl.when(i == last)
    def _():
        _segment_waits(total_ref, i, make_copy(slot))
        _zero_fill_copies(tail_len_ref, tail_dst_ref, nact_ref, zbuf, x_hbm, zsem, "wait")


def _dispatch(cnt, lstart, dst, total, tail_len, tail_dst, n_active, h2, lrow_t, n_rows):
    n, d = h2.shape
    tt = TOKEN_TILE
    grid_spec = pltpu.PrefetchScalarGridSpec(
        num_scalar_prefetch=7,
        grid=(n // tt,),
        in_specs=[pl.BlockSpec((tt, d), lambda i, *_: (i, 0)),
                  pl.BlockSpec((1, TOP_K, tt), lambda i, *_: (i, 0, 0))],
        out_specs=pl.BlockSpec(memory_space=pl.ANY),
        scratch_shapes=[pltpu.VMEM((2, TILE_SORTED_ROWS // SEG_ALIGN, SEG_ALIGN, d), BF16),
                        pltpu.SemaphoreType.DMA((2,)),
                        pltpu.VMEM((EXPERT_ROWS // SEG_ALIGN, SEG_ALIGN, d), BF16),
                        pltpu.SemaphoreType.DMA(())])
    return pl.pallas_call(
        _dispatch_kernel,
        grid_spec=grid_spec,
        out_shape=jax.ShapeDtypeStruct((n_rows // SEG_ALIGN, SEG_ALIGN, d), BF16),
        compiler_params=_cparams("arbitrary"),
        name="dispatch",
    )(cnt, lstart, dst, total, tail_len, tail_dst, n_active, h2, lrow_t)


def _experts_kernel(be_ref, nact_ref, x_ref, wup_ref, bup_ref, wdn_ref, bdn_ref, y_ref,
                    wup_bf, wdn_bf):
    j = pl.program_id(0)
    dff = wdn_ref.shape[1]

    @pl.when(j < nact_ref[0])
    def _():
        prev = be_ref[jnp.maximum(j - 1, 0)]

        @pl.when((j == 0) | (be_ref[j] != prev))
        def _():
            wup_bf[...] = wup_ref[0].astype(BF16)
            wdn_bf[...] = wdn_ref[0].astype(BF16)

        x = x_ref[...].reshape(-1, x_ref.shape[2])
        u = jnp.dot(x, wup_bf[...], preferred_element_type=F32) + bup_ref[0]
        glu = jnp.minimum(u[:, :dff], SWIGLU_LIMIT)
        lin = jnp.clip(u[:, dff:], -SWIGLU_LIMIT, SWIGLU_LIMIT)
        act = glu * (1.0 / (1.0 + jnp.exp(-SWIGLU_ALPHA * glu))) * (lin + 1.0)
        y = jnp.dot(act.astype(BF16), wdn_bf[...], preferred_element_type=F32) + bdn_ref[0]
        y_ref[...] = y.astype(y_ref.dtype).reshape(y_ref.shape)

    @pl.when(j >= nact_ref[0])
    def _():
        y_ref[...] = jnp.zeros_like(y_ref)


def _experts(block_expert, n_active, x_sorted, w_up, b_up, w_down, b_down):
    n_blocks = block_expert.shape[0]
    tm = EXPERT_ROWS
    d = x_sorted.shape[2]
    e, _, up_w = w_up.shape
    dff = w_down.shape[1]
    by_expert = lambda j, be, na: (be[j], 0, 0)
    x_map = lambda j, be, na: (jnp.minimum(j, na[0] - 1), 0, 0)
    rows3 = (tm // SEG_ALIGN, SEG_ALIGN, d)
    grid_spec = pltpu.PrefetchScalarGridSpec(
        num_scalar_prefetch=2,
        grid=(n_blocks,),
        in_specs=[pl.BlockSpec(rows3, x_map),
                  pl.BlockSpec((1, d, up_w), by_expert),
                  pl.BlockSpec((1, 1, up_w), by_expert),
                  pl.BlockSpec((1, dff, d), by_expert),
                  pl.BlockSpec((1, 1, d), by_expert)],
        out_specs=pl.BlockSpec(rows3, lambda j, be, na: (j, 0, 0)),
        scratch_shapes=[pltpu.VMEM((d, up_w), BF16),
                        pltpu.VMEM((dff, d), BF16)])
    return pl.pallas_call(
        _experts_kernel,
        grid_spec=grid_spec,
        out_shape=jax.ShapeDtypeStruct(x_sorted.shape, BF16),
        compiler_params=_cparams("arbitrary"),
        name="experts",
    )(block_expert, n_active, x_sorted, w_up, b_up.reshape(e, 1, up_w), w_down,
      b_down.reshape(e, 1, d))


def _combine_kernel(cnt_ref, lstart_ref, dst_ref, total_ref, lrow_ref, gate_ref, x1_ref, mod_ref, g_ref, b_ref,
                    y_hbm, o_ref, ybuf, sem):
    i = pl.program_id(0)
    last = pl.num_programs(0) - 1
    slot = i % 2
    rows = ybuf.shape[1] * SEG_ALIGN
    tt = x1_ref.shape[0]
    d = x1_ref.shape[1]
    chunk_groups = PERM_CHUNK // SEG_ALIGN

    def make_copy(tile_slot):
        def mk(ls, ds, size):
            return pltpu.make_async_copy(y_hbm.at[pl.ds(ds, size)], ybuf.at[tile_slot, pl.ds(ls, size)],
                                         sem.at[tile_slot])
        return mk

    @pl.when(i == 0)
    def _():
        ybuf[...] = jnp.zeros_like(ybuf)
        _segment_starts(cnt_ref, lstart_ref, dst_ref, i, make_copy(slot))

    @pl.when(i < last)
    def _():
        _segment_starts(cnt_ref, lstart_ref, dst_ref, i + 1, make_copy(1 - slot))

    def weights(j):
        c_iota = lax.broadcasted_iota(jnp.int32, (tt, PERM_CHUNK), 1) + j * PERM_CHUNK
        w = jnp.zeros((tt, PERM_CHUNK), F32)
        for k in range(TOP_K):
            w = w + jnp.where(c_iota == lrow_ref[:, k:k + 1], gate_ref[:, k:k + 1], 0.0)
        return w.astype(BF16)

    w_next = weights(0)
    _segment_waits(total_ref, i, make_copy(slot))
    f = None
    for j in range(rows // PERM_CHUNK):
        w_cur = w_next
        if (j + 1) * PERM_CHUNK < rows:
            w_next = weights(j + 1)
        y_chunk = ybuf[slot, j * chunk_groups:(j + 1) * chunk_groups].reshape(PERM_CHUNK, d)
        part = jnp.dot(w_cur, y_chunk, preferred_element_type=F32)
        f = part if f is None else f + part
    o_ref[...] = _layer_norm(DN_ALPHA * x1_ref[...] + mod_ref[0, 5:6, :] * f, g_ref[...], b_ref[...])


def _combine(cnt, lstart, dst, total, lrow, gates, x1, mod, g, b, y_sorted, tokens_per_sample):
    n, d = x1.shape
    tt = TOKEN_TILE
    tiles_per_sample = tokens_per_sample // tt
    const2 = lambda i, *_: (0, 0)
    tok = lambda i, *_: (i, 0)
    grid_spec = pltpu.PrefetchScalarGridSpec(
        num_scalar_prefetch=4,
        grid=(n // tt,),
        in_specs=[pl.BlockSpec((tt, TOP_K), tok),
                  pl.BlockSpec((tt, TOP_K), tok),
                  pl.BlockSpec((tt, d), tok),
                  pl.BlockSpec((1, 6, d), lambda i, *_: (i // tiles_per_sample, 0, 0)),
                  pl.BlockSpec((1, d), const2),
                  pl.BlockSpec((1, d), const2),
                  pl.BlockSpec(memory_space=pl.ANY)],
        out_specs=pl.BlockSpec((tt, d), tok),
        scratch_shapes=[pltpu.VMEM((2, TILE_SORTED_ROWS // SEG_ALIGN, SEG_ALIGN, d), BF16),
                        pltpu.SemaphoreType.DMA((2,))])
    return pl.pallas_call(
        _combine_kernel,
        grid_spec=grid_spec,
        out_shape=jax.ShapeDtypeStruct((n, d), F32),
        compiler_params=_cparams("arbitrary"),
        name="combine",
    )(cnt, lstart, dst, total, lrow, gates, x1, mod, g, b, y_sorted)


def _rope_tables(t):
    rows = jnp.repeat(jnp.arange(t // GRID_W, dtype=jnp.int32), GRID_W).astype(F32)
    cols = jnp.tile(jnp.arange(GRID_W, dtype=jnp.int32), t // GRID_W).astype(F32)
    n_freq = HEAD_DIM // 4
    inv = ROPE_BASE ** (-jnp.arange(n_freq, dtype=F32) / n_freq)
    ang_r = rows[:, None] * inv
    ang_c = cols[:, None] * inv
    ang = jnp.concatenate([ang_r, ang_r, ang_c, ang_c], -1)
    ang = jnp.concatenate([ang, ang], -1)
    cos, sin = jnp.cos(ang), jnp.sin(ang)
    first_half = (jnp.arange(LANES) % 32) < 16
    return cos, jnp.where(first_half, -sin, 0.0), jnp.where(first_half, 0.0, sin)


def kernel(x, c, ctx, c_ctx, ln_in_g, ln_in_b, w_ada, b_ada, w_in, ret_log_decay_f, ret_log_decay_b,
           ret_gn_g, ret_gn_b, q_norm_g, k_norm_g, w_o, ln1_g, ln1_b, router_w, router_b,
           w_up, b_up, w_down, b_down, ln2_g, ln2_b):
    b, t, d = x.shape
    n = b * t
    row = lambda v: v.reshape(1, -1)

    cc = jnp.concatenate([c, c_ctx[None]], 0)
    cc = jnp.pad(cc, ((0, (-cc.shape[0]) % 8), (0, 0)))
    mod_all = _ada_mod(cc, w_ada[0], row(b_ada[0]))
    mod = mod_all[:b].reshape(b, 6, d)
    mod_ctx = mod_all[b:b + 1].reshape(1, 6, d)

    w_in_bf = w_in[0].astype(BF16)
    lane = jnp.arange(LANES)
    seg = (lane[:, None] // HEAD_DIM == lane[None, :] // HEAD_DIM).astype(BF16)
    qg = jnp.tile(q_norm_g[0], ATT_HEADS).reshape(1, -1)
    kg = jnp.tile(k_norm_g[0], ATT_KV_HEADS).reshape(1, -1)
    lng, lnb = row(ln_in_g), row(ln_in_b)

    rq, rkt, rv, rg, aq, akt, av = _in_proj(x, mod, lng, lnb, w_in_bf, seg, qg, kg, _rope_tables(t))
    _, rktc, rvc, _, _, aktc, avc = _in_proj(ctx, mod_ctx, lng, lnb, w_in_bf, seg, qg, kg, None)

    ret = _retention(row(ret_log_decay_f[0]), row(ret_log_decay_b[0]), rq, rkt, rv, rg, rktc, rvc,
                     row(ret_gn_g[0]), row(ret_gn_b[0]))
    tk = ctx.shape[1] + t
    av_ones = jnp.concatenate([jnp.concatenate([avc, av], 1), jnp.ones((b, tk, LANES), BF16)], 2)
    att = _attention(aq, jnp.concatenate([aktc, akt], 2), av_ones)

    rw = jnp.pad(router_w[0], ((0, 0), (0, LANES - N_EXPERTS)))
    rb = jnp.pad(row(router_b[0]), ((0, 0), (0, LANES - N_EXPERTS)))
    rw_hi = rw.astype(BF16)
    rw_lo = (rw - rw_hi.astype(F32)).astype(BF16)
    tt = TOKEN_TILE
    n_tiles = n // tt
    before = jnp.triu(jnp.ones((tt, tt), BF16), 1)
    below = jnp.tril(jnp.ones((N_EXPERTS, N_EXPERTS), BF16), -1)
    x1, h2, gates_t, lrow_t, cnt, lstart, tbase = _post_mixer(
        ret, att, x, mod, lng, lnb, w_o[0].astype(BF16), row(ln1_g[0]), row(ln1_b[0]),
        rw_hi, rw_lo, rb, before, below)
    token_major = lambda v: v.transpose(0, 2, 1).reshape(n, TOP_K)
    gates, lrow = token_major(gates_t), token_major(lrow_t)

    tm = EXPERT_ROWS
    cnt, lstart, tbase = cnt[:, :, 0], lstart[:, :, 0], tbase[:, :, 0]
    totals = tbase[-1] + cnt[-1]
    exp_rows = (totals + tm - 1) // tm * tm
    exp_end = jnp.cumsum(exp_rows)
    exp_start = exp_end - exp_rows
    dst = (exp_start[None, :] + tbase).reshape(-1)
    max_rows = n * TOP_K + n_tiles * N_EXPERTS * (SEG_ALIGN - 1) + N_EXPERTS * (tm - 1)
    n_blocks = -(-max_rows // tm)
    block_start = jnp.arange(n_blocks, dtype=jnp.int32) * tm
    block_expert = jnp.minimum(jnp.sum(exp_end[None, :] <= block_start[:, None], axis=1),
                               N_EXPERTS - 1).astype(jnp.int32)
    n_active = (exp_end[-1:] // tm).astype(jnp.int32)
    groups = lambda v: (v // SEG_ALIGN).astype(jnp.int32).reshape(-1)
    cnt_g, lstart_g, dst_g, total_g = groups(cnt), groups(lstart), groups(dst), groups(jnp.sum(cnt, axis=1))

    x_sorted = _dispatch(cnt_g, lstart_g, dst_g, total_g, groups(exp_rows - totals), groups(exp_start + totals),
                         n_active, h2, lrow_t, n_blocks * tm)
    y_sorted = _experts(block_expert, n_active, x_sorted, w_up[0], b_up[0], w_down[0], b_down[0])
    out = _combine(cnt_g, lstart_g, dst_g, total_g, lrow, gates, x1, mod, row(ln2_g[0]), row(ln2_b[0]),
                   y_sorted, t)
    return out.reshape(b, t, d)
```

```python
import functools

import jax
import jax.numpy as jnp
from jax import lax
from jax.experimental import pallas as pl
from jax.experimental.pallas import tpu as pltpu

GRID_W = 64
HEAD_DIM = 64
ROPE_BASE = 10000.0
LN_EPS = 1e-6
RET_HEADS = 4
RET_DK = 64
RET_DV = 128
RET_CHUNK = 128
ATT_HEADS = 8
ATT_KV_HEADS = 2
ATT_GROUP = ATT_HEADS // ATT_KV_HEADS
N_EXPERTS = 32
TOP_K = 4
SWIGLU_LIMIT = 7.0
SWIGLU_ALPHA = 1.702
DEPTH = 1
DN_ALPHA = (2.0 * DEPTH) ** 0.25
LOG2_E = 1.4426950408889634

RET_Q_W = RET_HEADS * RET_DK
RET_V_W = RET_HEADS * RET_DV
ATT_Q_W = ATT_HEADS * HEAD_DIM
ATT_K_W = ATT_KV_HEADS * HEAD_DIM

LANES = 128
VMEM_LIMIT_BYTES = 56 * 1024 * 1024

TOKEN_TILE = 512
ATT_Q_TILE = 512
ATT_Q_SUB = 256
ATT_P_SHIFT = 8.0
EXPERT_ROWS = 512
SEG_ALIGN = 16
TILE_SORTED_ROWS = 2560
PERM_CHUNK = 512

F32 = jnp.float32
BF16 = jnp.bfloat16


def _cparams(*sem):
    return pltpu.CompilerParams(dimension_semantics=sem, vmem_limit_bytes=VMEM_LIMIT_BYTES)


def _layer_norm(x, g, b):
    mu = jnp.mean(x, -1, keepdims=True)
    xc = x - mu
    var = jnp.mean(xc * xc, -1, keepdims=True)
    return xc * lax.rsqrt(var + LN_EPS) * g + b


def _silu(x):
    return x * (1.0 / (1.0 + jnp.exp(-x)))


def _ada_kernel(c_ref, w_ref, b_ref, o_ref):
    a = _silu(c_ref[...])
    o_ref[...] = jnp.dot(a, w_ref[...], preferred_element_type=F32,
                         precision=lax.Precision.HIGHEST) + b_ref[...]


def _ada_mod(cc, w, b):
    m, d = cc.shape
    n = w.shape[1]
    tn = 1536
    return pl.pallas_call(
        _ada_kernel,
        grid=(n // tn,),
        in_specs=[pl.BlockSpec((m, d), lambda j: (0, 0)),
                  pl.BlockSpec((d, tn), lambda j: (0, j)),
                  pl.BlockSpec((1, tn), lambda j: (0, j))],
        out_specs=pl.BlockSpec((m, tn), lambda j: (0, j)),
        out_shape=jax.ShapeDtypeStruct((m, n), F32),
        compiler_params=_cparams("arbitrary"),
        name="ada_mod",
    )(cc, w, b)


def _rope(x, cos, sinn, sinp):
    outs = []
    for j in range(x.shape[1] // LANES):
        xg = x[:, j * LANES:(j + 1) * LANES]
        outs.append(xg * cos + pltpu.roll(xg, LANES - 16, 1) * sinn + pltpu.roll(xg, 16, 1) * sinp)
    return outs[0] if len(outs) == 1 else jnp.concatenate(outs, axis=1)


def _head_rms(x, seg, gain):
    outs = []
    for j in range(x.shape[1] // LANES):
        xg = x[:, j * LANES:(j + 1) * LANES]
        sq = xg * xg
        hi = sq.astype(BF16)
        lo = (sq - hi.astype(F32)).astype(BF16)
        ms = (jnp.dot(hi, seg, preferred_element_type=F32)
              + jnp.dot(lo, seg, preferred_element_type=F32)) * (1.0 / HEAD_DIM)
        outs.append(xg * lax.rsqrt(ms + LN_EPS) * gain[:, j * LANES:(j + 1) * LANES])
    return outs[0] if len(outs) == 1 else jnp.concatenate(outs, axis=1)


def _in_proj_kernel(*refs, rope):
    if rope:
        (x_ref, mod_ref, lng_ref, lnb_ref, w_ref, seg_ref, qg_ref, kg_ref, cos_ref, sinn_ref, sinp_ref,
         rq_ref, rkt_ref, rv_ref, rg_ref, aq_ref, akt_ref, av_ref) = refs
    else:
        (x_ref, mod_ref, lng_ref, lnb_ref, w_ref, seg_ref, qg_ref, kg_ref,
         rq_ref, rkt_ref, rv_ref, rg_ref, aq_ref, akt_ref, av_ref) = refs
    x = x_ref[0]
    xn = _layer_norm(x, lng_ref[...], lnb_ref[...])
    h = (xn * (1.0 + mod_ref[0, 1:2, :]) + mod_ref[0, 0:1, :]).astype(BF16)
    seg = seg_ref[...]

    def rotate(v):
        return _rope(v, cos_ref[...], sinn_ref[...], sinp_ref[...]) if rope else v

    o_rv = 2 * RET_Q_W
    o_aq = o_rv + 2 * RET_V_W
    o_ak = o_aq + ATT_Q_W
    o_end = o_ak + 2 * ATT_K_W

    def project(lo, hi):
        return jnp.dot(h, w_ref[:, lo:hi], preferred_element_type=F32)

    def finish_aq(u):
        aq = rotate(_head_rms(u, seg, qg_ref[...]))
        aq_ref[0] = (aq * (HEAD_DIM ** -0.5 * LOG2_E)).astype(BF16)

    def finish_akv(u):
        ak = rotate(_head_rms(u[:, :ATT_K_W], seg, kg_ref[...]))
        akt_ref[0] = ak.T.astype(BF16)
        av_ref[0] = u[:, ATT_K_W:].astype(BF16)

    def finish_rqk(u):
        rq_ref[0] = rotate(u[:, :RET_Q_W])
        rkt_ref[0] = (rotate(u[:, RET_Q_W:]) * (RET_DK ** -0.5)).T

    def finish_rvg(u):
        rv_ref[0] = u[:, :RET_V_W].astype(BF16)
        rg_ref[0] = u[:, RET_V_W:].astype(BF16)

    stages = [((o_aq, o_ak), finish_aq), ((o_ak, o_end), finish_akv),
              ((0, o_rv), finish_rqk), ((o_rv, o_aq), finish_rvg)]
    u_next = project(*stages[0][0])
    for idx, (_, finish) in enumerate(stages):
        u_cur = u_next
        if idx + 1 < len(stages):
            u_next = project(*stages[idx + 1][0])
        finish(u_cur)


def _in_proj(x, mod, lng, lnb, w_bf, seg, qg, kg, tables):
    b, t, d = x.shape
    tt = min(TOKEN_TILE, t)
    rope = tables is not None
    mod_b = mod.shape[0]
    mod_map = (lambda bi, i: (bi, 0, 0)) if mod_b > 1 else (lambda bi, i: (0, 0, 0))
    const2 = lambda bi, i: (0, 0)
    tok3 = lambda bi, i: (bi, i, 0)
    in_specs = [pl.BlockSpec((1, tt, d), tok3),
                pl.BlockSpec((1, 6, d), mod_map),
                pl.BlockSpec((1, d), const2),
                pl.BlockSpec((1, d), const2),
                pl.BlockSpec(w_bf.shape, const2),
                pl.BlockSpec(seg.shape, const2),
                pl.BlockSpec(qg.shape, const2),
                pl.BlockSpec(kg.shape, const2)]
    args = [x, mod, lng, lnb, w_bf, seg, qg, kg]
    if rope:
        in_specs += [pl.BlockSpec((tt, LANES), lambda bi, i: (i, 0))] * 3
        args += list(tables)
    out_shape = [jax.ShapeDtypeStruct((b, t, RET_Q_W), F32),
                 jax.ShapeDtypeStruct((b, RET_Q_W, t), F32),
                 jax.ShapeDtypeStruct((b, t, RET_V_W), BF16),
                 jax.ShapeDtypeStruct((b, t, RET_V_W), BF16),
                 jax.ShapeDtypeStruct((b, t, ATT_Q_W), BF16),
                 jax.ShapeDtypeStruct((b, ATT_K_W, t), BF16),
                 jax.ShapeDtypeStruct((b, t, ATT_K_W), BF16)]
    out_specs = [pl.BlockSpec((1, tt, RET_Q_W), tok3),
                 pl.BlockSpec((1, RET_Q_W, tt), lambda bi, i: (bi, 0, i)),
                 pl.BlockSpec((1, tt, RET_V_W), tok3),
                 pl.BlockSpec((1, tt, RET_V_W), tok3),
                 pl.BlockSpec((1, tt, ATT_Q_W), tok3),
                 pl.BlockSpec((1, ATT_K_W, tt), lambda bi, i: (bi, 0, i)),
                 pl.BlockSpec((1, tt, ATT_K_W), tok3)]
    return pl.pallas_call(
        functools.partial(_in_proj_kernel, rope=rope),
        grid=(b, t // tt),
        in_specs=in_specs,
        out_specs=out_specs,
        out_shape=out_shape,
        compiler_params=_cparams("parallel", "parallel"),
        name="in_proj_rope" if rope else "in_proj_ctx",
    )(*args)


def _retention_kernel(lgf_ref, lgb_ref, q_ref, kt_ref, v_ref, g_ref, ktc_ref, vc_ref, gng_ref, gnb_ref,
                      o_ref, kv_ref, rhs_ref):
    pair = pl.program_id(1)
    t = q_ref.shape[1]
    n_chunks = t // RET_CHUNK
    ctx_len = ktc_ref.shape[2]
    c = RET_CHUNK
    dk = RET_DK
    pos_col = lax.broadcasted_iota(jnp.int32, (c, 1), 0).astype(F32)
    pos_row = lax.broadcasted_iota(jnp.int32, (1, c), 1).astype(F32)
    row = lax.broadcasted_iota(jnp.int32, (c, c), 0)
    col = lax.broadcasted_iota(jnp.int32, (c, c), 1)
    diff = (row - col).astype(F32)
    lane = lax.broadcasted_iota(jnp.int32, (c, 2 * dk), 1)
    cpos = lax.broadcasted_iota(jnp.int32, (1, ctx_len), 1).astype(F32)

    heads = []
    for hh in range(2):
        head = pair * 2 + hh
        lgf = lgf_ref[0, head]
        lgb = lgb_ref[0, head]
        heads.append(dict(
            kl=slice(hh * dk, (hh + 1) * dk),
            vl=slice(hh * RET_DV, (hh + 1) * RET_DV),
            lgf=lgf, lgb=lgb,
            kdf=jnp.exp(lgf * (c - 1.0 - pos_row)),
            kdb=jnp.exp(lgb * pos_row),
            qd=jnp.where(lane < dk, jnp.exp(lgf * (pos_col + 1.0)), jnp.exp(lgb * (c - pos_col))),
            decay=(jnp.where(diff >= 0, jnp.exp(lgf * jnp.maximum(diff, 0.0)), 0.0)
                   + jnp.where(diff <= 0, jnp.exp(lgb * jnp.maximum(-diff, 0.0)), 0.0))))

    def kv_body(n, carry):
        r = pl.ds(pl.multiple_of(n * c, c), c)
        for hh, hd in enumerate(heads):
            kt = kt_ref[0, hd["kl"], r]
            vc = v_ref[0, r, hd["vl"]]
            both = jnp.concatenate([kt * hd["kdf"], kt * hd["kdb"]], axis=0).astype(BF16)
            kv_ref[hh, n] = jnp.dot(both, vc, preferred_element_type=F32)
            rhs_ref[hh, n, 0:c, :] = vc
        return carry

    lax.fori_loop(0, n_chunks, kv_body, 0, unroll=4)

    for hh, hd in enumerate(heads):
        lgf, lgb = hd["lgf"], hd["lgb"]
        ktc = ktc_ref[0, hd["kl"], :]
        both = jnp.concatenate([ktc * jnp.exp(lgf * (ctx_len - 1.0 - cpos)), ktc * jnp.exp(lgb * cpos)],
                               axis=0).astype(BF16)
        s_ctx = jnp.dot(both, vc_ref[0, :, hd["vl"]], preferred_element_type=F32)
        s_f, s_b = s_ctx[0:dk], s_ctx[dk:]
        gfc = jnp.exp(lgf * c)
        gbc = jnp.exp(lgb * c)
        for n in range(n_chunks):
            rhs_ref[hh, n, c:c + dk, :] = s_f.astype(BF16)
            s_f = gfc * s_f + kv_ref[hh, n, 0:dk, :]
        for n in range(n_chunks - 1, -1, -1):
            rhs_ref[hh, n, c + dk:, :] = s_b.astype(BF16)
            s_b = gbc * s_b + kv_ref[hh, n, dk:, :]

    def out_body(n, carry):
        r = pl.ds(pl.multiple_of(n * c, c), c)
        qq = q_ref[0, r, :]
        swapped = pltpu.roll(qq, dk, 1)
        for hh, hd in enumerate(heads):
            vl = hd["vl"]
            s = jnp.dot(q_ref[0, r, hd["kl"]].astype(BF16), kt_ref[0, hd["kl"], r].astype(BF16),
                        preferred_element_type=F32)
            q2 = jnp.where(lane < dk, qq, swapped) if hh == 0 else jnp.where(lane < dk, swapped, qq)
            lhs = jnp.concatenate([(s * hd["decay"]).astype(BF16), (q2 * hd["qd"]).astype(BF16)], axis=1)
            y = jnp.dot(lhs, rhs_ref[hh, n], preferred_element_type=F32)
            mu = jnp.mean(y, -1, keepdims=True)
            yc = y - mu
            var = jnp.mean(yc * yc, -1, keepdims=True)
            yn = yc * lax.rsqrt(var + LN_EPS) * gng_ref[:, vl] + gnb_ref[:, vl]
            o_ref[0, r, vl] = (yn * _silu(g_ref[0, r, vl].astype(F32))).astype(o_ref.dtype)
        return carry

    lax.fori_loop(0, n_chunks, out_body, 0, unroll=8)


def _retention(lgf, lgb, rq, rkt, rv, rg, rktc, rvc, gng, gnb):
    b, t, _ = rq.shape
    ctx_len = rktc.shape[2]
    n_chunks = t // RET_CHUNK
    pair3 = lambda bi, p: (bi, 0, p)
    pair_rows = lambda bi, p: (bi, p, 0)
    smem = pl.BlockSpec(memory_space=pltpu.SMEM)
    return pl.pallas_call(
        _retention_kernel,
        grid=(b, RET_HEADS // 2),
        in_specs=[smem, smem,
                  pl.BlockSpec((1, t, 2 * RET_DK), pair3),
                  pl.BlockSpec((1, 2 * RET_DK, t), pair_rows),
                  pl.BlockSpec((1, t, 2 * RET_DV), pair3),
                  pl.BlockSpec((1, t, 2 * RET_DV), pair3),
                  pl.BlockSpec((1, 2 * RET_DK, ctx_len), pair_rows),
                  pl.BlockSpec((1, ctx_len, 2 * RET_DV), pair3),
                  pl.BlockSpec((1, 2 * RET_DV), lambda bi, p: (0, p)),
                  pl.BlockSpec((1, 2 * RET_DV), lambda bi, p: (0, p))],
        out_specs=pl.BlockSpec((1, t, 2 * RET_DV), pair3),
        out_shape=jax.ShapeDtypeStruct((b, t, RET_V_W), BF16),
        scratch_shapes=[pltpu.VMEM((2, n_chunks, 2 * RET_DK, RET_DV), F32),
                        pltpu.VMEM((2, n_chunks, RET_CHUNK + 2 * RET_DK, RET_DV), BF16)],
        compiler_params=_cparams("parallel", "parallel"),
        name="retention",
    )(lgf, lgb, rq, rkt, rv, rg, rktc, rvc, gng, gnb)


def _attention_kernel(q_ref, kt_ref, v_ref, o_ref, s_buf, p_buf):
    ts = ATT_Q_SUB
    lane = lax.broadcasted_iota(jnp.int32, (ts, LANES), 1)
    pieces = [(sub, kvh) for sub in range(q_ref.shape[1] // ts) for kvh in range(ATT_KV_HEADS)]

    def scores(i):
        sub, kvh = pieces[i]
        q = jnp.concatenate(
            [q_ref[0, sub * ts:(sub + 1) * ts, (kvh * ATT_GROUP + g) * HEAD_DIM:(kvh * ATT_GROUP + g + 1) * HEAD_DIM]
             for g in range(ATT_GROUP)], axis=0)
        s_buf[i % 2] = jnp.dot(q, kt_ref[0, kvh * HEAD_DIM:(kvh + 1) * HEAD_DIM, :], preferred_element_type=F32)

    def numerators(i):
        s = s_buf[i % 2]
        top = jnp.max(s, -1, keepdims=True) - ATT_P_SHIFT
        p_buf[i % 2] = jnp.exp2(s - top).astype(jnp.float8_e4m3fn)

    def values(i):
        sub, kvh = pieces[i]
        ov = jnp.dot(p_buf[i % 2], v_ref[0], preferred_element_type=F32)
        o = ov[:, :LANES] / ov[:, LANES:LANES + 1]
        for pair in range(ATT_GROUP // 2):
            halves = []
            for half in range(2):
                g = pair * 2 + half
                og = o[g * ts:(g + 1) * ts, :]
                if kvh != half:
                    og = pltpu.roll(og, HEAD_DIM, 1)
                halves.append(og)
            col = (kvh * ATT_GROUP // 2 + pair) * LANES
            o_ref[0, sub * ts:(sub + 1) * ts, col:col + LANES] = jnp.where(
                lane < HEAD_DIM, halves[0], halves[1]).astype(o_ref.dtype)

    scores(0)
    scores(1)
    numerators(0)
    for i in range(len(pieces)):
        if i + 2 < len(pieces):
            scores(i + 2)
        if i + 1 < len(pieces):
            numerators(i + 1)
        values(i)


def _attention(aq, akt, av_ones):
    b, t, _ = aq.shape
    tk = akt.shape[2]
    tq = ATT_Q_TILE
    return pl.pallas_call(
        _attention_kernel,
        grid=(b, t // tq),
        in_specs=[pl.BlockSpec((1, tq, ATT_Q_W), lambda bi, i: (bi, i, 0)),
                  pl.BlockSpec((1, ATT_K_W, tk), lambda bi, i: (bi, 0, 0)),
                  pl.BlockSpec((1, tk, av_ones.shape[2]), lambda bi, i: (bi, 0, 0))],
        out_specs=pl.BlockSpec((1, tq, ATT_Q_W), lambda bi, i: (bi, i, 0)),
        out_shape=jax.ShapeDtypeStruct((b, t, ATT_Q_W), BF16),
        scratch_shapes=[pltpu.VMEM((2, ATT_GROUP * ATT_Q_SUB, tk), F32),
                        pltpu.VMEM((2, ATT_GROUP * ATT_Q_SUB, tk), jnp.float8_e4m3fn)],
        compiler_params=_cparams("parallel", "parallel"),
        name="attention",
    )(aq, akt, av_ones)


def _post_mixer_kernel(ret_ref, att_ref, x_ref, mod_ref, lng_ref, lnb_ref, wo_ref, l1g_ref, l1b_ref,
                       rwh_ref, rwl_ref, rb_ref, before_ref, below_ref,
                       x1_ref, h2_ref, gate_ref, lrow_ref, cnt_ref, lstart_ref, tbase_ref, base_acc):
    first = (pl.program_id(0) == 0) & (pl.program_id(1) == 0)

    @pl.when(first)
    def _():
        base_acc[...] = jnp.zeros_like(base_acc)

    half = ret_ref.shape[2]
    y = (jnp.dot(ret_ref[0], wo_ref[0:half, :], preferred_element_type=F32)
         + jnp.dot(att_ref[0], wo_ref[half:, :], preferred_element_type=F32))
    xn = _layer_norm(x_ref[0], lng_ref[...], lnb_ref[...])
    x1 = _layer_norm(DN_ALPHA * xn + mod_ref[0, 2:3, :] * y, l1g_ref[...], l1b_ref[...])
    x1_ref[...] = x1
    h2 = x1 * (1.0 + mod_ref[0, 4:5, :]) + mod_ref[0, 3:4, :]
    h_hi = h2.astype(BF16)
    h2_ref[...] = h_hi

    h_lo = (h2 - h_hi.astype(F32)).astype(BF16)
    logits = (jnp.dot(h_hi, rwh_ref[...], preferred_element_type=F32)
              + jnp.dot(h_lo, rwh_ref[...], preferred_element_type=F32)
              + jnp.dot(h_hi, rwl_ref[...], preferred_element_type=F32)) + rb_ref[...]

    tt = logits.shape[0]
    work = logits.T[0:N_EXPERTS, :]
    e_iota = lax.broadcasted_iota(jnp.int32, (N_EXPERTS, tt), 0)
    sels, vals = [], []
    onehot = jnp.zeros((N_EXPERTS, tt), F32)
    for _ in range(TOP_K):
        m = jnp.max(work, 0, keepdims=True)
        idx = jnp.min(jnp.where(work == m, e_iota, N_EXPERTS), 0, keepdims=True)
        sel = e_iota == idx
        sels.append(sel)
        vals.append(m)
        onehot = onehot + sel.astype(F32)
        work = jnp.where(sel, -jnp.inf, work)
    exps = [jnp.exp(v - vals[0]) for v in vals]
    denom = exps[0] + exps[1] + exps[2] + exps[3]

    count = jnp.sum(onehot, 1, keepdims=True)
    units = jnp.floor((count + (SEG_ALIGN - 1.0)) * (1.0 / SEG_ALIGN))
    lstart = SEG_ALIGN * jnp.dot(below_ref[...], jnp.broadcast_to(units, (N_EXPERTS, LANES)).astype(BF16),
                                 preferred_element_type=F32)[:, 0:1]
    rank = jnp.dot(onehot.astype(BF16), before_ref[...], preferred_element_type=F32) + lstart

    k_iota = lax.broadcasted_iota(jnp.int32, (TOP_K, tt), 0)
    gate_out = jnp.zeros((TOP_K, tt), F32)
    lrow_out = jnp.zeros((TOP_K, tt), F32)
    for k in range(TOP_K):
        pk = jnp.sum(jnp.where(sels[k], rank, 0.0), 0, keepdims=True)
        gate_out = jnp.where(k_iota == k, exps[k] / denom, gate_out)
        lrow_out = jnp.where(k_iota == k, pk, lrow_out)
    gate_ref[0] = gate_out
    lrow_ref[0] = lrow_out.astype(jnp.int32)
    cnt_ref[0] = (units * SEG_ALIGN).astype(jnp.int32)
    lstart_ref[0] = lstart.astype(jnp.int32)
    tbase_ref[0] = base_acc[...].astype(jnp.int32)
    base_acc[...] += units * SEG_ALIGN


def _post_mixer(ret, att, x, mod, lng, lnb, wo_bf, l1g, l1b, rw_hi, rw_lo, rb, before, below):
    b, t, d = x.shape
    tt = TOKEN_TILE
    nt = t // tt
    n = b * t
    const2 = lambda bi, i: (0, 0)
    tok3 = lambda bi, i: (bi, i, 0)
    flat = lambda bi, i: (bi * nt + i, 0)
    tile3 = lambda bi, i: (bi * nt + i, 0, 0)
    vec = pl.BlockSpec((1, d), const2)
    table = jax.ShapeDtypeStruct((b * nt, N_EXPERTS, 1), jnp.int32)
    table_spec = pl.BlockSpec((1, N_EXPERTS, 1), tile3)
    per_k = pl.BlockSpec((1, TOP_K, tt), tile3)
    return pl.pallas_call(
        _post_mixer_kernel,
        grid=(b, nt),
        in_specs=[pl.BlockSpec((1, tt, ret.shape[2]), tok3),
                  pl.BlockSpec((1, tt, att.shape[2]), tok3),
                  pl.BlockSpec((1, tt, d), tok3),
                  pl.BlockSpec((1, 6, d), lambda bi, i: (bi, 0, 0)),
                  vec, vec,
                  pl.BlockSpec(wo_bf.shape, const2),
                  vec, vec,
                  pl.BlockSpec(rw_hi.shape, const2),
                  pl.BlockSpec(rw_lo.shape, const2),
                  pl.BlockSpec(rb.shape, const2),
                  pl.BlockSpec((tt, tt), const2),
                  pl.BlockSpec((N_EXPERTS, N_EXPERTS), const2)],
        out_specs=[pl.BlockSpec((tt, d), flat),
                   pl.BlockSpec((tt, d), flat),
                   per_k, per_k, table_spec, table_spec, table_spec],
        out_shape=[jax.ShapeDtypeStruct((n, d), F32),
                   jax.ShapeDtypeStruct((n, d), BF16),
                   jax.ShapeDtypeStruct((b * nt, TOP_K, tt), F32),
                   jax.ShapeDtypeStruct((b * nt, TOP_K, tt), jnp.int32),
                   table, table, table],
        scratch_shapes=[pltpu.VMEM((N_EXPERTS, 1), F32)],
        compiler_params=_cparams("arbitrary", "arbitrary"),
        name="post_mixer",
    )(ret, att, x, mod, lng, lnb, wo_bf, l1g, l1b, rw_hi, rw_lo, rb, before, below)


def _segment_starts(cnt_ref, lstart_ref, dst_ref, tile, make_copy):
    def body(e, carry):
        t = tile * N_EXPERTS + e
        n = cnt_ref[t]

        @pl.when(n > 0)
        def _():
            make_copy(lstart_ref[t], dst_ref[t], n).start()
        return carry
    lax.fori_loop(0, N_EXPERTS, body, 0, unroll=4)


def _segment_waits(total_ref, tile, make_copy):
    make_copy(0, 0, total_ref[tile]).wait()


def _zero_fill_copies(tail_len_ref, tail_dst_ref, nact_ref, zbuf, x_hbm, zsem, act):
    block = zbuf.shape[0]

    def tail_body(e, carry):
        n = tail_len_ref[e]

        @pl.when(n > 0)
        def _():
            copy = pltpu.make_async_copy(zbuf.at[pl.ds(0, n)], x_hbm.at[pl.ds(tail_dst_ref[e], n)], zsem)
            getattr(copy, act)()
        return carry
    lax.fori_loop(0, N_EXPERTS, tail_body, 0)

    def block_body(j, carry):
        copy = pltpu.make_async_copy(zbuf, x_hbm.at[pl.ds(j * block, block)], zsem)
        getattr(copy, act)()
        return carry
    lax.fori_loop(nact_ref[0], x_hbm.shape[0] // block, block_body, 0)


def _dispatch_kernel(cnt_ref, lstart_ref, dst_ref, total_ref, tail_len_ref, tail_dst_ref, nact_ref,
                     h_ref, lrow_ref, x_hbm, sbuf, sem, zbuf, zsem):
    i = pl.program_id(0)
    last = pl.num_programs(0) - 1
    slot = i % 2
    rows = sbuf.shape[1] * SEG_ALIGN
    tt = h_ref.shape[0]
    d = h_ref.shape[1]
    chunk_groups = PERM_CHUNK // SEG_ALIGN

    def make_copy(tile_slot):
        def mk(ls, ds, size):
            return pltpu.make_async_copy(sbuf.at[tile_slot, pl.ds(ls, size)], x_hbm.at[pl.ds(ds, size)],
                                         sem.at[tile_slot])
        return mk

    @pl.when(i == 0)
    def _():
        zbuf[...] = jnp.zeros_like(zbuf)
        _zero_fill_copies(tail_len_ref, tail_dst_ref, nact_ref, zbuf, x_hbm, zsem, "start")

    def perm(j):
        r_iota = lax.broadcasted_iota(jnp.int32, (PERM_CHUNK, tt), 0) + j * PERM_CHUNK
        hit = r_iota == lrow_ref[0, 0:1, :]
        for k in range(1, TOP_K):
            hit = hit | (r_iota == lrow_ref[0, k:k + 1, :])
        return jnp.where(hit, 1.0, 0.0).astype(BF16)

    p_next = perm(0)
    for j in range(rows // PERM_CHUNK):
        p_cur = p_next
        if (j + 1) * PERM_CHUNK < rows:
            p_next = perm(j + 1)
        sbuf[slot, j * chunk_groups:(j + 1) * chunk_groups] = jnp.dot(
            p_cur, h_ref[...], preferred_element_type=F32).astype(BF16).reshape(chunk_groups, SEG_ALIGN, d)
    _segment_starts(cnt_ref, lstart_ref, dst_ref, i, make_copy(slot))

    @pl.when(i > 0)
    def _():
        _segment_waits(total_ref, i - 1, make_copy(1 - slot))

    @pl.when(i == last)
    def _():
        _segment_waits(total_ref, i, make_copy(slot))
        _zero_fill_copies(tail_len_ref, tail_dst_ref, nact_ref, zbuf, x_hbm, zsem, "wait")


def _dispatch(cnt, lstart, dst, total, tail_len, tail_dst, n_active, h2, lrow_t, n_rows):
    n, d = h2.shape
    tt = TOKEN_TILE
    grid_spec = pltpu.PrefetchScalarGridSpec(
        num_scalar_prefetch=7,
        grid=(n // tt,),
        in_specs=[pl.BlockSpec((tt, d), lambda i, *_: (i, 0)),
                  pl.BlockSpec((1, TOP_K, tt), lambda i, *_: (i, 0, 0))],
        out_specs=pl.BlockSpec(memory_space=pl.ANY),
        scratch_shapes=[pltpu.VMEM((2, TILE_SORTED_ROWS // SEG_ALIGN, SEG_ALIGN, d), BF16),
                        pltpu.SemaphoreType.DMA((2,)),
                        pltpu.VMEM((EXPERT_ROWS // SEG_ALIGN, SEG_ALIGN, d), BF16),
                        pltpu.SemaphoreType.DMA(())])
    return pl.pallas_call(
        _dispatch_kernel,
        grid_spec=grid_spec,
        out_shape=jax.ShapeDtypeStruct((n_rows // SEG_ALIGN, SEG_ALIGN, d), BF16),
        compiler_params=_cparams("arbitrary"),
        name="dispatch",
    )(cnt, lstart, dst, total, tail_len, tail_dst, n_active, h2, lrow_t)


def _experts_kernel(be_ref, nact_ref, x_ref, wup_ref, bup_ref, wdn_ref, bdn_ref, y_ref,
                    wup_bf, wdn_bf):
    j = pl.program_id(0)
    dff = wdn_ref.shape[1]

    @pl.when(j < nact_ref[0])
    def _():
        prev = be_ref[jnp.maximum(j - 1, 0)]

        @pl.when((j == 0) | (be_ref[j] != prev))
        def _():
            wup_bf[...] = wup_ref[0].astype(BF16)
            wdn_bf[...] = wdn_ref[0].astype(BF16)

        x = x_ref[...].reshape(-1, x_ref.shape[2])
        u = jnp.dot(x, wup_bf[...], preferred_element_type=F32) + bup_ref[0]
        glu = jnp.minimum(u[:, :dff], SWIGLU_LIMIT)
        lin = jnp.clip(u[:, dff:], -SWIGLU_LIMIT, SWIGLU_LIMIT)
        act = glu * (1.0 / (1.0 + jnp.exp(-SWIGLU_ALPHA * glu))) * (lin + 1.0)
        y = jnp.dot(act.astype(BF16), wdn_bf[...], preferred_element_type=F32) + bdn_ref[0]
        y_ref[...] = y.astype(y_ref.dtype).reshape(y_ref.shape)

    @pl.when(j >= nact_ref[0])
    def _():
        y_ref[...] = jnp.zeros_like(y_ref)


def _experts(block_expert, n_active, x_sorted, w_up, b_up, w_down, b_down):
    n_blocks = block_expert.shape[0]
    tm = EXPERT_ROWS
    d = x_sorted.shape[2]
    e, _, up_w = w_up.shape
    dff = w_down.shape[1]
    by_expert = lambda j, be, na: (be[j], 0, 0)
    x_map = lambda j, be, na: (jnp.minimum(j, na[0] - 1), 0, 0)
    rows3 = (tm // SEG_ALIGN, SEG_ALIGN, d)
    grid_spec = pltpu.PrefetchScalarGridSpec(
        num_scalar_prefetch=2,
        grid=(n_blocks,),
        in_specs=[pl.BlockSpec(rows3, x_map),
                  pl.BlockSpec((1, d, up_w), by_expert),
                  pl.BlockSpec((1, 1, up_w), by_expert),
                  pl.BlockSpec((1, dff, d), by_expert),
                  pl.BlockSpec((1, 1, d), by_expert)],
        out_specs=pl.BlockSpec(rows3, lambda j, be, na: (j, 0, 0)),
        scratch_shapes=[pltpu.VMEM((d, up_w), BF16),
                        pltpu.VMEM((dff, d), BF16)])
    return pl.pallas_call(
        _experts_kernel,
        grid_spec=grid_spec,
        out_shape=jax.ShapeDtypeStruct(x_sorted.shape, BF16),
        compiler_params=_cparams("arbitrary"),
        name="experts",
    )(block_expert, n_active, x_sorted, w_up, b_up.reshape(e, 1, up_w), w_down,
      b_down.reshape(e, 1, d))


def _combine_kernel(cnt_ref, lstart_ref, dst_ref, total_ref, lrow_ref, gate_ref, x1_ref, mod_ref, g_ref, b_ref,
                    y_hbm, o_ref, ybuf, sem):
    i = pl.program_id(0)
    last = pl.num_programs(0) - 1
    slot = i % 2
    rows = ybuf.shape[1] * SEG_ALIGN
    tt = x1_ref.shape[0]
    d = x1_ref.shape[1]
    chunk_groups = PERM_CHUNK // SEG_ALIGN

    def make_copy(tile_slot):
        def mk(ls, ds, size):
            return pltpu.make_async_copy(y_hbm.at[pl.ds(ds, size)], ybuf.at[tile_slot, pl.ds(ls, size)],
                                         sem.at[tile_slot])
        return mk

    @pl.when(i == 0)
    def _():
        ybuf[...] = jnp.zeros_like(ybuf)
        _segment_starts(cnt_ref, lstart_ref, dst_ref, i, make_copy(slot))

    @pl.when(i < last)
    def _():
        _segment_starts(cnt_ref, lstart_ref, dst_ref, i + 1, make_copy(1 - slot))

    def weights(j):
        c_iota = lax.broadcasted_iota(jnp.int32, (tt, PERM_CHUNK), 1) + j * PERM_CHUNK
        w = jnp.zeros((tt, PERM_CHUNK), F32)
        for k in range(TOP_K):
            w = w + jnp.where(c_iota == lrow_ref[:, k:k + 1], gate_ref[:, k:k + 1], 0.0)
        return w.astype(BF16)

    w_next = weights(0)
    _segment_waits(total_ref, i, make_copy(slot))
    f = None
    for j in range(rows // PERM_CHUNK):
        w_cur = w_next
        if (j + 1) * PERM_CHUNK < rows:
            w_next = weights(j + 1)
        y_chunk = ybuf[slot, j * chunk_groups:(j + 1) * chunk_groups].reshape(PERM_CHUNK, d)
        part = jnp.dot(w_cur, y_chunk, preferred_element_type=F32)
        f = part if f is None else f + part
    o_ref[...] = _layer_norm(DN_ALPHA * x1_ref[...] + mod_ref[0, 5:6, :] * f, g_ref[...], b_ref[...])


def _combine(cnt, lstart, dst, total, lrow, gates, x1, mod, g, b, y_sorted, tokens_per_sample):
    n, d = x1.shape
    tt = TOKEN_TILE
    tiles_per_sample = tokens_per_sample // tt
    const2 = lambda i, *_: (0, 0)
    tok = lambda i, *_: (i, 0)
    grid_spec = pltpu.PrefetchScalarGridSpec(
        num_scalar_prefetch=4,
        grid=(n // tt,),
        in_specs=[pl.BlockSpec((tt, TOP_K), tok),
                  pl.BlockSpec((tt, TOP_K), tok),
                  pl.BlockSpec((tt, d), tok),
                  pl.BlockSpec((1, 6, d), lambda i, *_: (i // tiles_per_sample, 0, 0)),
                  pl.BlockSpec((1, d), const2),
                  pl.BlockSpec((1, d), const2),
                  pl.BlockSpec(memory_space=pl.ANY)],
        out_specs=pl.BlockSpec((tt, d), tok),
        scratch_shapes=[pltpu.VMEM((2, TILE_SORTED_ROWS // SEG_ALIGN, SEG_ALIGN, d), BF16),
                        pltpu.SemaphoreType.DMA((2,))])
    return pl.pallas_call(
        _combine_kernel,
        grid_spec=grid_spec,
        out_shape=jax.ShapeDtypeStruct((n, d), F32),
        compiler_params=_cparams("arbitrary"),
        name="combine",
    )(cnt, lstart, dst, total, lrow, gates, x1, mod, g, b, y_sorted)


def _rope_tables(t):
    rows = jnp.repeat(jnp.arange(t // GRID_W, dtype=jnp.int32), GRID_W).astype(F32)
    cols = jnp.tile(jnp.arange(GRID_W, dtype=jnp.int32), t // GRID_W).astype(F32)
    n_freq = HEAD_DIM // 4
    inv = ROPE_BASE ** (-jnp.arange(n_freq, dtype=F32) / n_freq)
    ang_r = rows[:, None] * inv
    ang_c = cols[:, None] * inv
    ang = jnp.concatenate([ang_r, ang_r, ang_c, ang_c], -1)
    ang = jnp.concatenate([ang, ang], -1)
    cos, sin = jnp.cos(ang), jnp.sin(ang)
    first_half = (jnp.arange(LANES) % 32) < 16
    return cos, jnp.where(first_half, -sin, 0.0), jnp.where(first_half, 0.0, sin)


def kernel(x, c, ctx, c_ctx, ln_in_g, ln_in_b, w_ada, b_ada, w_in, ret_log_decay_f, ret_log_decay_b,
           ret_gn_g, ret_gn_b, q_norm_g, k_norm_g, w_o, ln1_g, ln1_b, router_w, router_b,
           w_up, b_up, w_down, b_down, ln2_g, ln2_b):
    b, t, d = x.shape
    n = b * t
    row = lambda v: v.reshape(1, -1)

    cc = jnp.concatenate([c, c_ctx[None]], 0)
    cc = jnp.pad(cc, ((0, (-cc.shape[0]) % 8), (0, 0)))
    mod_all = _ada_mod(cc, w_ada[0], row(b_ada[0]))
    mod = mod_all[:b].reshape(b, 6, d)
    mod_ctx = mod_all[b:b + 1].reshape(1, 6, d)

    w_in_bf = w_in[0].astype(BF16)
    lane = jnp.arange(LANES)
    seg = (lane[:, None] // HEAD_DIM == lane[None, :] // HEAD_DIM).astype(BF16)
    qg = jnp.tile(q_norm_g[0], ATT_HEADS).reshape(1, -1)
    kg = jnp.tile(k_norm_g[0], ATT_KV_HEADS).reshape(1, -1)
    lng, lnb = row(ln_in_g), row(ln_in_b)

    rq, rkt, rv, rg, aq, akt, av = _in_proj(x, mod, lng, lnb, w_in_bf, seg, qg, kg, _rope_tables(t))
    _, rktc, rvc, _, _, aktc, avc = _in_proj(ctx, mod_ctx, lng, lnb, w_in_bf, seg, qg, kg, None)

    ret = _retention(row(ret_log_decay_f[0]), row(ret_log_decay_b[0]), rq, rkt, rv, rg, rktc, rvc,
                     row(ret_gn_g[0]), row(ret_gn_b[0]))
    tk = ctx.shape[1] + t
    av_ones = jnp.concatenate([jnp.concatenate([avc, av], 1), jnp.ones((b, tk, LANES), BF16)], 2)
    att = _attention(aq, jnp.concatenate([aktc, akt], 2), av_ones.astype(jnp.float8_e4m3fn))

    rw = jnp.pad(router_w[0], ((0, 0), (0, LANES - N_EXPERTS)))
    rb = jnp.pad(row(router_b[0]), ((0, 0), (0, LANES - N_EXPERTS)))
    rw_hi = rw.astype(BF16)
    rw_lo = (rw - rw_hi.astype(F32)).astype(BF16)
    tt = TOKEN_TILE
    n_tiles = n // tt
    before = jnp.triu(jnp.ones((tt, tt), BF16), 1)
    below = jnp.tril(jnp.ones((N_EXPERTS, N_EXPERTS), BF16), -1)
    x1, h2, gates_t, lrow_t, cnt, lstart, tbase = _post_mixer(
        ret, att, x, mod, lng, lnb, w_o[0].astype(BF16), row(ln1_g[0]), row(ln1_b[0]),
        rw_hi, rw_lo, rb, before, below)
    token_major = lambda v: v.transpose(0, 2, 1).reshape(n, TOP_K)
    gates, lrow = token_major(gates_t), token_major(lrow_t)

    tm = EXPERT_ROWS
    cnt, lstart, tbase = cnt[:, :, 0], lstart[:, :, 0], tbase[:, :, 0]
    totals = tbase[-1] + cnt[-1]
    exp_rows = (totals + tm - 1) // tm * tm
    exp_end = jnp.cumsum(exp_rows)
    exp_start = exp_end - exp_rows
    dst = (exp_start[None, :] + tbase).reshape(-1)
    max_rows = n * TOP_K + n_tiles * N_EXPERTS * (SEG_ALIGN - 1) + N_EXPERTS * (tm - 1)
    n_blocks = -(-max_rows // tm)
    block_start = jnp.arange(n_blocks, dtype=jnp.int32) * tm
    block_expert = jnp.minimum(jnp.sum(exp_end[None, :] <= block_start[:, None], axis=1),
                               N_EXPERTS - 1).astype(jnp.int32)
    n_active = (exp_end[-1:] // tm).astype(jnp.int32)
    groups = lambda v: (v // SEG_ALIGN).astype(jnp.int32).reshape(-1)
    cnt_g, lstart_g, dst_g, total_g = groups(cnt), groups(lstart), groups(dst), groups(jnp.sum(cnt, axis=1))

    x_sorted = _dispatch(cnt_g, lstart_g, dst_g, total_g, groups(exp_rows - totals), groups(exp_start + totals),
                         n_active, h2, lrow_t, n_blocks * tm)
    y_sorted = _experts(block_expert, n_active, x_sorted, w_up[0], b_up[0], w_down[0], b_down[0])
    out = _combine(cnt_g, lstart_g, dst_g, total_g, lrow, gates, x1, mod, row(ln2_g[0]), row(ln2_b[0]),
                   y_sorted, t)
    return out.reshape(b, t, d)
```

```python
import functools

import jax
import jax.numpy as jnp
from jax import lax
from jax.experimental import pallas as pl
from jax.experimental.pallas import tpu as pltpu

GRID_W = 64
HEAD_DIM = 64
ROPE_BASE = 10000.0
LN_EPS = 1e-6
RET_HEADS = 4
RET_DK = 64
RET_DV = 128
RET_CHUNK = 128
ATT_HEADS = 8
ATT_KV_HEADS = 2
ATT_GROUP = ATT_HEADS // ATT_KV_HEADS
N_EXPERTS = 32
TOP_K = 4
SWIGLU_LIMIT = 7.0
SWIGLU_ALPHA = 1.702
DEPTH = 1
DN_ALPHA = (2.0 * DEPTH) ** 0.25
LOG2_E = 1.4426950408889634

RET_Q_W = RET_HEADS * RET_DK
RET_V_W = RET_HEADS * RET_DV
ATT_Q_W = ATT_HEADS * HEAD_DIM
ATT_K_W = ATT_KV_HEADS * HEAD_DIM

LANES = 128
VMEM_LIMIT_BYTES = 56 * 1024 * 1024

TOKEN_TILE = 512
ATT_Q_TILE = 512
ATT_Q_SUB = 256
ATT_P_SHIFT = 8.0
EXPERT_ROWS = 512
SEG_ALIGN = 16
TILE_SORTED_ROWS = 3072
PERM_CHUNK = 512

F32 = jnp.float32
BF16 = jnp.bfloat16


def _cparams(*sem):
    return pltpu.CompilerParams(dimension_semantics=sem, vmem_limit_bytes=VMEM_LIMIT_BYTES)


def _layer_norm(x, g, b):
    mu = jnp.mean(x, -1, keepdims=True)
    xc = x - mu
    var = jnp.mean(xc * xc, -1, keepdims=True)
    return xc * lax.rsqrt(var + LN_EPS) * g + b


def _silu(x):
    return x * (1.0 / (1.0 + jnp.exp(-x)))


def _ada_kernel(c_ref, w_ref, b_ref, o_ref):
    a = _silu(c_ref[...])
    o_ref[...] = jnp.dot(a, w_ref[...], preferred_element_type=F32,
                         precision=lax.Precision.HIGHEST) + b_ref[...]


def _ada_mod(cc, w, b):
    m, d = cc.shape
    n = w.shape[1]
    tn = 1536
    return pl.pallas_call(
        _ada_kernel,
        grid=(n // tn,),
        in_specs=[pl.BlockSpec((m, d), lambda j: (0, 0)),
                  pl.BlockSpec((d, tn), lambda j: (0, j)),
                  pl.BlockSpec((1, tn), lambda j: (0, j))],
        out_specs=pl.BlockSpec((m, tn), lambda j: (0, j)),
        out_shape=jax.ShapeDtypeStruct((m, n), F32),
        compiler_params=_cparams("arbitrary"),
        name="ada_mod",
    )(cc, w, b)


def _rope(x, cos, sinn, sinp):
    outs = []
    for j in range(x.shape[1] // LANES):
        xg = x[:, j * LANES:(j + 1) * LANES]
        outs.append(xg * cos + pltpu.roll(xg, LANES - 16, 1) * sinn + pltpu.roll(xg, 16, 1) * sinp)
    return outs[0] if len(outs) == 1 else jnp.concatenate(outs, axis=1)


def _head_rms(x, seg, gain):
    outs = []
    for j in range(x.shape[1] // LANES):
        xg = x[:, j * LANES:(j + 1) * LANES]
        sq = xg * xg
        hi = sq.astype(BF16)
        lo = (sq - hi.astype(F32)).astype(BF16)
        ms = (jnp.dot(hi, seg, preferred_element_type=F32)
              + jnp.dot(lo, seg, preferred_element_type=F32)) * (1.0 / HEAD_DIM)
        outs.append(xg * lax.rsqrt(ms + LN_EPS) * gain[:, j * LANES:(j + 1) * LANES])
    return outs[0] if len(outs) == 1 else jnp.concatenate(outs, axis=1)


def _in_proj_kernel(*refs, rope):
    if rope:
        (x_ref, mod_ref, lng_ref, lnb_ref, w_ref, seg_ref, qg_ref, kg_ref, cos_ref, sinn_ref, sinp_ref,
         rq_ref, rkt_ref, rv_ref, rg_ref, aq_ref, akt_ref, av_ref) = refs
    else:
        (x_ref, mod_ref, lng_ref, lnb_ref, w_ref, seg_ref, qg_ref, kg_ref,
         rq_ref, rkt_ref, rv_ref, rg_ref, aq_ref, akt_ref, av_ref) = refs
    x = x_ref[0]
    xn = _layer_norm(x, lng_ref[...], lnb_ref[...])
    h = (xn * (1.0 + mod_ref[0, 1:2, :]) + mod_ref[0, 0:1, :]).astype(BF16)
    seg = seg_ref[...]

    def rotate(v):
        return _rope(v, cos_ref[...], sinn_ref[...], sinp_ref[...]) if rope else v

    o_rv = 2 * RET_Q_W
    o_aq = o_rv + 2 * RET_V_W
    o_ak = o_aq + ATT_Q_W
    o_end = o_ak + 2 * ATT_K_W

    def project(lo, hi):
        return jnp.dot(h, w_ref[:, lo:hi], preferred_element_type=F32)

    def finish_aq(u):
        aq = rotate(_head_rms(u, seg, qg_ref[...]))
        aq_ref[0] = (aq * (HEAD_DIM ** -0.5 * LOG2_E)).astype(BF16)

    def finish_akv(u):
        ak = rotate(_head_rms(u[:, :ATT_K_W], seg, kg_ref[...]))
        akt_ref[0] = ak.T.astype(BF16)
        av_ref[0] = u[:, ATT_K_W:].astype(BF16)

    def finish_rqk(u):
        rq_ref[0] = rotate(u[:, :RET_Q_W])
        rkt_ref[0] = (rotate(u[:, RET_Q_W:]) * (RET_DK ** -0.5)).T

    def finish_rvg(u):
        rv_ref[0] = u[:, :RET_V_W].astype(BF16)
        rg_ref[0] = u[:, RET_V_W:].astype(BF16)

    stages = [((o_aq, o_ak), finish_aq), ((o_ak, o_end), finish_akv),
              ((0, o_rv), finish_rqk), ((o_rv, o_aq), finish_rvg)]
    u_next = project(*stages[0][0])
    for idx, (_, finish) in enumerate(stages):
        u_cur = u_next
        if idx + 1 < len(stages):
            u_next = project(*stages[idx + 1][0])
        finish(u_cur)


def _in_proj(x, mod, lng, lnb, w_bf, seg, qg, kg, tables):
    b, t, d = x.shape
    tt = min(TOKEN_TILE, t)
    rope = tables is not None
    mod_b = mod.shape[0]
    mod_map = (lambda bi, i: (bi, 0, 0)) if mod_b > 1 else (lambda bi, i: (0, 0, 0))
    const2 = lambda bi, i: (0, 0)
    tok3 = lambda bi, i: (bi, i, 0)
    in_specs = [pl.BlockSpec((1, tt, d), tok3),
                pl.BlockSpec((1, 6, d), mod_map),
                pl.BlockSpec((1, d), const2),
                pl.BlockSpec((1, d), const2),
                pl.BlockSpec(w_bf.shape, const2),
                pl.BlockSpec(seg.shape, const2),
                pl.BlockSpec(qg.shape, const2),
                pl.BlockSpec(kg.shape, const2)]
    args = [x, mod, lng, lnb, w_bf, seg, qg, kg]
    if rope:
        in_specs += [pl.BlockSpec((tt, LANES), lambda bi, i: (i, 0))] * 3
        args += list(tables)
    out_shape = [jax.ShapeDtypeStruct((b, t, RET_Q_W), F32),
                 jax.ShapeDtypeStruct((b, RET_Q_W, t), F32),
                 jax.ShapeDtypeStruct((b, t, RET_V_W), BF16),
                 jax.ShapeDtypeStruct((b, t, RET_V_W), BF16),
                 jax.ShapeDtypeStruct((b, t, ATT_Q_W), BF16),
                 jax.ShapeDtypeStruct((b, ATT_K_W, t), BF16),
                 jax.ShapeDtypeStruct((b, t, ATT_K_W), BF16)]
    out_specs = [pl.BlockSpec((1, tt, RET_Q_W), tok3),
                 pl.BlockSpec((1, RET_Q_W, tt), lambda bi, i: (bi, 0, i)),
                 pl.BlockSpec((1, tt, RET_V_W), tok3),
                 pl.BlockSpec((1, tt, RET_V_W), tok3),
                 pl.BlockSpec((1, tt, ATT_Q_W), tok3),
                 pl.BlockSpec((1, ATT_K_W, tt), lambda bi, i: (bi, 0, i)),
                 pl.BlockSpec((1, tt, ATT_K_W), tok3)]
    return pl.pallas_call(
        functools.partial(_in_proj_kernel, rope=rope),
        grid=(b, t // tt),
        in_specs=in_specs,
        out_specs=out_specs,
        out_shape=out_shape,
        compiler_params=_cparams("parallel", "parallel"),
        name="in_proj_rope" if rope else "in_proj_ctx",
    )(*args)


def _retention_kernel(lgf_ref, lgb_ref, q_ref, kt_ref, v_ref, g_ref, ktc_ref, vc_ref, gng_ref, gnb_ref,
                      o_ref, kv_ref, rhs_ref):
    pair = pl.program_id(1)
    t = q_ref.shape[1]
    n_chunks = t // RET_CHUNK
    ctx_len = ktc_ref.shape[2]
    c = RET_CHUNK
    dk = RET_DK
    pos_col = lax.broadcasted_iota(jnp.int32, (c, 1), 0).astype(F32)
    pos_row = lax.broadcasted_iota(jnp.int32, (1, c), 1).astype(F32)
    row = lax.broadcasted_iota(jnp.int32, (c, c), 0)
    col = lax.broadcasted_iota(jnp.int32, (c, c), 1)
    diff = (row - col).astype(F32)
    lane = lax.broadcasted_iota(jnp.int32, (c, 2 * dk), 1)
    cpos = lax.broadcasted_iota(jnp.int32, (1, ctx_len), 1).astype(F32)

    heads = []
    for hh in range(2):
        head = pair * 2 + hh
        lgf = lgf_ref[0, head]
        lgb = lgb_ref[0, head]
        heads.append(dict(
            kl=slice(hh * dk, (hh + 1) * dk),
            vl=slice(hh * RET_DV, (hh + 1) * RET_DV),
            lgf=lgf, lgb=lgb,
            kdf=jnp.exp(lgf * (c - 1.0 - pos_row)),
            kdb=jnp.exp(lgb * pos_row),
            qd=jnp.where(lane < dk, jnp.exp(lgf * (pos_col + 1.0)), jnp.exp(lgb * (c - pos_col))),
            decay=(jnp.where(diff >= 0, jnp.exp(lgf * jnp.maximum(diff, 0.0)), 0.0)
                   + jnp.where(diff <= 0, jnp.exp(lgb * jnp.maximum(-diff, 0.0)), 0.0))))

    def kv_body(n, carry):
        r = pl.ds(pl.multiple_of(n * c, c), c)
        for hh, hd in enumerate(heads):
            kt = kt_ref[0, hd["kl"], r]
            vc = v_ref[0, r, hd["vl"]]
            both = jnp.concatenate([kt * hd["kdf"], kt * hd["kdb"]], axis=0).astype(BF16)
            kv_ref[hh, n] = jnp.dot(both, vc, preferred_element_type=F32)
            rhs_ref[hh, n, 0:c, :] = vc
        return carry

    lax.fori_loop(0, n_chunks, kv_body, 0, unroll=4)

    for hh, hd in enumerate(heads):
        lgf, lgb = hd["lgf"], hd["lgb"]
        ktc = ktc_ref[0, hd["kl"], :]
        both = jnp.concatenate([ktc * jnp.exp(lgf * (ctx_len - 1.0 - cpos)), ktc * jnp.exp(lgb * cpos)],
                               axis=0).astype(BF16)
        s_ctx = jnp.dot(both, vc_ref[0, :, hd["vl"]], preferred_element_type=F32)
        s_f, s_b = s_ctx[0:dk], s_ctx[dk:]
        gfc = jnp.exp(lgf * c)
        gbc = jnp.exp(lgb * c)
        for n in range(n_chunks):
            rhs_ref[hh, n, c:c + dk, :] = s_f.astype(BF16)
            s_f = gfc * s_f + kv_ref[hh, n, 0:dk, :]
        for n in range(n_chunks - 1, -1, -1):
            rhs_ref[hh, n, c + dk:, :] = s_b.astype(BF16)
            s_b = gbc * s_b + kv_ref[hh, n, dk:, :]

    def out_body(n, carry):
        r = pl.ds(pl.multiple_of(n * c, c), c)
        qq = q_ref[0, r, :]
        swapped = pltpu.roll(qq, dk, 1)
        for hh, hd in enumerate(heads):
            vl = hd["vl"]
            s = jnp.dot(q_ref[0, r, hd["kl"]].astype(BF16), kt_ref[0, hd["kl"], r].astype(BF16),
                        preferred_element_type=F32)
            q2 = jnp.where(lane < dk, qq, swapped) if hh == 0 else jnp.where(lane < dk, swapped, qq)
            lhs = jnp.concatenate([(s * hd["decay"]).astype(BF16), (q2 * hd["qd"]).astype(BF16)], axis=1)
            y = jnp.dot(lhs, rhs_ref[hh, n], preferred_element_type=F32)
            mu = jnp.mean(y, -1, keepdims=True)
            yc = y - mu
            var = jnp.mean(yc * yc, -1, keepdims=True)
            yn = yc * lax.rsqrt(var + LN_EPS) * gng_ref[:, vl] + gnb_ref[:, vl]
            o_ref[0, r, vl] = (yn * _silu(g_ref[0, r, vl].astype(F32))).astype(o_ref.dtype)
        return carry

    lax.fori_loop(0, n_chunks, out_body, 0, unroll=8)


def _retention(lgf, lgb, rq, rkt, rv, rg, rktc, rvc, gng, gnb):
    b, t, _ = rq.shape
    ctx_len = rktc.shape[2]
    n_chunks = t // RET_CHUNK
    pair3 = lambda bi, p: (bi, 0, p)
    pair_rows = lambda bi, p: (bi, p, 0)
    smem = pl.BlockSpec(memory_space=pltpu.SMEM)
    return pl.pallas_call(
        _retention_kernel,
        grid=(b, RET_HEADS // 2),
        in_specs=[smem, smem,
                  pl.BlockSpec((1, t, 2 * RET_DK), pair3),
                  pl.BlockSpec((1, 2 * RET_DK, t), pair_rows),
                  pl.BlockSpec((1, t, 2 * RET_DV), pair3),
                  pl.BlockSpec((1, t, 2 * RET_DV), pair3),
                  pl.BlockSpec((1, 2 * RET_DK, ctx_len), pair_rows),
                  pl.BlockSpec((1, ctx_len, 2 * RET_DV), pair3),
                  pl.BlockSpec((1, 2 * RET_DV), lambda bi, p: (0, p)),
                  pl.BlockSpec((1, 2 * RET_DV), lambda bi, p: (0, p))],
        out_specs=pl.BlockSpec((1, t, 2 * RET_DV), pair3),
        out_shape=jax.ShapeDtypeStruct((b, t, RET_V_W), BF16),
        scratch_shapes=[pltpu.VMEM((2, n_chunks, 2 * RET_DK, RET_DV), F32),
                        pltpu.VMEM((2, n_chunks, RET_CHUNK + 2 * RET_DK, RET_DV), BF16)],
        compiler_params=_cparams("parallel", "parallel"),
        name="retention",
    )(lgf, lgb, rq, rkt, rv, rg, rktc, rvc, gng, gnb)


def _attention_kernel(q_ref, kt_ref, v_ref, o_ref, s_buf, p_buf):
    ts = ATT_Q_SUB
    lane = lax.broadcasted_iota(jnp.int32, (ts, LANES), 1)
    pieces = [(sub, kvh) for sub in range(q_ref.shape[1] // ts) for kvh in range(ATT_KV_HEADS)]

    def scores(i):
        sub, kvh = pieces[i]
        q = jnp.concatenate(
            [q_ref[0, sub * ts:(sub + 1) * ts, (kvh * ATT_GROUP + g) * HEAD_DIM:(kvh * ATT_GROUP + g + 1) * HEAD_DIM]
             for g in range(ATT_GROUP)], axis=0)
        s_buf[i % 2] = jnp.dot(q, kt_ref[0, kvh * HEAD_DIM:(kvh + 1) * HEAD_DIM, :], preferred_element_type=F32)

    def numerators(i):
        s = s_buf[i % 2]
        top = jnp.max(s, -1, keepdims=True) - ATT_P_SHIFT
        p_buf[i % 2] = jnp.exp2(s - top).astype(jnp.float8_e4m3fn)

    def values(i):
        sub, kvh = pieces[i]
        ov = jnp.dot(p_buf[i % 2], v_ref[0], preferred_element_type=F32)
        o = ov[:, :LANES] / ov[:, LANES:LANES + 1]
        for pair in range(ATT_GROUP // 2):
            halves = []
            for half in range(2):
                g = pair * 2 + half
                og = o[g * ts:(g + 1) * ts, :]
                if kvh != half:
                    og = pltpu.roll(og, HEAD_DIM, 1)
                halves.append(og)
            col = (kvh * ATT_GROUP // 2 + pair) * LANES
            o_ref[0, sub * ts:(sub + 1) * ts, col:col + LANES] = jnp.where(
                lane < HEAD_DIM, halves[0], halves[1]).astype(o_ref.dtype)

    scores(0)
    scores(1)
    numerators(0)
    for i in range(len(pieces)):
        if i + 2 < len(pieces):
            scores(i + 2)
        if i + 1 < len(pieces):
            numerators(i + 1)
        values(i)


def _attention(aq, akt, av_ones):
    b, t, _ = aq.shape
    tk = akt.shape[2]
    tq = ATT_Q_TILE
    return pl.pallas_call(
        _attention_kernel,
        grid=(b, t // tq),
        in_specs=[pl.BlockSpec((1, tq, ATT_Q_W), lambda bi, i: (bi, i, 0)),
                  pl.BlockSpec((1, ATT_K_W, tk), lambda bi, i: (bi, 0, 0)),
                  pl.BlockSpec((1, tk, av_ones.shape[2]), lambda bi, i: (bi, 0, 0))],
        out_specs=pl.BlockSpec((1, tq, ATT_Q_W), lambda bi, i: (bi, i, 0)),
        out_shape=jax.ShapeDtypeStruct((b, t, ATT_Q_W), BF16),
        scratch_shapes=[pltpu.VMEM((2, ATT_GROUP * ATT_Q_SUB, tk), F32),
                        pltpu.VMEM((2, ATT_GROUP * ATT_Q_SUB, tk), jnp.float8_e4m3fn)],
        compiler_params=_cparams("parallel", "parallel"),
        name="attention",
    )(aq, akt, av_ones)


def _post_mixer_kernel(ret_ref, att_ref, x_ref, mod_ref, lng_ref, lnb_ref, wo_ref, l1g_ref, l1b_ref,
                       rwh_ref, rwl_ref, rb_ref, before_ref, below_ref,
                       x1_ref, h2_ref, gate_ref, lrow_ref, cnt_ref, lstart_ref, tbase_ref, phi_ref, rem_ref,
                       base_acc):
    first = (pl.program_id(0) == 0) & (pl.program_id(1) == 0)

    @pl.when(first)
    def _():
        base_acc[...] = jnp.zeros_like(base_acc)

    half = ret_ref.shape[2]
    y = (jnp.dot(ret_ref[0], wo_ref[0:half, :], preferred_element_type=F32)
         + jnp.dot(att_ref[0], wo_ref[half:, :], preferred_element_type=F32))
    xn = _layer_norm(x_ref[0], lng_ref[...], lnb_ref[...])
    x1 = _layer_norm(DN_ALPHA * xn + mod_ref[0, 2:3, :] * y, l1g_ref[...], l1b_ref[...])
    x1_ref[...] = x1
    h2 = x1 * (1.0 + mod_ref[0, 4:5, :]) + mod_ref[0, 3:4, :]
    h_hi = h2.astype(BF16)
    h2_ref[...] = h_hi

    h_lo = (h2 - h_hi.astype(F32)).astype(BF16)
    logits = (jnp.dot(h_hi, rwh_ref[...], preferred_element_type=F32)
              + jnp.dot(h_lo, rwh_ref[...], preferred_element_type=F32)
              + jnp.dot(h_hi, rwl_ref[...], preferred_element_type=F32)) + rb_ref[...]

    tt = logits.shape[0]
    work = logits.T[0:N_EXPERTS, :]
    e_iota = lax.broadcasted_iota(jnp.int32, (N_EXPERTS, tt), 0)
    sels, vals = [], []
    onehot = jnp.zeros((N_EXPERTS, tt), F32)
    for _ in range(TOP_K):
        m = jnp.max(work, 0, keepdims=True)
        idx = jnp.min(jnp.where(work == m, e_iota, N_EXPERTS), 0, keepdims=True)
        sel = e_iota == idx
        sels.append(sel)
        vals.append(m)
        onehot = onehot + sel.astype(F32)
        work = jnp.where(sel, -jnp.inf, work)
    exps = [jnp.exp(v - vals[0]) for v in vals]
    denom = exps[0] + exps[1] + exps[2] + exps[3]

    count = jnp.sum(onehot, 1, keepdims=True)
    base = base_acc[...]
    phi = base - SEG_ALIGN * jnp.floor(base * (1.0 / SEG_ALIGN))
    units = jnp.where(count > 0, jnp.floor((phi + count + (SEG_ALIGN - 1.0)) * (1.0 / SEG_ALIGN)), 0.0)
    lstart = SEG_ALIGN * jnp.dot(below_ref[...], jnp.broadcast_to(units, (N_EXPERTS, LANES)).astype(BF16),
                                 preferred_element_type=F32)[:, 0:1]
    rank = jnp.dot(onehot.astype(BF16), before_ref[...], preferred_element_type=F32) + (lstart + phi)
    end = phi + count

    k_iota = lax.broadcasted_iota(jnp.int32, (TOP_K, tt), 0)
    gate_out = jnp.zeros((TOP_K, tt), F32)
    lrow_out = jnp.zeros((TOP_K, tt), F32)
    for k in range(TOP_K):
        pk = jnp.sum(jnp.where(sels[k], rank, 0.0), 0, keepdims=True)
        gate_out = jnp.where(k_iota == k, exps[k] / denom, gate_out)
        lrow_out = jnp.where(k_iota == k, pk, lrow_out)
    gate_ref[0] = gate_out
    lrow_ref[0] = lrow_out.astype(jnp.int32)
    cnt_ref[0] = (units * SEG_ALIGN).astype(jnp.int32)
    lstart_ref[0] = lstart.astype(jnp.int32)
    tbase_ref[0] = (base - phi).astype(jnp.int32)
    phi_ref[0] = jnp.where(count > 0, phi, 0.0).astype(jnp.int32)
    rem_ref[0] = jnp.where(count > 0, end - SEG_ALIGN * jnp.floor(end * (1.0 / SEG_ALIGN)), 0.0).astype(jnp.int32)
    base_acc[...] = base + count


def _post_mixer(ret, att, x, mod, lng, lnb, wo_bf, l1g, l1b, rw_hi, rw_lo, rb, before, below):
    b, t, d = x.shape
    tt = TOKEN_TILE
    nt = t // tt
    n = b * t
    const2 = lambda bi, i: (0, 0)
    tok3 = lambda bi, i: (bi, i, 0)
    flat = lambda bi, i: (bi * nt + i, 0)
    tile3 = lambda bi, i: (bi * nt + i, 0, 0)
    vec = pl.BlockSpec((1, d), const2)
    table = jax.ShapeDtypeStruct((b * nt, N_EXPERTS, 1), jnp.int32)
    table_spec = pl.BlockSpec((1, N_EXPERTS, 1), tile3)
    per_k = pl.BlockSpec((1, TOP_K, tt), tile3)
    return pl.pallas_call(
        _post_mixer_kernel,
        grid=(b, nt),
        in_specs=[pl.BlockSpec((1, tt, ret.shape[2]), tok3),
                  pl.BlockSpec((1, tt, att.shape[2]), tok3),
                  pl.BlockSpec((1, tt, d), tok3),
                  pl.BlockSpec((1, 6, d), lambda bi, i: (bi, 0, 0)),
                  vec, vec,
                  pl.BlockSpec(wo_bf.shape, const2),
                  vec, vec,
                  pl.BlockSpec(rw_hi.shape, const2),
                  pl.BlockSpec(rw_lo.shape, const2),
                  pl.BlockSpec(rb.shape, const2),
                  pl.BlockSpec((tt, tt), const2),
                  pl.BlockSpec((N_EXPERTS, N_EXPERTS), const2)],
        out_specs=[pl.BlockSpec((tt, d), flat),
                   pl.BlockSpec((tt, d), flat),
                   per_k, per_k, table_spec, table_spec, table_spec, table_spec, table_spec],
        out_shape=[jax.ShapeDtypeStruct((n, d), F32),
                   jax.ShapeDtypeStruct((n, d), BF16),
                   jax.ShapeDtypeStruct((b * nt, TOP_K, tt), F32),
                   jax.ShapeDtypeStruct((b * nt, TOP_K, tt), jnp.int32),
                   table, table, table, table, table],
        scratch_shapes=[pltpu.VMEM((N_EXPERTS, 1), F32)],
        compiler_params=_cparams("arbitrary", "arbitrary"),
        name="post_mixer",
    )(ret, att, x, mod, lng, lnb, wo_bf, l1g, l1b, rw_hi, rw_lo, rb, before, below)


def _segment_starts(cnt_ref, lstart_ref, dst_ref, tile, make_copy):
    def body(e, carry):
        t = tile * N_EXPERTS + e
        n = cnt_ref[t]

        @pl.when(n > 0)
        def _():
            make_copy(lstart_ref[t], dst_ref[t], n).start()
        return carry
    lax.fori_loop(0, N_EXPERTS, body, 0, unroll=4)


def _segment_waits(total_ref, tile, make_copy):
    make_copy(0, 0, total_ref[tile]).wait()


def _zero_fill_copies(tail_len_ref, tail_dst_ref, nact_ref, zbuf, x_hbm, zsem, act):
    block = zbuf.shape[0]

    def tail_body(e, carry):
        n = tail_len_ref[e]

        @pl.when(n > 0)
        def _():
            copy = pltpu.make_async_copy(zbuf.at[pl.ds(0, n)], x_hbm.at[pl.ds(tail_dst_ref[e], n)], zsem)
            getattr(copy, act)()
        return carry
    lax.fori_loop(0, N_EXPERTS, tail_body, 0)

    def block_body(j, carry):
        copy = pltpu.make_async_copy(zbuf, x_hbm.at[pl.ds(j * block, block)], zsem)
        getattr(copy, act)()
        return carry
    lax.fori_loop(nact_ref[0], x_hbm.shape[0] // block, block_body, 0)


def _dispatch_kernel(cnt_ref, lstart_ref, dst_ref, total_ref, phi_ref, rem_ref, tail_len_ref, tail_dst_ref, nact_ref,
                     h_ref, lrow_ref, x_hbm, sbuf, sem, zbuf, zsem, carry):
    i = pl.program_id(0)
    last = pl.num_programs(0) - 1
    slot = i % 2
    rows = sbuf.shape[1] * SEG_ALIGN
    tt = h_ref.shape[0]
    d = h_ref.shape[1]
    chunk_groups = PERM_CHUNK // SEG_ALIGN

    def make_copy(tile_slot):
        def mk(ls, ds, size):
            return pltpu.make_async_copy(sbuf.at[tile_slot, pl.ds(ls, size)], x_hbm.at[pl.ds(ds, size)],
                                         sem.at[tile_slot])
        return mk

    @pl.when(i == 0)
    def _():
        zbuf[...] = jnp.zeros_like(zbuf)
        carry[...] = jnp.zeros_like(carry)
        _zero_fill_copies(tail_len_ref, tail_dst_ref, nact_ref, zbuf, x_hbm, zsem, "start")

    def perm(j):
        r_iota = lax.broadcasted_iota(jnp.int32, (PERM_CHUNK, tt), 0) + j * PERM_CHUNK
        hit = r_iota == lrow_ref[0, 0:1, :]
        for k in range(1, TOP_K):
            hit = hit | (r_iota == lrow_ref[0, k:k + 1, :])
        return jnp.where(hit, 1.0, 0.0).astype(BF16)

    def place(j, p):
        sbuf[slot, j * chunk_groups:(j + 1) * chunk_groups] = jnp.dot(
            p, h_ref[...], preferred_element_type=F32).astype(BF16).reshape(chunk_groups, SEG_ALIGN, d)

    n_chunks = rows // PERM_CHUNK
    p_next = perm(0)
    for j in range(n_chunks - 1):
        p_cur = p_next
        if j + 2 < n_chunks:
            p_next = perm(j + 1)
        place(j, p_cur)

    @pl.when(total_ref[i] > (n_chunks - 1) * chunk_groups)
    def _():
        place(n_chunks - 1, perm(n_chunks - 1))
    def merge_body(e, c):
        t = i * N_EXPERTS + e
        n = cnt_ref[t]
        first = lstart_ref[t]

        @pl.when((n > 0) & (phi_ref[t] > 0))
        def _():
            sbuf[slot, first] = sbuf[slot, first] + carry[e]

        @pl.when((n > 0) & (rem_ref[t] > 0))
        def _():
            carry[e] = sbuf[slot, first + n - 1]

        @pl.when((n > 0) & (rem_ref[t] == 0))
        def _():
            carry[e] = jnp.zeros(carry.shape[1:], carry.dtype)
        return c
    lax.fori_loop(0, N_EXPERTS, merge_body, 0)

    @pl.when(i > 0)
    def _():
        _segment_waits(total_ref, i - 1, make_copy(1 - slot))
    _segment_starts(cnt_ref, lstart_ref, dst_ref, i, make_copy(slot))

    @pl.when(i == last)
    def _():
        _segment_waits(total_ref, i, make_copy(slot))
        _zero_fill_copies(tail_len_ref, tail_dst_ref, nact_ref, zbuf, x_hbm, zsem, "wait")


def _dispatch(cnt, lstart, dst, total, phi, rem, tail_len, tail_dst, n_active, h2, lrow_t, n_rows):
    n, d = h2.shape
    tt = TOKEN_TILE
    grid_spec = pltpu.PrefetchScalarGridSpec(
        num_scalar_prefetch=9,
        grid=(n // tt,),
        in_specs=[pl.BlockSpec((tt, d), lambda i, *_: (i, 0)),
                  pl.BlockSpec((1, TOP_K, tt), lambda i, *_: (i, 0, 0))],
        out_specs=pl.BlockSpec(memory_space=pl.ANY),
        scratch_shapes=[pltpu.VMEM((2, TILE_SORTED_ROWS // SEG_ALIGN, SEG_ALIGN, d), BF16),
                        pltpu.SemaphoreType.DMA((2,)),
                        pltpu.VMEM((EXPERT_ROWS // SEG_ALIGN, SEG_ALIGN, d), BF16),
                        pltpu.SemaphoreType.DMA(()),
                        pltpu.VMEM((N_EXPERTS, SEG_ALIGN, d), BF16)])
    return pl.pallas_call(
        _dispatch_kernel,
        grid_spec=grid_spec,
        out_shape=jax.ShapeDtypeStruct((n_rows // SEG_ALIGN, SEG_ALIGN, d), BF16),
        compiler_params=_cparams("arbitrary"),
        name="dispatch",
    )(cnt, lstart, dst, total, phi, rem, tail_len, tail_dst, n_active, h2, lrow_t)


def _experts_kernel(be_ref, nact_ref, x_ref, wup_ref, bup_ref, wdn_ref, bdn_ref, y_ref,
                    wup_bf, wdn_bf):
    j = pl.program_id(0)
    dff = wdn_ref.shape[1]

    @pl.when(j < nact_ref[0])
    def _():
        prev = be_ref[jnp.maximum(j - 1, 0)]

        @pl.when((j == 0) | (be_ref[j] != prev))
        def _():
            wup_bf[...] = wup_ref[0].astype(BF16)
            wdn_bf[...] = wdn_ref[0].astype(BF16)

        x = x_ref[...].reshape(-1, x_ref.shape[2])
        u = jnp.dot(x, wup_bf[...], preferred_element_type=F32) + bup_ref[0]
        glu = jnp.minimum(u[:, :dff], SWIGLU_LIMIT)
        lin = jnp.clip(u[:, dff:], -SWIGLU_LIMIT, SWIGLU_LIMIT)
        act = glu * (1.0 / (1.0 + jnp.exp(-SWIGLU_ALPHA * glu))) * (lin + 1.0)
        y = jnp.dot(act.astype(BF16), wdn_bf[...], preferred_element_type=F32) + bdn_ref[0]
        y_ref[...] = y.astype(y_ref.dtype).reshape(y_ref.shape)

    @pl.when(j >= nact_ref[0])
    def _():
        y_ref[...] = jnp.zeros_like(y_ref)


def _experts(block_expert, n_active, x_sorted, w_up, b_up, w_down, b_down):
    n_blocks = block_expert.shape[0]
    tm = EXPERT_ROWS
    d = x_sorted.shape[2]
    e, _, up_w = w_up.shape
    dff = w_down.shape[1]
    by_expert = lambda j, be, na: (be[j], 0, 0)
    x_map = lambda j, be, na: (jnp.minimum(j, na[0] - 1), 0, 0)
    rows3 = (tm // SEG_ALIGN, SEG_ALIGN, d)
    grid_spec = pltpu.PrefetchScalarGridSpec(
        num_scalar_prefetch=2,
        grid=(n_blocks,),
        in_specs=[pl.BlockSpec(rows3, x_map),
                  pl.BlockSpec((1, d, up_w), by_expert),
                  pl.BlockSpec((1, 1, up_w), by_expert),
                  pl.BlockSpec((1, dff, d), by_expert),
                  pl.BlockSpec((1, 1, d), by_expert)],
        out_specs=pl.BlockSpec(rows3, lambda j, be, na: (j, 0, 0)),
        scratch_shapes=[pltpu.VMEM((d, up_w), BF16),
                        pltpu.VMEM((dff, d), BF16)])
    return pl.pallas_call(
        _experts_kernel,
        grid_spec=grid_spec,
        out_shape=jax.ShapeDtypeStruct(x_sorted.shape, BF16),
        compiler_params=_cparams("arbitrary"),
        name="experts",
    )(block_expert, n_active, x_sorted, w_up, b_up.reshape(e, 1, up_w), w_down,
      b_down.reshape(e, 1, d))


def _combine_kernel(cnt_ref, lstart_ref, dst_ref, total_ref, lrow_ref, gate_ref, x1_ref, mod_ref, g_ref, b_ref,
                    y_hbm, o_ref, ybuf, sem):
    i = pl.program_id(0)
    last = pl.num_programs(0) - 1
    slot = i % 2
    rows = ybuf.shape[1] * SEG_ALIGN
    tt = x1_ref.shape[0]
    d = x1_ref.shape[1]
    chunk_groups = PERM_CHUNK // SEG_ALIGN

    def make_copy(tile_slot):
        def mk(ls, ds, size):
            return pltpu.make_async_copy(y_hbm.at[pl.ds(ds, size)], ybuf.at[tile_slot, pl.ds(ls, size)],
                                         sem.at[tile_slot])
        return mk

    @pl.when(i == 0)
    def _():
        ybuf[...] = jnp.zeros_like(ybuf)
        _segment_starts(cnt_ref, lstart_ref, dst_ref, i, make_copy(slot))

    @pl.when(i < last)
    def _():
        _segment_starts(cnt_ref, lstart_ref, dst_ref, i + 1, make_copy(1 - slot))

    def weights(j):
        c_iota = lax.broadcasted_iota(jnp.int32, (tt, PERM_CHUNK), 1) + j * PERM_CHUNK
        w = jnp.zeros((tt, PERM_CHUNK), F32)
        for k in range(TOP_K):
            w = jnp.where(c_iota == lrow_ref[:, k:k + 1], gate_ref[:, k:k + 1], w)
        return w.astype(BF16)

    def gather(j, w):
        y_chunk = ybuf[slot, j * chunk_groups:(j + 1) * chunk_groups].reshape(PERM_CHUNK, d)
        return jnp.dot(w, y_chunk, preferred_element_type=F32)

    def finish(f):
        o_ref[...] = _layer_norm(DN_ALPHA * x1_ref[...] + mod_ref[0, 5:6, :] * f, g_ref[...], b_ref[...])

    n_chunks = rows // PERM_CHUNK
    w_next = weights(0)
    _segment_waits(total_ref, i, make_copy(slot))
    f = None
    for j in range(n_chunks - 1):
        w_cur = w_next
        if j + 2 < n_chunks:
            w_next = weights(j + 1)
        part = gather(j, w_cur)
        f = part if f is None else f + part

    need_last = total_ref[i] > (n_chunks - 1) * chunk_groups

    @pl.when(need_last)
    def _():
        finish(f + gather(n_chunks - 1, weights(n_chunks - 1)))

    @pl.when(jnp.logical_not(need_last))
    def _():
        finish(f)


def _combine(cnt, lstart, dst, total, lrow, gates, x1, mod, g, b, y_sorted, tokens_per_sample):
    n, d = x1.shape
    tt = TOKEN_TILE
    tiles_per_sample = tokens_per_sample // tt
    const2 = lambda i, *_: (0, 0)
    tok = lambda i, *_: (i, 0)
    grid_spec = pltpu.PrefetchScalarGridSpec(
        num_scalar_prefetch=4,
        grid=(n // tt,),
        in_specs=[pl.BlockSpec((tt, TOP_K), tok),
                  pl.BlockSpec((tt, TOP_K), tok),
                  pl.BlockSpec((tt, d), tok),
                  pl.BlockSpec((1, 6, d), lambda i, *_: (i // tiles_per_sample, 0, 0)),
                  pl.BlockSpec((1, d), const2),
                  pl.BlockSpec((1, d), const2),
                  pl.BlockSpec(memory_space=pl.ANY)],
        out_specs=pl.BlockSpec((tt, d), tok),
        scratch_shapes=[pltpu.VMEM((2, TILE_SORTED_ROWS // SEG_ALIGN, SEG_ALIGN, d), BF16),
                        pltpu.SemaphoreType.DMA((2,))])
    return pl.pallas_call(
        _combine_kernel,
        grid_spec=grid_spec,
        out_shape=jax.ShapeDtypeStruct((n, d), F32),
        compiler_params=_cparams("arbitrary"),
        name="combine",
    )(cnt, lstart, dst, total, lrow, gates, x1, mod, g, b, y_sorted)


def _rope_tables(t):
    rows = jnp.repeat(jnp.arange(t // GRID_W, dtype=jnp.int32), GRID_W).astype(F32)
    cols = jnp.tile(jnp.arange(GRID_W, dtype=jnp.int32), t // GRID_W).astype(F32)
    n_freq = HEAD_DIM // 4
    inv = ROPE_BASE ** (-jnp.arange(n_freq, dtype=F32) / n_freq)
    ang_r = rows[:, None] * inv
    ang_c = cols[:, None] * inv
    ang = jnp.concatenate([ang_r, ang_r, ang_c, ang_c], -1)
    ang = jnp.concatenate([ang, ang], -1)
    cos, sin = jnp.cos(ang), jnp.sin(ang)
    first_half = (jnp.arange(LANES) % 32) < 16
    return cos, jnp.where(first_half, -sin, 0.0), jnp.where(first_half, 0.0, sin)


def kernel(x, c, ctx, c_ctx, ln_in_g, ln_in_b, w_ada, b_ada, w_in, ret_log_decay_f, ret_log_decay_b,
           ret_gn_g, ret_gn_b, q_norm_g, k_norm_g, w_o, ln1_g, ln1_b, router_w, router_b,
           w_up, b_up, w_down, b_down, ln2_g, ln2_b):
    b, t, d = x.shape
    n = b * t
    row = lambda v: v.reshape(1, -1)

    cc = jnp.concatenate([c, c_ctx[None]], 0)
    cc = jnp.pad(cc, ((0, (-cc.shape[0]) % 8), (0, 0)))
    mod_all = _ada_mod(cc, w_ada[0], row(b_ada[0]))
    mod = mod_all[:b].reshape(b, 6, d)
    mod_ctx = mod_all[b:b + 1].reshape(1, 6, d)

    w_in_bf = w_in[0].astype(BF16)
    lane = jnp.arange(LANES)
    seg = (lane[:, None] // HEAD_DIM == lane[None, :] // HEAD_DIM).astype(BF16)
    qg = jnp.tile(q_norm_g[0], ATT_HEADS).reshape(1, -1)
    kg = jnp.tile(k_norm_g[0], ATT_KV_HEADS).reshape(1, -1)
    lng, lnb = row(ln_in_g), row(ln_in_b)

    rq, rkt, rv, rg, aq, akt, av = _in_proj(x, mod, lng, lnb, w_in_bf, seg, qg, kg, _rope_tables(t))
    _, rktc, rvc, _, _, aktc, avc = _in_proj(ctx, mod_ctx, lng, lnb, w_in_bf, seg, qg, kg, None)

    ret = _retention(row(ret_log_decay_f[0]), row(ret_log_decay_b[0]), rq, rkt, rv, rg, rktc, rvc,
                     row(ret_gn_g[0]), row(ret_gn_b[0]))
    tk = ctx.shape[1] + t
    av_ones = jnp.concatenate([jnp.concatenate([avc, av], 1), jnp.ones((b, tk, LANES), BF16)], 2)
    att = _attention(aq, jnp.concatenate([aktc, akt], 2), av_ones.astype(jnp.float8_e4m3fn))

    rw = jnp.pad(router_w[0], ((0, 0), (0, LANES - N_EXPERTS)))
    rb = jnp.pad(row(router_b[0]), ((0, 0), (0, LANES - N_EXPERTS)))
    rw_hi = rw.astype(BF16)
    rw_lo = (rw - rw_hi.astype(F32)).astype(BF16)
    tt = TOKEN_TILE
    n_tiles = n // tt
    before = jnp.triu(jnp.ones((tt, tt), BF16), 1)
    below = jnp.tril(jnp.ones((N_EXPERTS, N_EXPERTS), BF16), -1)
    x1, h2, gates_t, lrow_t, cnt, lstart, tbase, phi, rem = _post_mixer(
        ret, att, x, mod, lng, lnb, w_o[0].astype(BF16), row(ln1_g[0]), row(ln1_b[0]),
        rw_hi, rw_lo, rb, before, below)
    token_major = lambda v: v.transpose(0, 2, 1).reshape(n, TOP_K)
    gates, lrow = token_major(gates_t), token_major(lrow_t)

    tm = EXPERT_ROWS
    cnt, lstart, tbase, phi, rem = (v[:, :, 0] for v in (cnt, lstart, tbase, phi, rem))
    totals = jnp.max(tbase + cnt, axis=0)
    exp_rows = (totals + tm - 1) // tm * tm
    exp_end = jnp.cumsum(exp_rows)
    exp_start = exp_end - exp_rows
    dst = (exp_start[None, :] + tbase).reshape(-1)
    max_rows = n * TOP_K + N_EXPERTS * (SEG_ALIGN - 1) + N_EXPERTS * (tm - 1)
    n_blocks = -(-max_rows // tm)
    block_start = jnp.arange(n_blocks, dtype=jnp.int32) * tm
    block_expert = jnp.minimum(jnp.sum(exp_end[None, :] <= block_start[:, None], axis=1),
                               N_EXPERTS - 1).astype(jnp.int32)
    n_active = (exp_end[-1:] // tm).astype(jnp.int32)
    groups = lambda v: (v // SEG_ALIGN).astype(jnp.int32).reshape(-1)
    cnt_g, lstart_g, dst_g, total_g = groups(cnt), groups(lstart), groups(dst), groups(jnp.sum(cnt, axis=1))

    flat = lambda v: v.astype(jnp.int32).reshape(-1)
    x_sorted = _dispatch(cnt_g, lstart_g, dst_g, total_g, flat(phi), flat(rem), groups(exp_rows - totals),
                         groups(exp_start + totals), n_active, h2, lrow_t, n_blocks * tm)
    y_sorted = _experts(block_expert, n_active, x_sorted, w_up[0], b_up[0], w_down[0], b_down[0])
    out = _combine(cnt_g, lstart_g, dst_g, total_g, lrow, gates, x1, mod, row(ln2_g[0]), row(ln2_b[0]),
                   y_sorted, t)
    return out.reshape(b, t, d)
```

```python
import functools

import jax
import jax.numpy as jnp
from jax import lax
from jax.experimental import pallas as pl
from jax.experimental.pallas import tpu as pltpu

GRID_W = 64
HEAD_DIM = 64
ROPE_BASE = 10000.0
LN_EPS = 1e-6
RET_HEADS = 4
RET_DK = 64
RET_DV = 128
RET_CHUNK = 128
ATT_HEADS = 8
ATT_KV_HEADS = 2
ATT_GROUP = ATT_HEADS // ATT_KV_HEADS
N_EXPERTS = 32
TOP_K = 4
SWIGLU_LIMIT = 7.0
SWIGLU_ALPHA = 1.702
DEPTH = 1
DN_ALPHA = (2.0 * DEPTH) ** 0.25
LOG2_E = 1.4426950408889634

RET_Q_W = RET_HEADS * RET_DK
RET_V_W = RET_HEADS * RET_DV
ATT_Q_W = ATT_HEADS * HEAD_DIM
ATT_K_W = ATT_KV_HEADS * HEAD_DIM

LANES = 128
VMEM_LIMIT_BYTES = 56 * 1024 * 1024

TOKEN_TILE = 512
ATT_Q_TILE = 512
ATT_Q_SUB = 256
ATT_P_SHIFT = 8.0
EXPERT_ROWS = 512
SEG_ALIGN = 16
TILE_SORTED_ROWS = 3072
PERM_CHUNK = 512

F32 = jnp.float32
BF16 = jnp.bfloat16


def _cparams(*sem):
    return pltpu.CompilerParams(dimension_semantics=sem, vmem_limit_bytes=VMEM_LIMIT_BYTES)


def _layer_norm(x, g, b):
    mu = jnp.mean(x, -1, keepdims=True)
    xc = x - mu
    var = jnp.mean(xc * xc, -1, keepdims=True)
    return xc * lax.rsqrt(var + LN_EPS) * g + b


def _silu(x):
    return x * (1.0 / (1.0 + jnp.exp(-x)))


def _ada_kernel(c_ref, w_ref, b_ref, o_ref):
    a = _silu(c_ref[...])
    o_ref[...] = jnp.dot(a, w_ref[...], preferred_element_type=F32,
                         precision=lax.Precision.HIGHEST) + b_ref[...]


def _ada_mod(cc, w, b):
    m, d = cc.shape
    n = w.shape[1]
    tn = 1536
    return pl.pallas_call(
        _ada_kernel,
        grid=(n // tn,),
        in_specs=[pl.BlockSpec((m, d), lambda j: (0, 0)),
                  pl.BlockSpec((d, tn), lambda j: (0, j)),
                  pl.BlockSpec((1, tn), lambda j: (0, j))],
        out_specs=pl.BlockSpec((m, tn), lambda j: (0, j)),
        out_shape=jax.ShapeDtypeStruct((m, n), F32),
        compiler_params=_cparams("arbitrary"),
        name="ada_mod",
    )(cc, w, b)


def _rope(x, cos, sinn, sinp):
    outs = []
    for j in range(x.shape[1] // LANES):
        xg = x[:, j * LANES:(j + 1) * LANES]
        outs.append(xg * cos + pltpu.roll(xg, LANES - 16, 1) * sinn + pltpu.roll(xg, 16, 1) * sinp)
    return outs[0] if len(outs) == 1 else jnp.concatenate(outs, axis=1)


def _head_rms(x, seg, gain):
    outs = []
    for j in range(x.shape[1] // LANES):
        xg = x[:, j * LANES:(j + 1) * LANES]
        sq = xg * xg
        hi = sq.astype(BF16)
        lo = (sq - hi.astype(F32)).astype(BF16)
        ms = (jnp.dot(hi, seg, preferred_element_type=F32)
              + jnp.dot(lo, seg, preferred_element_type=F32)) * (1.0 / HEAD_DIM)
        outs.append(xg * lax.rsqrt(ms + LN_EPS) * gain[:, j * LANES:(j + 1) * LANES])
    return outs[0] if len(outs) == 1 else jnp.concatenate(outs, axis=1)


def _in_proj_kernel(*refs, rope):
    if rope:
        (x_ref, mod_ref, lng_ref, lnb_ref, w_ref, seg_ref, qg_ref, kg_ref, cos_ref, sinn_ref, sinp_ref,
         rq_ref, rkt_ref, rv_ref, rg_ref, aq_ref, akt_ref, av_ref) = refs
    else:
        (x_ref, mod_ref, lng_ref, lnb_ref, w_ref, seg_ref, qg_ref, kg_ref,
         rq_ref, rkt_ref, rv_ref, rg_ref, aq_ref, akt_ref, av_ref) = refs
    x = x_ref[0]
    xn = _layer_norm(x, lng_ref[...], lnb_ref[...])
    h = (xn * (1.0 + mod_ref[0, 1:2, :]) + mod_ref[0, 0:1, :]).astype(BF16)
    seg = seg_ref[...]

    def rotate(v):
        return _rope(v, cos_ref[...], sinn_ref[...], sinp_ref[...]) if rope else v

    o_rv = 2 * RET_Q_W
    o_aq = o_rv + 2 * RET_V_W
    o_ak = o_aq + ATT_Q_W
    o_end = o_ak + 2 * ATT_K_W

    def project(lo, hi):
        return jnp.dot(h, w_ref[:, lo:hi], preferred_element_type=F32)

    def finish_aq(u):
        aq = rotate(_head_rms(u, seg, qg_ref[...]))
        aq_ref[0] = (aq * (HEAD_DIM ** -0.5 * LOG2_E)).astype(BF16)

    def finish_akv(u):
        ak = rotate(_head_rms(u[:, :ATT_K_W], seg, kg_ref[...]))
        akt_ref[0] = ak.T.astype(BF16)
        av_ref[0] = u[:, ATT_K_W:].astype(BF16)

    def finish_rqk(u):
        rq_ref[0] = rotate(u[:, :RET_Q_W])
        rkt_ref[0] = (rotate(u[:, RET_Q_W:]) * (RET_DK ** -0.5)).T

    def finish_rvg(u):
        rv_ref[0] = u[:, :RET_V_W].astype(BF16)
        rg_ref[0] = u[:, RET_V_W:].astype(BF16)

    stages = [((o_aq, o_ak), finish_aq), ((o_ak, o_end), finish_akv),
              ((0, o_rv), finish_rqk), ((o_rv, o_aq), finish_rvg)]
    u_next = project(*stages[0][0])
    for idx, (_, finish) in enumerate(stages):
        u_cur = u_next
        if idx + 1 < len(stages):
            u_next = project(*stages[idx + 1][0])
        finish(u_cur)


def _in_proj(x, mod, lng, lnb, w_bf, seg, qg, kg, tables):
    b, t, d = x.shape
    tt = min(TOKEN_TILE, t)
    rope = tables is not None
    mod_b = mod.shape[0]
    mod_map = (lambda bi, i: (bi, 0, 0)) if mod_b > 1 else (lambda bi, i: (0, 0, 0))
    const2 = lambda bi, i: (0, 0)
    tok3 = lambda bi, i: (bi, i, 0)
    in_specs = [pl.BlockSpec((1, tt, d), tok3),
                pl.BlockSpec((1, 6, d), mod_map),
                pl.BlockSpec((1, d), const2),
                pl.BlockSpec((1, d), const2),
                pl.BlockSpec(w_bf.shape, const2),
                pl.BlockSpec(seg.shape, const2),
                pl.BlockSpec(qg.shape, const2),
                pl.BlockSpec(kg.shape, const2)]
    args = [x, mod, lng, lnb, w_bf, seg, qg, kg]
    if rope:
        in_specs += [pl.BlockSpec((tt, LANES), lambda bi, i: (i, 0))] * 3
        args += list(tables)
    out_shape = [jax.ShapeDtypeStruct((b, t, RET_Q_W), F32),
                 jax.ShapeDtypeStruct((b, RET_Q_W, t), F32),
                 jax.ShapeDtypeStruct((b, t, RET_V_W), BF16),
                 jax.ShapeDtypeStruct((b, t, RET_V_W), BF16),
                 jax.ShapeDtypeStruct((b, t, ATT_Q_W), BF16),
                 jax.ShapeDtypeStruct((b, ATT_K_W, t), BF16),
                 jax.ShapeDtypeStruct((b, t, ATT_K_W), BF16)]
    out_specs = [pl.BlockSpec((1, tt, RET_Q_W), tok3),
                 pl.BlockSpec((1, RET_Q_W, tt), lambda bi, i: (bi, 0, i)),
                 pl.BlockSpec((1, tt, RET_V_W), tok3),
                 pl.BlockSpec((1, tt, RET_V_W), tok3),
                 pl.BlockSpec((1, tt, ATT_Q_W), tok3),
                 pl.BlockSpec((1, ATT_K_W, tt), lambda bi, i: (bi, 0, i)),
                 pl.BlockSpec((1, tt, ATT_K_W), tok3)]
    return pl.pallas_call(
        functools.partial(_in_proj_kernel, rope=rope),
        grid=(b, t // tt),
        in_specs=in_specs,
        out_specs=out_specs,
        out_shape=out_shape,
        compiler_params=_cparams("parallel", "parallel"),
        name="in_proj_rope" if rope else "in_proj_ctx",
    )(*args)


def _retention_kernel(lgf_ref, lgb_ref, q_ref, kt_ref, v_ref, g_ref, ktc_ref, vc_ref, gng_ref, gnb_ref,
                      o_ref, kv_ref, rhs_ref):
    pair = pl.program_id(1)
    t = q_ref.shape[1]
    n_chunks = t // RET_CHUNK
    ctx_len = ktc_ref.shape[2]
    c = RET_CHUNK
    dk = RET_DK
    pos_col = lax.broadcasted_iota(jnp.int32, (c, 1), 0).astype(F32)
    pos_row = lax.broadcasted_iota(jnp.int32, (1, c), 1).astype(F32)
    row = lax.broadcasted_iota(jnp.int32, (c, c), 0)
    col = lax.broadcasted_iota(jnp.int32, (c, c), 1)
    diff = (row - col).astype(F32)
    lane = lax.broadcasted_iota(jnp.int32, (c, 2 * dk), 1)
    cpos = lax.broadcasted_iota(jnp.int32, (1, ctx_len), 1).astype(F32)

    heads = []
    for hh in range(2):
        head = pair * 2 + hh
        lgf = lgf_ref[0, head]
        lgb = lgb_ref[0, head]
        heads.append(dict(
            kl=slice(hh * dk, (hh + 1) * dk),
            vl=slice(hh * RET_DV, (hh + 1) * RET_DV),
            lgf=lgf, lgb=lgb,
            kdf=jnp.exp(lgf * (c - 1.0 - pos_row)),
            kdb=jnp.exp(lgb * pos_row),
            qd=jnp.where(lane < dk, jnp.exp(lgf * (pos_col + 1.0)), jnp.exp(lgb * (c - pos_col))),
            decay=(jnp.where(diff >= 0, jnp.exp(lgf * jnp.maximum(diff, 0.0)), 0.0)
                   + jnp.where(diff <= 0, jnp.exp(lgb * jnp.maximum(-diff, 0.0)), 0.0))))

    def kv_body(n, carry):
        r = pl.ds(pl.multiple_of(n * c, c), c)
        for hh, hd in enumerate(heads):
            kt = kt_ref[0, hd["kl"], r]
            vc = v_ref[0, r, hd["vl"]]
            both = jnp.concatenate([kt * hd["kdf"], kt * hd["kdb"]], axis=0).astype(BF16)
            kv_ref[hh, n] = jnp.dot(both, vc, preferred_element_type=F32)
            rhs_ref[hh, n, 0:c, :] = vc
        return carry

    lax.fori_loop(0, n_chunks, kv_body, 0, unroll=4)

    for hh, hd in enumerate(heads):
        lgf, lgb = hd["lgf"], hd["lgb"]
        ktc = ktc_ref[0, hd["kl"], :]
        both = jnp.concatenate([ktc * jnp.exp(lgf * (ctx_len - 1.0 - cpos)), ktc * jnp.exp(lgb * cpos)],
                               axis=0).astype(BF16)
        s_ctx = jnp.dot(both, vc_ref[0, :, hd["vl"]], preferred_element_type=F32)
        s_f, s_b = s_ctx[0:dk], s_ctx[dk:]
        gfc = jnp.exp(lgf * c)
        gbc = jnp.exp(lgb * c)
        for n in range(n_chunks):
            rhs_ref[hh, n, c:c + dk, :] = s_f.astype(BF16)
            s_f = gfc * s_f + kv_ref[hh, n, 0:dk, :]
        for n in range(n_chunks - 1, -1, -1):
            rhs_ref[hh, n, c + dk:, :] = s_b.astype(BF16)
            s_b = gbc * s_b + kv_ref[hh, n, dk:, :]

    def out_body(n, carry):
        r = pl.ds(pl.multiple_of(n * c, c), c)
        qq = q_ref[0, r, :]
        swapped = pltpu.roll(qq, dk, 1)
        for hh, hd in enumerate(heads):
            vl = hd["vl"]
            s = jnp.dot(q_ref[0, r, hd["kl"]].astype(BF16), kt_ref[0, hd["kl"], r].astype(BF16),
                        preferred_element_type=F32)
            q2 = jnp.where(lane < dk, qq, swapped) if hh == 0 else jnp.where(lane < dk, swapped, qq)
            lhs = jnp.concatenate([(s * hd["decay"]).astype(BF16), (q2 * hd["qd"]).astype(BF16)], axis=1)
            y = jnp.dot(lhs, rhs_ref[hh, n], preferred_element_type=F32)
            mu = jnp.mean(y, -1, keepdims=True)
            yc = y - mu
            var = jnp.mean(yc * yc, -1, keepdims=True)
            yn = yc * lax.rsqrt(var + LN_EPS) * gng_ref[:, vl] + gnb_ref[:, vl]
            o_ref[0, r, vl] = (yn * _silu(g_ref[0, r, vl].astype(F32))).astype(o_ref.dtype)
        return carry

    lax.fori_loop(0, n_chunks, out_body, 0, unroll=8)


def _retention(lgf, lgb, rq, rkt, rv, rg, rktc, rvc, gng, gnb):
    b, t, _ = rq.shape
    ctx_len = rktc.shape[2]
    n_chunks = t // RET_CHUNK
    pair3 = lambda bi, p: (bi, 0, p)
    pair_rows = lambda bi, p: (bi, p, 0)
    smem = pl.BlockSpec(memory_space=pltpu.SMEM)
    return pl.pallas_call(
        _retention_kernel,
        grid=(b, RET_HEADS // 2),
        in_specs=[smem, smem,
                  pl.BlockSpec((1, t, 2 * RET_DK), pair3),
                  pl.BlockSpec((1, 2 * RET_DK, t), pair_rows),
                  pl.BlockSpec((1, t, 2 * RET_DV), pair3),
                  pl.BlockSpec((1, t, 2 * RET_DV), pair3),
                  pl.BlockSpec((1, 2 * RET_DK, ctx_len), pair_rows),
                  pl.BlockSpec((1, ctx_len, 2 * RET_DV), pair3),
                  pl.BlockSpec((1, 2 * RET_DV), lambda bi, p: (0, p)),
                  pl.BlockSpec((1, 2 * RET_DV), lambda bi, p: (0, p))],
        out_specs=pl.BlockSpec((1, t, 2 * RET_DV), pair3),
        out_shape=jax.ShapeDtypeStruct((b, t, RET_V_W), BF16),
        scratch_shapes=[pltpu.VMEM((2, n_chunks, 2 * RET_DK, RET_DV), F32),
                        pltpu.VMEM((2, n_chunks, RET_CHUNK + 2 * RET_DK, RET_DV), BF16)],
        compiler_params=_cparams("parallel", "parallel"),
        name="retention",
    )(lgf, lgb, rq, rkt, rv, rg, rktc, rvc, gng, gnb)


def _attention_kernel(q_ref, kt_ref, v_ref, o_ref, s_buf, p_buf):
    ts = ATT_Q_SUB
    lane = lax.broadcasted_iota(jnp.int32, (ts, LANES), 1)
    pieces = [(sub, kvh) for sub in range(q_ref.shape[1] // ts) for kvh in range(ATT_KV_HEADS)]

    def scores(i):
        sub, kvh = pieces[i]
        q = jnp.concatenate(
            [q_ref[0, sub * ts:(sub + 1) * ts, (kvh * ATT_GROUP + g) * HEAD_DIM:(kvh * ATT_GROUP + g + 1) * HEAD_DIM]
             for g in range(ATT_GROUP)], axis=0)
        s_buf[i % 2] = jnp.dot(q, kt_ref[0, kvh * HEAD_DIM:(kvh + 1) * HEAD_DIM, :], preferred_element_type=F32)

    def numerators(i):
        s = s_buf[i % 2]
        top = jnp.max(s, -1, keepdims=True) - ATT_P_SHIFT
        p_buf[i % 2] = jnp.exp2(s - top).astype(jnp.float8_e4m3fn)

    def values(i):
        sub, kvh = pieces[i]
        ov = jnp.dot(p_buf[i % 2], v_ref[0], preferred_element_type=F32)
        o = ov[:, :LANES] / ov[:, LANES:LANES + 1]
        for pair in range(ATT_GROUP // 2):
            halves = []
            for half in range(2):
                g = pair * 2 + half
                og = o[g * ts:(g + 1) * ts, :]
                if kvh != half:
                    og = pltpu.roll(og, HEAD_DIM, 1)
                halves.append(og)
            col = (kvh * ATT_GROUP // 2 + pair) * LANES
            o_ref[0, sub * ts:(sub + 1) * ts, col:col + LANES] = jnp.where(
                lane < HEAD_DIM, halves[0], halves[1]).astype(o_ref.dtype)

    scores(0)
    scores(1)
    numerators(0)
    for i in range(len(pieces)):
        if i + 2 < len(pieces):
            scores(i + 2)
        if i + 1 < len(pieces):
            numerators(i + 1)
        values(i)


def _attention(aq, akt, av_ones):
    b, t, _ = aq.shape
    tk = akt.shape[2]
    tq = ATT_Q_TILE
    return pl.pallas_call(
        _attention_kernel,
        grid=(b, t // tq),
        in_specs=[pl.BlockSpec((1, tq, ATT_Q_W), lambda bi, i: (bi, i, 0)),
                  pl.BlockSpec((1, ATT_K_W, tk), lambda bi, i: (bi, 0, 0)),
                  pl.BlockSpec((1, tk, av_ones.shape[2]), lambda bi, i: (bi, 0, 0))],
        out_specs=pl.BlockSpec((1, tq, ATT_Q_W), lambda bi, i: (bi, i, 0)),
        out_shape=jax.ShapeDtypeStruct((b, t, ATT_Q_W), BF16),
        scratch_shapes=[pltpu.VMEM((2, ATT_GROUP * ATT_Q_SUB, tk), F32),
                        pltpu.VMEM((2, ATT_GROUP * ATT_Q_SUB, tk), jnp.float8_e4m3fn)],
        compiler_params=_cparams("parallel", "parallel"),
        name="attention",
    )(aq, akt, av_ones)


def _post_mixer_kernel(ret_ref, att_ref, x_ref, mod_ref, lng_ref, lnb_ref, wo_ref, l1g_ref, l1b_ref,
                       rwh_ref, rwl_ref, rb_ref, before_ref, below_ref,
                       x1_ref, h2_ref, gate_ref, lrow_ref, cnt_ref, lstart_ref, tbase_ref, phi_ref, rem_ref,
                       base_acc):
    first = (pl.program_id(0) == 0) & (pl.program_id(1) == 0)

    @pl.when(first)
    def _():
        base_acc[...] = jnp.zeros_like(base_acc)

    half = ret_ref.shape[2]
    y = (jnp.dot(ret_ref[0], wo_ref[0:half, :], preferred_element_type=F32)
         + jnp.dot(att_ref[0], wo_ref[half:, :], preferred_element_type=F32))
    xn = _layer_norm(x_ref[0], lng_ref[...], lnb_ref[...])
    x1 = _layer_norm(DN_ALPHA * xn + mod_ref[0, 2:3, :] * y, l1g_ref[...], l1b_ref[...])
    x1_ref[...] = x1
    h2 = x1 * (1.0 + mod_ref[0, 4:5, :]) + mod_ref[0, 3:4, :]
    h_hi = h2.astype(BF16)
    h2_ref[...] = h_hi

    h_lo = (h2 - h_hi.astype(F32)).astype(BF16)
    logits = (jnp.dot(h_hi, rwh_ref[...], preferred_element_type=F32)
              + jnp.dot(h_lo, rwh_ref[...], preferred_element_type=F32)
              + jnp.dot(h_hi, rwl_ref[...], preferred_element_type=F32)) + rb_ref[...]

    tt = logits.shape[0]
    work = logits.T[0:N_EXPERTS, :]
    e_iota = lax.broadcasted_iota(jnp.int32, (N_EXPERTS, tt), 0)
    sels, vals = [], []
    onehot = jnp.zeros((N_EXPERTS, tt), F32)
    for _ in range(TOP_K):
        m = jnp.max(work, 0, keepdims=True)
        idx = jnp.min(jnp.where(work == m, e_iota, N_EXPERTS), 0, keepdims=True)
        sel = e_iota == idx
        sels.append(sel)
        vals.append(m)
        onehot = onehot + sel.astype(F32)
        work = jnp.where(sel, -jnp.inf, work)
    exps = [jnp.exp(v - vals[0]) for v in vals]
    denom = exps[0] + exps[1] + exps[2] + exps[3]

    count = jnp.sum(onehot, 1, keepdims=True)
    base = base_acc[...]
    phi = base - SEG_ALIGN * jnp.floor(base * (1.0 / SEG_ALIGN))
    units = jnp.where(count > 0, jnp.floor((phi + count + (SEG_ALIGN - 1.0)) * (1.0 / SEG_ALIGN)), 0.0)
    lstart = SEG_ALIGN * jnp.dot(below_ref[...], jnp.broadcast_to(units, (N_EXPERTS, LANES)).astype(BF16),
                                 preferred_element_type=F32)[:, 0:1]
    rank = jnp.dot(onehot.astype(BF16), before_ref[...], preferred_element_type=F32) + (lstart + phi)
    end = phi + count

    k_iota = lax.broadcasted_iota(jnp.int32, (TOP_K, tt), 0)
    gate_out = jnp.zeros((TOP_K, tt), F32)
    lrow_out = jnp.zeros((TOP_K, tt), F32)
    for k in range(TOP_K):
        pk = jnp.sum(jnp.where(sels[k], rank, 0.0), 0, keepdims=True)
        gate_out = jnp.where(k_iota == k, exps[k] / denom, gate_out)
        lrow_out = jnp.where(k_iota == k, pk, lrow_out)
    gate_ref[0] = gate_out
    lrow_ref[0] = lrow_out.astype(jnp.int32)
    cnt_ref[0] = (units * SEG_ALIGN).astype(jnp.int32)
    lstart_ref[0] = lstart.astype(jnp.int32)
    tbase_ref[0] = (base - phi).astype(jnp.int32)
    phi_ref[0] = jnp.where(count > 0, phi, 0.0).astype(jnp.int32)
    rem_ref[0] = jnp.where(count > 0, end - SEG_ALIGN * jnp.floor(end * (1.0 / SEG_ALIGN)), 0.0).astype(jnp.int32)
    base_acc[...] = base + count


def _post_mixer(ret, att, x, mod, lng, lnb, wo_bf, l1g, l1b, rw_hi, rw_lo, rb, before, below):
    b, t, d = x.shape
    tt = TOKEN_TILE
    nt = t // tt
    n = b * t
    const2 = lambda bi, i: (0, 0)
    tok3 = lambda bi, i: (bi, i, 0)
    flat = lambda bi, i: (bi * nt + i, 0)
    tile3 = lambda bi, i: (bi * nt + i, 0, 0)
    vec = pl.BlockSpec((1, d), const2)
    table = jax.ShapeDtypeStruct((b * nt, N_EXPERTS, 1), jnp.int32)
    table_spec = pl.BlockSpec((1, N_EXPERTS, 1), tile3)
    per_k = pl.BlockSpec((1, TOP_K, tt), tile3)
    return pl.pallas_call(
        _post_mixer_kernel,
        grid=(b, nt),
        in_specs=[pl.BlockSpec((1, tt, ret.shape[2]), tok3),
                  pl.BlockSpec((1, tt, att.shape[2]), tok3),
                  pl.BlockSpec((1, tt, d), tok3),
                  pl.BlockSpec((1, 6, d), lambda bi, i: (bi, 0, 0)),
                  vec, vec,
                  pl.BlockSpec(wo_bf.shape, const2),
                  vec, vec,
                  pl.BlockSpec(rw_hi.shape, const2),
                  pl.BlockSpec(rw_lo.shape, const2),
                  pl.BlockSpec(rb.shape, const2),
                  pl.BlockSpec((tt, tt), const2),
                  pl.BlockSpec((N_EXPERTS, N_EXPERTS), const2)],
        out_specs=[pl.BlockSpec((tt, d), flat),
                   pl.BlockSpec((tt, d), flat),
                   per_k, per_k, table_spec, table_spec, table_spec, table_spec, table_spec],
        out_shape=[jax.ShapeDtypeStruct((n, d), F32),
                   jax.ShapeDtypeStruct((n, d), BF16),
                   jax.ShapeDtypeStruct((b * nt, TOP_K, tt), F32),
                   jax.ShapeDtypeStruct((b * nt, TOP_K, tt), jnp.int32),
                   table, table, table, table, table],
        scratch_shapes=[pltpu.VMEM((N_EXPERTS, 1), F32)],
        compiler_params=_cparams("arbitrary", "arbitrary"),
        name="post_mixer",
    )(ret, att, x, mod, lng, lnb, wo_bf, l1g, l1b, rw_hi, rw_lo, rb, before, below)


def _segment_starts(cnt_ref, lstart_ref, dst_ref, tile, make_copy):
    def body(e, carry):
        t = tile * N_EXPERTS + e
        n = cnt_ref[t]

        @pl.when(n > 0)
        def _():
            make_copy(lstart_ref[t], dst_ref[t], n).start()
        return carry
    lax.fori_loop(0, N_EXPERTS, body, 0, unroll=4)


def _segment_waits(total_ref, tile, make_copy):
    make_copy(0, 0, total_ref[tile]).wait()


def _zero_fill_copies(tail_len_ref, tail_dst_ref, nact_ref, zbuf, x_hbm, zsem, act):
    block = zbuf.shape[0]

    def tail_body(e, carry):
        n = tail_len_ref[e]

        @pl.when(n > 0)
        def _():
            copy = pltpu.make_async_copy(zbuf.at[pl.ds(0, n)], x_hbm.at[pl.ds(tail_dst_ref[e], n)], zsem)
            getattr(copy, act)()
        return carry
    lax.fori_loop(0, N_EXPERTS, tail_body, 0)

    def block_body(j, carry):
        copy = pltpu.make_async_copy(zbuf, x_hbm.at[pl.ds(j * block, block)], zsem)
        getattr(copy, act)()
        return carry
    lax.fori_loop(nact_ref[0], x_hbm.shape[0] // block, block_body, 0)


def _dispatch_kernel(cnt_ref, ncopy_ref, lstart_ref, dst_ref, total_ref, extent_ref, phi_ref, rem_ref, flush_ref,
                     flush_dst_ref,
                     tail_len_ref, tail_dst_ref, nact_ref,
                     h_ref, lrow_ref, x_hbm, sbuf, sem, zbuf, zsem, carry):
    i = pl.program_id(0)
    last = pl.num_programs(0) - 1
    slot = i % 2
    rows = sbuf.shape[1] * SEG_ALIGN
    tt = h_ref.shape[0]
    d = h_ref.shape[1]
    chunk_groups = PERM_CHUNK // SEG_ALIGN

    def make_copy(tile_slot):
        def mk(ls, ds, size):
            return pltpu.make_async_copy(sbuf.at[tile_slot, pl.ds(ls, size)], x_hbm.at[pl.ds(ds, size)],
                                         sem.at[tile_slot])
        return mk

    @pl.when(i == 0)
    def _():
        zbuf[...] = jnp.zeros_like(zbuf)
        carry[...] = jnp.zeros_like(carry)
        _zero_fill_copies(tail_len_ref, tail_dst_ref, nact_ref, zbuf, x_hbm, zsem, "start")

    def perm(j):
        r_iota = lax.broadcasted_iota(jnp.int32, (PERM_CHUNK, tt), 0) + j * PERM_CHUNK
        hit = r_iota == lrow_ref[0, 0:1, :]
        for k in range(1, TOP_K):
            hit = hit | (r_iota == lrow_ref[0, k:k + 1, :])
        return jnp.where(hit, 1.0, 0.0).astype(BF16)

    def place(j, p):
        sbuf[slot, j * chunk_groups:(j + 1) * chunk_groups] = jnp.dot(
            p, h_ref[...], preferred_element_type=F32).astype(BF16).reshape(chunk_groups, SEG_ALIGN, d)

    n_chunks = rows // PERM_CHUNK
    p_next = perm(0)
    for j in range(n_chunks - 1):
        p_cur = p_next
        if j + 2 < n_chunks:
            p_next = perm(j + 1)
        place(j, p_cur)

    @pl.when(extent_ref[i] > (n_chunks - 1) * chunk_groups)
    def _():
        place(n_chunks - 1, perm(n_chunks - 1))
    def merge_body(e, c):
        t = i * N_EXPERTS + e
        n = cnt_ref[t]
        first = lstart_ref[t]

        @pl.when((n > 0) & (phi_ref[t] > 0))
        def _():
            sbuf[slot, first] = sbuf[slot, first] + carry[e]

        @pl.when((n > 0) & (rem_ref[t] > 0))
        def _():
            carry[e] = sbuf[slot, first + n - 1]

        @pl.when((n > 0) & (rem_ref[t] == 0))
        def _():
            carry[e] = jnp.zeros(carry.shape[1:], carry.dtype)
        return c
    lax.fori_loop(0, N_EXPERTS, merge_body, 0)

    _segment_starts(ncopy_ref, lstart_ref, dst_ref, i, make_copy(slot))

    @pl.when(i > 0)
    def _():
        _segment_waits(total_ref, i - 1, make_copy(1 - slot))

    def flush_copies(act):
        def body(e, c):
            @pl.when(flush_ref[e] > 0)
            def _():
                copy = pltpu.make_async_copy(carry.at[pl.ds(e, 1)], x_hbm.at[pl.ds(flush_dst_ref[e], 1)], zsem)
                getattr(copy, act)()
            return c
        lax.fori_loop(0, N_EXPERTS, body, 0)

    @pl.when(i == last)
    def _():
        flush_copies("start")
        _segment_waits(total_ref, i, make_copy(slot))
        _zero_fill_copies(tail_len_ref, tail_dst_ref, nact_ref, zbuf, x_hbm, zsem, "wait")
        flush_copies("wait")


def _dispatch(cnt, ncopy, lstart, dst, total, extent, phi, rem, flush, flush_dst, tail_len, tail_dst, n_active,
              h2, lrow_t, n_rows):
    n, d = h2.shape
    tt = TOKEN_TILE
    grid_spec = pltpu.PrefetchScalarGridSpec(
        num_scalar_prefetch=13,
        grid=(n // tt,),
        in_specs=[pl.BlockSpec((tt, d), lambda i, *_: (i, 0)),
                  pl.BlockSpec((1, TOP_K, tt), lambda i, *_: (i, 0, 0))],
        out_specs=pl.BlockSpec(memory_space=pl.ANY),
        scratch_shapes=[pltpu.VMEM((2, TILE_SORTED_ROWS // SEG_ALIGN, SEG_ALIGN, d), BF16),
                        pltpu.SemaphoreType.DMA((2,)),
                        pltpu.VMEM((EXPERT_ROWS // SEG_ALIGN, SEG_ALIGN, d), BF16),
                        pltpu.SemaphoreType.DMA(()),
                        pltpu.VMEM((N_EXPERTS, SEG_ALIGN, d), BF16)])
    return pl.pallas_call(
        _dispatch_kernel,
        grid_spec=grid_spec,
        out_shape=jax.ShapeDtypeStruct((n_rows // SEG_ALIGN, SEG_ALIGN, d), BF16),
        compiler_params=_cparams("arbitrary"),
        name="dispatch",
    )(cnt, ncopy, lstart, dst, total, extent, phi, rem, flush, flush_dst, tail_len, tail_dst, n_active, h2, lrow_t)


def _experts_kernel(be_ref, nact_ref, x_ref, wup_ref, bup_ref, wdn_ref, bdn_ref, y_ref,
                    wup_bf, wdn_bf):
    j = pl.program_id(0)
    dff = wdn_ref.shape[1]

    @pl.when(j < nact_ref[0])
    def _():
        prev = be_ref[jnp.maximum(j - 1, 0)]

        @pl.when((j == 0) | (be_ref[j] != prev))
        def _():
            wup_bf[...] = wup_ref[0].astype(BF16)
            wdn_bf[...] = wdn_ref[0].astype(BF16)

        x = x_ref[...].reshape(-1, x_ref.shape[2])
        u = jnp.dot(x, wup_bf[...], preferred_element_type=F32) + bup_ref[0]
        glu = jnp.minimum(u[:, :dff], SWIGLU_LIMIT)
        lin = jnp.clip(u[:, dff:], -SWIGLU_LIMIT, SWIGLU_LIMIT)
        act = glu * (1.0 / (1.0 + jnp.exp(-SWIGLU_ALPHA * glu))) * (lin + 1.0)
        y = jnp.dot(act.astype(BF16), wdn_bf[...], preferred_element_type=F32) + bdn_ref[0]
        y_ref[...] = y.astype(y_ref.dtype).reshape(y_ref.shape)

    @pl.when(j >= nact_ref[0])
    def _():
        y_ref[...] = jnp.zeros_like(y_ref)


def _experts(block_expert, n_active, x_sorted, w_up, b_up, w_down, b_down):
    n_blocks = block_expert.shape[0]
    tm = EXPERT_ROWS
    d = x_sorted.shape[2]
    e, _, up_w = w_up.shape
    dff = w_down.shape[1]
    by_expert = lambda j, be, na: (be[j], 0, 0)
    x_map = lambda j, be, na: (jnp.minimum(j, na[0] - 1), 0, 0)
    rows3 = (tm // SEG_ALIGN, SEG_ALIGN, d)
    grid_spec = pltpu.PrefetchScalarGridSpec(
        num_scalar_prefetch=2,
        grid=(n_blocks,),
        in_specs=[pl.BlockSpec(rows3, x_map),
                  pl.BlockSpec((1, d, up_w), by_expert),
                  pl.BlockSpec((1, 1, up_w), by_expert),
                  pl.BlockSpec((1, dff, d), by_expert),
                  pl.BlockSpec((1, 1, d), by_expert)],
        out_specs=pl.BlockSpec(rows3, lambda j, be, na: (j, 0, 0)),
        scratch_shapes=[pltpu.VMEM((d, up_w), BF16),
                        pltpu.VMEM((dff, d), BF16)])
    return pl.pallas_call(
        _experts_kernel,
        grid_spec=grid_spec,
        out_shape=jax.ShapeDtypeStruct(x_sorted.shape, BF16),
        compiler_params=_cparams("arbitrary"),
        name="experts",
    )(block_expert, n_active, x_sorted, w_up, b_up.reshape(e, 1, up_w), w_down,
      b_down.reshape(e, 1, d))


def _combine_kernel(cnt_ref, lstart_ref, dst_ref, total_ref, lrow_ref, gate_ref, x1_ref, mod_ref, g_ref, b_ref,
                    y_hbm, o_ref, ybuf, sem):
    i = pl.program_id(0)
    last = pl.num_programs(0) - 1
    slot = i % 2
    rows = ybuf.shape[1] * SEG_ALIGN
    tt = x1_ref.shape[0]
    d = x1_ref.shape[1]
    chunk_groups = PERM_CHUNK // SEG_ALIGN

    def make_copy(tile_slot):
        def mk(ls, ds, size):
            return pltpu.make_async_copy(y_hbm.at[pl.ds(ds, size)], ybuf.at[tile_slot, pl.ds(ls, size)],
                                         sem.at[tile_slot])
        return mk

    @pl.when(i == 0)
    def _():
        ybuf[...] = jnp.zeros_like(ybuf)
        _segment_starts(cnt_ref, lstart_ref, dst_ref, i, make_copy(slot))

    @pl.when(i < last)
    def _():
        _segment_starts(cnt_ref, lstart_ref, dst_ref, i + 1, make_copy(1 - slot))

    def weights(j):
        c_iota = lax.broadcasted_iota(jnp.int32, (tt, PERM_CHUNK), 1) + j * PERM_CHUNK
        w = jnp.zeros((tt, PERM_CHUNK), F32)
        for k in range(TOP_K):
            w = jnp.where(c_iota == lrow_ref[:, k:k + 1], gate_ref[:, k:k + 1], w)
        return w.astype(BF16)

    def gather(j, w):
        y_chunk = ybuf[slot, j * chunk_groups:(j + 1) * chunk_groups].reshape(PERM_CHUNK, d)
        return jnp.dot(w, y_chunk, preferred_element_type=F32)

    def finish(f):
        o_ref[...] = _layer_norm(DN_ALPHA * x1_ref[...] + mod_ref[0, 5:6, :] * f, g_ref[...], b_ref[...])

    n_chunks = rows // PERM_CHUNK
    w_next = weights(0)
    _segment_waits(total_ref, i, make_copy(slot))
    f = None
    for j in range(n_chunks - 1):
        w_cur = w_next
        if j + 2 < n_chunks:
            w_next = weights(j + 1)
        part = gather(j, w_cur)
        f = part if f is None else f + part

    need_last = total_ref[i] > (n_chunks - 1) * chunk_groups

    @pl.when(need_last)
    def _():
        finish(f + gather(n_chunks - 1, weights(n_chunks - 1)))

    @pl.when(jnp.logical_not(need_last))
    def _():
        finish(f)


def _combine(cnt, lstart, dst, total, lrow, gates, x1, mod, g, b, y_sorted, tokens_per_sample):
    n, d = x1.shape
    tt = TOKEN_TILE
    tiles_per_sample = tokens_per_sample // tt
    const2 = lambda i, *_: (0, 0)
    tok = lambda i, *_: (i, 0)
    grid_spec = pltpu.PrefetchScalarGridSpec(
        num_scalar_prefetch=4,
        grid=(n // tt,),
        in_specs=[pl.BlockSpec((tt, TOP_K), tok),
                  pl.BlockSpec((tt, TOP_K), tok),
                  pl.BlockSpec((tt, d), tok),
                  pl.BlockSpec((1, 6, d), lambda i, *_: (i // tiles_per_sample, 0, 0)),
                  pl.BlockSpec((1, d), const2),
                  pl.BlockSpec((1, d), const2),
                  pl.BlockSpec(memory_space=pl.ANY)],
        out_specs=pl.BlockSpec((tt, d), tok),
        scratch_shapes=[pltpu.VMEM((2, TILE_SORTED_ROWS // SEG_ALIGN, SEG_ALIGN, d), BF16),
                        pltpu.SemaphoreType.DMA((2,))])
    return pl.pallas_call(
        _combine_kernel,
        grid_spec=grid_spec,
        out_shape=jax.ShapeDtypeStruct((n, d), F32),
        compiler_params=_cparams("arbitrary"),
        name="combine",
    )(cnt, lstart, dst, total, lrow, gates, x1, mod, g, b, y_sorted)


def _rope_tables(t):
    rows = jnp.repeat(jnp.arange(t // GRID_W, dtype=jnp.int32), GRID_W).astype(F32)
    cols = jnp.tile(jnp.arange(GRID_W, dtype=jnp.int32), t // GRID_W).astype(F32)
    n_freq = HEAD_DIM // 4
    inv = ROPE_BASE ** (-jnp.arange(n_freq, dtype=F32) / n_freq)
    ang_r = rows[:, None] * inv
    ang_c = cols[:, None] * inv
    ang = jnp.concatenate([ang_r, ang_r, ang_c, ang_c], -1)
    ang = jnp.concatenate([ang, ang], -1)
    cos, sin = jnp.cos(ang), jnp.sin(ang)
    first_half = (jnp.arange(LANES) % 32) < 16
    return cos, jnp.where(first_half, -sin, 0.0), jnp.where(first_half, 0.0, sin)


def kernel(x, c, ctx, c_ctx, ln_in_g, ln_in_b, w_ada, b_ada, w_in, ret_log_decay_f, ret_log_decay_b,
           ret_gn_g, ret_gn_b, q_norm_g, k_norm_g, w_o, ln1_g, ln1_b, router_w, router_b,
           w_up, b_up, w_down, b_down, ln2_g, ln2_b):
    b, t, d = x.shape
    n = b * t
    row = lambda v: v.reshape(1, -1)

    cc = jnp.concatenate([c, c_ctx[None]], 0)
    cc = jnp.pad(cc, ((0, (-cc.shape[0]) % 8), (0, 0)))
    mod_all = _ada_mod(cc, w_ada[0], row(b_ada[0]))
    mod = mod_all[:b].reshape(b, 6, d)
    mod_ctx = mod_all[b:b + 1].reshape(1, 6, d)

    w_in_bf = w_in[0].astype(BF16)
    lane = jnp.arange(LANES)
    seg = (lane[:, None] // HEAD_DIM == lane[None, :] // HEAD_DIM).astype(BF16)
    qg = jnp.tile(q_norm_g[0], ATT_HEADS).reshape(1, -1)
    kg = jnp.tile(k_norm_g[0], ATT_KV_HEADS).reshape(1, -1)
    lng, lnb = row(ln_in_g), row(ln_in_b)

    rq, rkt, rv, rg, aq, akt, av = _in_proj(x, mod, lng, lnb, w_in_bf, seg, qg, kg, _rope_tables(t))
    _, rktc, rvc, _, _, aktc, avc = _in_proj(ctx, mod_ctx, lng, lnb, w_in_bf, seg, qg, kg, None)

    ret = _retention(row(ret_log_decay_f[0]), row(ret_log_decay_b[0]), rq, rkt, rv, rg, rktc, rvc,
                     row(ret_gn_g[0]), row(ret_gn_b[0]))
    tk = ctx.shape[1] + t
    av_ones = jnp.concatenate([jnp.concatenate([avc, av], 1), jnp.ones((b, tk, LANES), BF16)], 2)
    att = _attention(aq, jnp.concatenate([aktc, akt], 2), av_ones.astype(jnp.float8_e4m3fn))

    rw = jnp.pad(router_w[0], ((0, 0), (0, LANES - N_EXPERTS)))
    rb = jnp.pad(row(router_b[0]), ((0, 0), (0, LANES - N_EXPERTS)))
    rw_hi = rw.astype(BF16)
    rw_lo = (rw - rw_hi.astype(F32)).astype(BF16)
    tt = TOKEN_TILE
    n_tiles = n // tt
    before = jnp.triu(jnp.ones((tt, tt), BF16), 1)
    below = jnp.tril(jnp.ones((N_EXPERTS, N_EXPERTS), BF16), -1)
    x1, h2, gates_t, lrow_t, cnt, lstart, tbase, phi, rem = _post_mixer(
        ret, att, x, mod, lng, lnb, w_o[0].astype(BF16), row(ln1_g[0]), row(ln1_b[0]),
        rw_hi, rw_lo, rb, before, below)
    token_major = lambda v: v.transpose(0, 2, 1).reshape(n, TOP_K)
    gates, lrow = token_major(gates_t), token_major(lrow_t)

    tm = EXPERT_ROWS
    cnt, lstart, tbase, phi, rem = (v[:, :, 0] for v in (cnt, lstart, tbase, phi, rem))
    totals = jnp.max(tbase + cnt, axis=0)
    exp_rows = (totals + tm - 1) // tm * tm
    exp_end = jnp.cumsum(exp_rows)
    exp_start = exp_end - exp_rows
    dst = (exp_start[None, :] + tbase).reshape(-1)
    max_rows = n * TOP_K + N_EXPERTS * (SEG_ALIGN - 1) + N_EXPERTS * (tm - 1)
    n_blocks = -(-max_rows // tm)
    block_start = jnp.arange(n_blocks, dtype=jnp.int32) * tm
    block_expert = jnp.minimum(jnp.sum(exp_end[None, :] <= block_start[:, None], axis=1),
                               N_EXPERTS - 1).astype(jnp.int32)
    n_active = (exp_end[-1:] // tm).astype(jnp.int32)
    groups = lambda v: (v // SEG_ALIGN).astype(jnp.int32).reshape(-1)
    cnt_g, lstart_g, dst_g, total_g = groups(cnt), groups(lstart), groups(dst), groups(jnp.sum(cnt, axis=1))

    flat = lambda v: v.astype(jnp.int32).reshape(-1)
    ncopy = cnt // SEG_ALIGN - (rem > 0)
    real_rows = jnp.where(cnt > 0, cnt - phi - jnp.where(rem > 0, SEG_ALIGN - rem, 0), 0)
    real_totals = jnp.sum(real_rows, axis=0)
    flush = (real_totals % SEG_ALIGN) > 0
    flush_dst = (exp_start + real_totals // SEG_ALIGN * SEG_ALIGN) // SEG_ALIGN
    x_sorted = _dispatch(cnt_g, flat(ncopy), lstart_g, dst_g, flat(jnp.sum(ncopy, axis=1)), total_g, flat(phi),
                         flat(rem), flat(flush), flat(flush_dst), groups(exp_rows - totals),
                         groups(exp_start + totals), n_active, h2, lrow_t, n_blocks * tm)
    y_sorted = _experts(block_expert, n_active, x_sorted, w_up[0], b_up[0], w_down[0], b_down[0])
    out = _combine(cnt_g, lstart_g, dst_g, total_g, lrow, gates, x1, mod, row(ln2_g[0]), row(ln2_b[0]),
                   y_sorted, t)
    return out.reshape(b, t, d)
```

```python
import functools

import jax
import jax.numpy as jnp
from jax import lax
from jax.experimental import pallas as pl
from jax.experimental.pallas import tpu as pltpu

GRID_W = 64
HEAD_DIM = 64
ROPE_BASE = 10000.0
LN_EPS = 1e-6
RET_HEADS = 4
RET_DK = 64
RET_DV = 128
RET_CHUNK = 128
ATT_HEADS = 8
ATT_KV_HEADS = 2
ATT_GROUP = ATT_HEADS // ATT_KV_HEADS
N_EXPERTS = 32
TOP_K = 4
SWIGLU_LIMIT = 7.0
SWIGLU_ALPHA = 1.702
DEPTH = 1
DN_ALPHA = (2.0 * DEPTH) ** 0.25
LOG2_E = 1.4426950408889634

RET_Q_W = RET_HEADS * RET_DK
RET_V_W = RET_HEADS * RET_DV
ATT_Q_W = ATT_HEADS * HEAD_DIM
ATT_K_W = ATT_KV_HEADS * HEAD_DIM

LANES = 128
VMEM_LIMIT_BYTES = 56 * 1024 * 1024

TOKEN_TILE = 512
ATT_Q_TILE = 512
ATT_Q_SUB = 256
ATT_P_SHIFT = 8.0
EXPERT_ROWS = 512
SEG_ALIGN = 16
TILE_SORTED_ROWS = 3072
PERM_CHUNK = 512

F32 = jnp.float32
BF16 = jnp.bfloat16


def _cparams(*sem):
    return pltpu.CompilerParams(dimension_semantics=sem, vmem_limit_bytes=VMEM_LIMIT_BYTES)


def _layer_norm(x, g, b):
    mu = jnp.mean(x, -1, keepdims=True)
    xc = x - mu
    var = jnp.mean(xc * xc, -1, keepdims=True)
    return xc * lax.rsqrt(var + LN_EPS) * g + b


def _silu(x):
    return x * (1.0 / (1.0 + jnp.exp(-x)))


def _ada_kernel(c_ref, w_ref, b_ref, o_ref):
    a = _silu(c_ref[...])
    o_ref[...] = jnp.dot(a, w_ref[...], preferred_element_type=F32,
                         precision=lax.Precision.HIGHEST) + b_ref[...]


def _ada_mod(cc, w, b):
    m, d = cc.shape
    n = w.shape[1]
    tn = 1536
    return pl.pallas_call(
        _ada_kernel,
        grid=(n // tn,),
        in_specs=[pl.BlockSpec((m, d), lambda j: (0, 0)),
                  pl.BlockSpec((d, tn), lambda j: (0, j)),
                  pl.BlockSpec((1, tn), lambda j: (0, j))],
        out_specs=pl.BlockSpec((m, tn), lambda j: (0, j)),
        out_shape=jax.ShapeDtypeStruct((m, n), F32),
        compiler_params=_cparams("arbitrary"),
        name="ada_mod",
    )(cc, w, b)


def _rope(x, cos, sinn, sinp):
    outs = []
    for j in range(x.shape[1] // LANES):
        xg = x[:, j * LANES:(j + 1) * LANES]
        outs.append(xg * cos + pltpu.roll(xg, LANES - 16, 1) * sinn + pltpu.roll(xg, 16, 1) * sinp)
    return outs[0] if len(outs) == 1 else jnp.concatenate(outs, axis=1)


def _head_rms(x, seg, gain):
    outs = []
    for j in range(x.shape[1] // LANES):
        xg = x[:, j * LANES:(j + 1) * LANES]
        sq = xg * xg
        hi = sq.astype(BF16)
        lo = (sq - hi.astype(F32)).astype(BF16)
        ms = (jnp.dot(hi, seg, preferred_element_type=F32)
              + jnp.dot(lo, seg, preferred_element_type=F32)) * (1.0 / HEAD_DIM)
        outs.append(xg * lax.rsqrt(ms + LN_EPS) * gain[:, j * LANES:(j + 1) * LANES])
    return outs[0] if len(outs) == 1 else jnp.concatenate(outs, axis=1)


def _in_proj_kernel(*refs, rope):
    if rope:
        (x_ref, mod_ref, lng_ref, lnb_ref, w_ref, seg_ref, qg_ref, kg_ref, cos_ref, sinn_ref, sinp_ref,
         rq_ref, rkt_ref, rv_ref, rg_ref, aq_ref, akt_ref, av_ref) = refs
    else:
        (x_ref, mod_ref, lng_ref, lnb_ref, w_ref, seg_ref, qg_ref, kg_ref,
         rq_ref, rkt_ref, rv_ref, rg_ref, aq_ref, akt_ref, av_ref) = refs
    x = x_ref[0]
    xn = _layer_norm(x, lng_ref[...], lnb_ref[...])
    h = (xn * (1.0 + mod_ref[0, 1:2, :]) + mod_ref[0, 0:1, :]).astype(BF16)
    seg = seg_ref[...]

    def rotate(v):
        return _rope(v, cos_ref[...], sinn_ref[...], sinp_ref[...]) if rope else v

    o_rv = 2 * RET_Q_W
    o_aq = o_rv + 2 * RET_V_W
    o_ak = o_aq + ATT_Q_W
    o_end = o_ak + 2 * ATT_K_W

    def project(lo, hi):
        return jnp.dot(h, w_ref[:, lo:hi], preferred_element_type=F32)

    def finish_aq(u):
        aq = rotate(_head_rms(u, seg, qg_ref[...]))
        aq_ref[0] = (aq * (HEAD_DIM ** -0.5 * LOG2_E)).astype(BF16)

    def finish_akv(u):
        ak = rotate(_head_rms(u[:, :ATT_K_W], seg, kg_ref[...]))
        akt_ref[0] = ak.T.astype(BF16)
        av_ref[0] = u[:, ATT_K_W:].astype(BF16)

    def finish_rqk(u):
        rq_ref[0] = rotate(u[:, :RET_Q_W])
        rkt_ref[0] = (rotate(u[:, RET_Q_W:]) * (RET_DK ** -0.5)).T

    def finish_rvg(u):
        rv_ref[0] = u[:, :RET_V_W].astype(BF16)
        rg_ref[0] = u[:, RET_V_W:].astype(BF16)

    stages = [((o_aq, o_ak), finish_aq), ((o_ak, o_end), finish_akv),
              ((0, o_rv), finish_rqk), ((o_rv, o_aq), finish_rvg)]
    u_next = project(*stages[0][0])
    for idx, (_, finish) in enumerate(stages):
        u_cur = u_next
        if idx + 1 < len(stages):
            u_next = project(*stages[idx + 1][0])
        finish(u_cur)


def _in_proj(x, mod, lng, lnb, w_bf, seg, qg, kg, tables):
    b, t, d = x.shape
    tt = min(TOKEN_TILE, t)
    rope = tables is not None
    mod_b = mod.shape[0]
    mod_map = (lambda bi, i: (bi, 0, 0)) if mod_b > 1 else (lambda bi, i: (0, 0, 0))
    const2 = lambda bi, i: (0, 0)
    tok3 = lambda bi, i: (bi, i, 0)
    in_specs = [pl.BlockSpec((1, tt, d), tok3),
                pl.BlockSpec((1, 6, d), mod_map),
                pl.BlockSpec((1, d), const2),
                pl.BlockSpec((1, d), const2),
                pl.BlockSpec(w_bf.shape, const2),
                pl.BlockSpec(seg.shape, const2),
                pl.BlockSpec(qg.shape, const2),
                pl.BlockSpec(kg.shape, const2)]
    args = [x, mod, lng, lnb, w_bf, seg, qg, kg]
    if rope:
        in_specs += [pl.BlockSpec((tt, LANES), lambda bi, i: (i, 0))] * 3
        args += list(tables)
    out_shape = [jax.ShapeDtypeStruct((b, t, RET_Q_W), F32),
                 jax.ShapeDtypeStruct((b, RET_Q_W, t), F32),
                 jax.ShapeDtypeStruct((b, t, RET_V_W), BF16),
                 jax.ShapeDtypeStruct((b, t, RET_V_W), BF16),
                 jax.ShapeDtypeStruct((b, t, ATT_Q_W), BF16),
                 jax.ShapeDtypeStruct((b, ATT_K_W, t), BF16),
                 jax.ShapeDtypeStruct((b, t, ATT_K_W), BF16)]
    out_specs = [pl.BlockSpec((1, tt, RET_Q_W), tok3),
                 pl.BlockSpec((1, RET_Q_W, tt), lambda bi, i: (bi, 0, i)),
                 pl.BlockSpec((1, tt, RET_V_W), tok3),
                 pl.BlockSpec((1, tt, RET_V_W), tok3),
                 pl.BlockSpec((1, tt, ATT_Q_W), tok3),
                 pl.BlockSpec((1, ATT_K_W, tt), lambda bi, i: (bi, 0, i)),
                 pl.BlockSpec((1, tt, ATT_K_W), tok3)]
    return pl.pallas_call(
        functools.partial(_in_proj_kernel, rope=rope),
        grid=(b, t // tt),
        in_specs=in_specs,
        out_specs=out_specs,
        out_shape=out_shape,
        compiler_params=_cparams("parallel", "parallel"),
        name="in_proj_rope" if rope else "in_proj_ctx",
    )(*args)


def _retention_kernel(lgf_ref, lgb_ref, q_ref, kt_ref, v_ref, g_ref, ktc_ref, vc_ref, gng_ref, gnb_ref,
                      o_ref, kv_ref, rhs_ref):
    pair = pl.program_id(1)
    t = q_ref.shape[1]
    n_chunks = t // RET_CHUNK
    ctx_len = ktc_ref.shape[2]
    c = RET_CHUNK
    dk = RET_DK
    pos_col = lax.broadcasted_iota(jnp.int32, (c, 1), 0).astype(F32)
    pos_row = lax.broadcasted_iota(jnp.int32, (1, c), 1).astype(F32)
    row = lax.broadcasted_iota(jnp.int32, (c, c), 0)
    col = lax.broadcasted_iota(jnp.int32, (c, c), 1)
    diff = (row - col).astype(F32)
    lane = lax.broadcasted_iota(jnp.int32, (c, 2 * dk), 1)
    cpos = lax.broadcasted_iota(jnp.int32, (1, ctx_len), 1).astype(F32)

    heads = []
    for hh in range(2):
        head = pair * 2 + hh
        lgf = lgf_ref[0, head]
        lgb = lgb_ref[0, head]
        heads.append(dict(
            kl=slice(hh * dk, (hh + 1) * dk),
            vl=slice(hh * RET_DV, (hh + 1) * RET_DV),
            lgf=lgf, lgb=lgb,
            kdf=jnp.exp(lgf * (c - 1.0 - pos_row)),
            kdb=jnp.exp(lgb * pos_row),
            qd=jnp.where(lane < dk, jnp.exp(lgf * (pos_col + 1.0)), jnp.exp(lgb * (c - pos_col))),
            decay=(jnp.where(diff >= 0, jnp.exp(lgf * jnp.maximum(diff, 0.0)), 0.0)
                   + jnp.where(diff <= 0, jnp.exp(lgb * jnp.maximum(-diff, 0.0)), 0.0))))

    def kv_body(n, carry):
        r = pl.ds(pl.multiple_of(n * c, c), c)
        for hh, hd in enumerate(heads):
            kt = kt_ref[0, hd["kl"], r]
            vc = v_ref[0, r, hd["vl"]]
            both = jnp.concatenate([kt * hd["kdf"], kt * hd["kdb"]], axis=0).astype(BF16)
            kv_ref[hh, n] = jnp.dot(both, vc, preferred_element_type=F32)
            rhs_ref[hh, n, 0:c, :] = vc
        return carry

    lax.fori_loop(0, n_chunks, kv_body, 0, unroll=4)

    for hh, hd in enumerate(heads):
        lgf, lgb = hd["lgf"], hd["lgb"]
        ktc = ktc_ref[0, hd["kl"], :]
        both = jnp.concatenate([ktc * jnp.exp(lgf * (ctx_len - 1.0 - cpos)), ktc * jnp.exp(lgb * cpos)],
                               axis=0).astype(BF16)
        s_ctx = jnp.dot(both, vc_ref[0, :, hd["vl"]], preferred_element_type=F32)
        s_f, s_b = s_ctx[0:dk], s_ctx[dk:]
        gfc = jnp.exp(lgf * c)
        gbc = jnp.exp(lgb * c)
        for n in range(n_chunks):
            rhs_ref[hh, n, c:c + dk, :] = s_f.astype(BF16)
            s_f = gfc * s_f + kv_ref[hh, n, 0:dk, :]
        for n in range(n_chunks - 1, -1, -1):
            rhs_ref[hh, n, c + dk:, :] = s_b.astype(BF16)
            s_b = gbc * s_b + kv_ref[hh, n, dk:, :]

    def out_body(n, carry):
        r = pl.ds(pl.multiple_of(n * c, c), c)
        qq = q_ref[0, r, :]
        swapped = pltpu.roll(qq, dk, 1)
        for hh, hd in enumerate(heads):
            vl = hd["vl"]
            s = jnp.dot(q_ref[0, r, hd["kl"]].astype(BF16), kt_ref[0, hd["kl"], r].astype(BF16),
                        preferred_element_type=F32)
            q2 = jnp.where(lane < dk, qq, swapped) if hh == 0 else jnp.where(lane < dk, swapped, qq)
            lhs = jnp.concatenate([(s * hd["decay"]).astype(BF16), (q2 * hd["qd"]).astype(BF16)], axis=1)
            y = jnp.dot(lhs, rhs_ref[hh, n], preferred_element_type=F32)
            mu = jnp.mean(y, -1, keepdims=True)
            yc = y - mu
            var = jnp.mean(yc * yc, -1, keepdims=True)
            yn = yc * lax.rsqrt(var + LN_EPS) * gng_ref[:, vl] + gnb_ref[:, vl]
            o_ref[0, r, vl] = (yn * _silu(g_ref[0, r, vl].astype(F32))).astype(o_ref.dtype)
        return carry

    lax.fori_loop(0, n_chunks, out_body, 0, unroll=8)


def _retention(lgf, lgb, rq, rkt, rv, rg, rktc, rvc, gng, gnb):
    b, t, _ = rq.shape
    ctx_len = rktc.shape[2]
    n_chunks = t // RET_CHUNK
    pair3 = lambda bi, p: (bi, 0, p)
    pair_rows = lambda bi, p: (bi, p, 0)
    smem = pl.BlockSpec(memory_space=pltpu.SMEM)
    return pl.pallas_call(
        _retention_kernel,
        grid=(b, RET_HEADS // 2),
        in_specs=[smem, smem,
                  pl.BlockSpec((1, t, 2 * RET_DK), pair3),
                  pl.BlockSpec((1, 2 * RET_DK, t), pair_rows),
                  pl.BlockSpec((1, t, 2 * RET_DV), pair3),
                  pl.BlockSpec((1, t, 2 * RET_DV), pair3),
                  pl.BlockSpec((1, 2 * RET_DK, ctx_len), pair_rows),
                  pl.BlockSpec((1, ctx_len, 2 * RET_DV), pair3),
                  pl.BlockSpec((1, 2 * RET_DV), lambda bi, p: (0, p)),
                  pl.BlockSpec((1, 2 * RET_DV), lambda bi, p: (0, p))],
        out_specs=pl.BlockSpec((1, t, 2 * RET_DV), pair3),
        out_shape=jax.ShapeDtypeStruct((b, t, RET_V_W), BF16),
        scratch_shapes=[pltpu.VMEM((2, n_chunks, 2 * RET_DK, RET_DV), F32),
                        pltpu.VMEM((2, n_chunks, RET_CHUNK + 2 * RET_DK, RET_DV), BF16)],
        compiler_params=_cparams("parallel", "parallel"),
        name="retention",
    )(lgf, lgb, rq, rkt, rv, rg, rktc, rvc, gng, gnb)


def _attention_kernel(q_ref, kt_ref, v_ref, o_ref, s_buf, p_buf):
    ts = ATT_Q_SUB
    lane = lax.broadcasted_iota(jnp.int32, (ts, LANES), 1)
    pieces = [(sub, kvh) for sub in range(q_ref.shape[1] // ts) for kvh in range(ATT_KV_HEADS)]

    def scores(i):
        sub, kvh = pieces[i]
        q = jnp.concatenate(
            [q_ref[0, sub * ts:(sub + 1) * ts, (kvh * ATT_GROUP + g) * HEAD_DIM:(kvh * ATT_GROUP + g + 1) * HEAD_DIM]
             for g in range(ATT_GROUP)], axis=0)
        s_buf[i % 2] = jnp.dot(q, kt_ref[0, kvh * HEAD_DIM:(kvh + 1) * HEAD_DIM, :], preferred_element_type=F32)

    def numerators(i):
        s = s_buf[i % 2]
        top = jnp.max(s, -1, keepdims=True) - ATT_P_SHIFT
        p_buf[i % 2] = jnp.exp2(s - top).astype(jnp.float8_e4m3fn)

    def values(i):
        sub, kvh = pieces[i]
        ov = jnp.dot(p_buf[i % 2], v_ref[0], preferred_element_type=F32)
        o = ov[:, :LANES] / ov[:, LANES:LANES + 1]
        for pair in range(ATT_GROUP // 2):
            halves = []
            for half in range(2):
                g = pair * 2 + half
                og = o[g * ts:(g + 1) * ts, :]
                if kvh != half:
                    og = pltpu.roll(og, HEAD_DIM, 1)
                halves.append(og)
            col = (kvh * ATT_GROUP // 2 + pair) * LANES
            o_ref[0, sub * ts:(sub + 1) * ts, col:col + LANES] = jnp.where(
                lane < HEAD_DIM, halves[0], halves[1]).astype(o_ref.dtype)

    scores(0)
    scores(1)
    numerators(0)
    for i in range(len(pieces)):
        if i + 2 < len(pieces):
            scores(i + 2)
        if i + 1 < len(pieces):
            numerators(i + 1)
        values(i)


def _attention(aq, akt, av_ones):
    b, t, _ = aq.shape
    tk = akt.shape[2]
    tq = ATT_Q_TILE
    return pl.pallas_call(
        _attention_kernel,
        grid=(b, t // tq),
        in_specs=[pl.BlockSpec((1, tq, ATT_Q_W), lambda bi, i: (bi, i, 0)),
                  pl.BlockSpec((1, ATT_K_W, tk), lambda bi, i: (bi, 0, 0)),
                  pl.BlockSpec((1, tk, av_ones.shape[2]), lambda bi, i: (bi, 0, 0))],
        out_specs=pl.BlockSpec((1, tq, ATT_Q_W), lambda bi, i: (bi, i, 0)),
        out_shape=jax.ShapeDtypeStruct((b, t, ATT_Q_W), BF16),
        scratch_shapes=[pltpu.VMEM((2, ATT_GROUP * ATT_Q_SUB, tk), F32),
                        pltpu.VMEM((2, ATT_GROUP * ATT_Q_SUB, tk), jnp.float8_e4m3fn)],
        compiler_params=_cparams("parallel", "parallel"),
        name="attention",
    )(aq, akt, av_ones)


def _post_mixer_kernel(ret_ref, att_ref, x_ref, mod_ref, lng_ref, lnb_ref, wo_ref, l1g_ref, l1b_ref,
                       rwh_ref, rwl_ref, rb_ref, before_ref, below_ref,
                       x1_ref, h2_ref, gate_ref, lrow_ref, cnt_ref, lstart_ref, tbase_ref, phi_ref, rem_ref,
                       base_acc):
    first = (pl.program_id(0) == 0) & (pl.program_id(1) == 0)

    @pl.when(first)
    def _():
        base_acc[...] = jnp.zeros_like(base_acc)

    half = ret_ref.shape[2]
    y = (jnp.dot(ret_ref[0], wo_ref[0:half, :], preferred_element_type=F32)
         + jnp.dot(att_ref[0], wo_ref[half:, :], preferred_element_type=F32))
    xn = _layer_norm(x_ref[0], lng_ref[...], lnb_ref[...])
    x1 = _layer_norm(DN_ALPHA * xn + mod_ref[0, 2:3, :] * y, l1g_ref[...], l1b_ref[...])
    x1_ref[...] = x1
    h2 = x1 * (1.0 + mod_ref[0, 4:5, :]) + mod_ref[0, 3:4, :]
    h_hi = h2.astype(BF16)
    h2_ref[...] = h_hi

    h_lo = (h2 - h_hi.astype(F32)).astype(BF16)
    logits = (jnp.dot(h_hi, rwh_ref[...], preferred_element_type=F32)
              + jnp.dot(h_lo, rwh_ref[...], preferred_element_type=F32)
              + jnp.dot(h_hi, rwl_ref[...], preferred_element_type=F32)) + rb_ref[...]

    tt = logits.shape[0]
    work = logits.T[0:N_EXPERTS, :]
    e_iota = lax.broadcasted_iota(jnp.int32, (N_EXPERTS, tt), 0)
    sels, vals = [], []
    onehot = jnp.zeros((N_EXPERTS, tt), F32)
    for _ in range(TOP_K):
        m = jnp.max(work, 0, keepdims=True)
        idx = jnp.min(jnp.where(work == m, e_iota, N_EXPERTS), 0, keepdims=True)
        sel = e_iota == idx
        sels.append(sel)
        vals.append(m)
        onehot = onehot + sel.astype(F32)
        work = jnp.where(sel, -jnp.inf, work)
    exps = [jnp.exp(v - vals[0]) for v in vals]
    denom = exps[0] + exps[1] + exps[2] + exps[3]

    count = jnp.sum(onehot, 1, keepdims=True)
    base = base_acc[...]
    phi = base - SEG_ALIGN * jnp.floor(base * (1.0 / SEG_ALIGN))
    units = jnp.where(count > 0, jnp.floor((phi + count + (SEG_ALIGN - 1.0)) * (1.0 / SEG_ALIGN)), 0.0)
    lstart = SEG_ALIGN * jnp.dot(below_ref[...], jnp.broadcast_to(units, (N_EXPERTS, LANES)).astype(BF16),
                                 preferred_element_type=F32)[:, 0:1]
    rank = jnp.dot(onehot.astype(BF16), before_ref[...], preferred_element_type=F32) + (lstart + phi)
    end = phi + count

    k_iota = lax.broadcasted_iota(jnp.int32, (TOP_K, tt), 0)
    gate_out = jnp.zeros((TOP_K, tt), F32)
    lrow_out = jnp.zeros((TOP_K, tt), F32)
    for k in range(TOP_K):
        pk = jnp.sum(jnp.where(sels[k], rank, 0.0), 0, keepdims=True)
        gate_out = jnp.where(k_iota == k, exps[k] / denom, gate_out)
        lrow_out = jnp.where(k_iota == k, pk, lrow_out)
    gate_ref[0] = gate_out
    lrow_ref[0] = lrow_out.astype(jnp.int32)
    cnt_ref[0] = (units * SEG_ALIGN).astype(jnp.int32)
    lstart_ref[0] = lstart.astype(jnp.int32)
    tbase_ref[0] = (base - phi).astype(jnp.int32)
    phi_ref[0] = jnp.where(count > 0, phi, 0.0).astype(jnp.int32)
    rem_ref[0] = jnp.where(count > 0, end - SEG_ALIGN * jnp.floor(end * (1.0 / SEG_ALIGN)), 0.0).astype(jnp.int32)
    base_acc[...] = base + count


def _post_mixer(ret, att, x, mod, lng, lnb, wo_bf, l1g, l1b, rw_hi, rw_lo, rb, before, below):
    b, t, d = x.shape
    tt = TOKEN_TILE
    nt = t // tt
    n = b * t
    const2 = lambda bi, i: (0, 0)
    tok3 = lambda bi, i: (bi, i, 0)
    flat = lambda bi, i: (bi * nt + i, 0)
    tile3 = lambda bi, i: (bi * nt + i, 0, 0)
    vec = pl.BlockSpec((1, d), const2)
    table = jax.ShapeDtypeStruct((b * nt, N_EXPERTS, 1), jnp.int32)
    table_spec = pl.BlockSpec((1, N_EXPERTS, 1), tile3)
    per_k = pl.BlockSpec((1, TOP_K, tt), tile3)
    return pl.pallas_call(
        _post_mixer_kernel,
        grid=(b, nt),
        in_specs=[pl.BlockSpec((1, tt, ret.shape[2]), tok3),
                  pl.BlockSpec((1, tt, att.shape[2]), tok3),
                  pl.BlockSpec((1, tt, d), tok3),
                  pl.BlockSpec((1, 6, d), lambda bi, i: (bi, 0, 0)),
                  vec, vec,
                  pl.BlockSpec(wo_bf.shape, const2),
                  vec, vec,
                  pl.BlockSpec(rw_hi.shape, const2),
                  pl.BlockSpec(rw_lo.shape, const2),
                  pl.BlockSpec(rb.shape, const2),
                  pl.BlockSpec((tt, tt), const2),
                  pl.BlockSpec((N_EXPERTS, N_EXPERTS), const2)],
        out_specs=[pl.BlockSpec((tt, d), flat),
                   pl.BlockSpec((tt, d), flat),
                   per_k, per_k, table_spec, table_spec, table_spec, table_spec, table_spec],
        out_shape=[jax.ShapeDtypeStruct((n, d), F32),
                   jax.ShapeDtypeStruct((n, d), BF16),
                   jax.ShapeDtypeStruct((b * nt, TOP_K, tt), F32),
                   jax.ShapeDtypeStruct((b * nt, TOP_K, tt), jnp.int32),
                   table, table, table, table, table],
        scratch_shapes=[pltpu.VMEM((N_EXPERTS, 1), F32)],
        compiler_params=_cparams("arbitrary", "arbitrary"),
        name="post_mixer",
    )(ret, att, x, mod, lng, lnb, wo_bf, l1g, l1b, rw_hi, rw_lo, rb, before, below)


def _segment_starts(cnt_ref, lstart_ref, dst_ref, tile, make_copy):
    def body(e, carry):
        t = tile * N_EXPERTS + e
        n = cnt_ref[t]

        @pl.when(n > 0)
        def _():
            make_copy(lstart_ref[t], dst_ref[t], n).start()
        return carry
    lax.fori_loop(0, N_EXPERTS, body, 0, unroll=4)


def _segment_waits(total_ref, tile, make_copy):
    make_copy(0, 0, total_ref[tile]).wait()


def _zero_fill_copies(tail_len_ref, tail_dst_ref, nact_ref, zbuf, x_hbm, zsem, act):
    block = zbuf.shape[0]

    def tail_body(e, carry):
        n = tail_len_ref[e]

        @pl.when(n > 0)
        def _():
            copy = pltpu.make_async_copy(zbuf.at[pl.ds(0, n)], x_hbm.at[pl.ds(tail_dst_ref[e], n)], zsem)
            getattr(copy, act)()
        return carry
    lax.fori_loop(0, N_EXPERTS, tail_body, 0)

    def block_body(j, carry):
        copy = pltpu.make_async_copy(zbuf, x_hbm.at[pl.ds(j * block, block)], zsem)
        getattr(copy, act)()
        return carry
    lax.fori_loop(nact_ref[0], x_hbm.shape[0] // block, block_body, 0)


def _dispatch_kernel(cnt_ref, ncopy_ref, lstart_ref, dst_ref, total_ref, extent_ref, phi_ref, rem_ref, flush_ref,
                     flush_dst_ref,
                     tail_len_ref, tail_dst_ref, nact_ref,
                     h_ref, lrow_ref, x_hbm, sbuf, sem, zbuf, zsem, carry):
    i = pl.program_id(0)
    last = pl.num_programs(0) - 1
    slot = i % 2
    rows = sbuf.shape[1] * SEG_ALIGN
    tt = h_ref.shape[0]
    d = h_ref.shape[1]
    chunk_groups = PERM_CHUNK // SEG_ALIGN

    def make_copy(tile_slot):
        def mk(ls, ds, size):
            return pltpu.make_async_copy(sbuf.at[tile_slot, pl.ds(ls, size)], x_hbm.at[pl.ds(ds, size)],
                                         sem.at[tile_slot])
        return mk

    @pl.when(i == 0)
    def _():
        zbuf[...] = jnp.zeros_like(zbuf)
        carry[...] = jnp.zeros_like(carry)
        _zero_fill_copies(tail_len_ref, tail_dst_ref, nact_ref, zbuf, x_hbm, zsem, "start")

    def perm(j):
        r_iota = lax.broadcasted_iota(jnp.int32, (PERM_CHUNK, tt), 0) + j * PERM_CHUNK
        hit = r_iota == lrow_ref[0, 0:1, :]
        for k in range(1, TOP_K):
            hit = hit | (r_iota == lrow_ref[0, k:k + 1, :])
        return jnp.where(hit, 1.0, 0.0).astype(BF16)

    def place(j, p):
        sbuf[slot, j * chunk_groups:(j + 1) * chunk_groups] = jnp.dot(
            p, h_ref[...], preferred_element_type=F32).astype(BF16).reshape(chunk_groups, SEG_ALIGN, d)

    n_chunks = rows // PERM_CHUNK
    p_next = perm(0)
    for j in range(n_chunks - 1):
        p_cur = p_next
        if j + 2 < n_chunks:
            p_next = perm(j + 1)
        place(j, p_cur)

    @pl.when(extent_ref[i] > (n_chunks - 1) * chunk_groups)
    def _():
        place(n_chunks - 1, perm(n_chunks - 1))
    def merge_body(e, c):
        t = i * N_EXPERTS + e
        n = cnt_ref[t]
        first = lstart_ref[t]
        present = n > 0
        old = carry[e]
        sbuf[slot, first] = sbuf[slot, first] + jnp.where(present, old, jnp.zeros_like(old))
        tail = sbuf[slot, jnp.maximum(first + n - 1, 0)]
        carry[e] = jnp.where(present, jnp.where(rem_ref[t] > 0, tail, jnp.zeros_like(tail)), old)
        return c
    lax.fori_loop(0, N_EXPERTS, merge_body, 0, unroll=4)

    _segment_starts(ncopy_ref, lstart_ref, dst_ref, i, make_copy(slot))

    @pl.when(i > 0)
    def _():
        _segment_waits(total_ref, i - 1, make_copy(1 - slot))

    def flush_copies(act):
        def body(e, c):
            @pl.when(flush_ref[e] > 0)
            def _():
                copy = pltpu.make_async_copy(carry.at[pl.ds(e, 1)], x_hbm.at[pl.ds(flush_dst_ref[e], 1)], zsem)
                getattr(copy, act)()
            return c
        lax.fori_loop(0, N_EXPERTS, body, 0)

    @pl.when(i == last)
    def _():
        flush_copies("start")
        _segment_waits(total_ref, i, make_copy(slot))
        _zero_fill_copies(tail_len_ref, tail_dst_ref, nact_ref, zbuf, x_hbm, zsem, "wait")
        flush_copies("wait")


def _dispatch(cnt, ncopy, lstart, dst, total, extent, phi, rem, flush, flush_dst, tail_len, tail_dst, n_active,
              h2, lrow_t, n_rows):
    n, d = h2.shape
    tt = TOKEN_TILE
    grid_spec = pltpu.PrefetchScalarGridSpec(
        num_scalar_prefetch=13,
        grid=(n // tt,),
        in_specs=[pl.BlockSpec((tt, d), lambda i, *_: (i, 0)),
                  pl.BlockSpec((1, TOP_K, tt), lambda i, *_: (i, 0, 0))],
        out_specs=pl.BlockSpec(memory_space=pl.ANY),
        scratch_shapes=[pltpu.VMEM((2, TILE_SORTED_ROWS // SEG_ALIGN, SEG_ALIGN, d), BF16),
                        pltpu.SemaphoreType.DMA((2,)),
                        pltpu.VMEM((EXPERT_ROWS // SEG_ALIGN, SEG_ALIGN, d), BF16),
                        pltpu.SemaphoreType.DMA(()),
                        pltpu.VMEM((N_EXPERTS, SEG_ALIGN, d), BF16)])
    return pl.pallas_call(
        _dispatch_kernel,
        grid_spec=grid_spec,
        out_shape=jax.ShapeDtypeStruct((n_rows // SEG_ALIGN, SEG_ALIGN, d), BF16),
        compiler_params=_cparams("arbitrary"),
        name="dispatch",
    )(cnt, ncopy, lstart, dst, total, extent, phi, rem, flush, flush_dst, tail_len, tail_dst, n_active, h2, lrow_t)


def _experts_kernel(be_ref, nact_ref, x_ref, wup_ref, bup_ref, wdn_ref, bdn_ref, y_ref,
                    wup_bf, wdn_bf):
    j = pl.program_id(0)
    dff = wdn_ref.shape[1]

    @pl.when(j < nact_ref[0])
    def _():
        prev = be_ref[jnp.maximum(j - 1, 0)]

        @pl.when((j == 0) | (be_ref[j] != prev))
        def _():
            wup_bf[...] = wup_ref[0].astype(BF16)
            wdn_bf[...] = wdn_ref[0].astype(BF16)

        x = x_ref[...].reshape(-1, x_ref.shape[2])
        u = jnp.dot(x, wup_bf[...], preferred_element_type=F32) + bup_ref[0]
        glu = jnp.minimum(u[:, :dff], SWIGLU_LIMIT)
        lin = jnp.clip(u[:, dff:], -SWIGLU_LIMIT, SWIGLU_LIMIT)
        act = glu * (1.0 / (1.0 + jnp.exp(-SWIGLU_ALPHA * glu))) * (lin + 1.0)
        y = jnp.dot(act.astype(BF16), wdn_bf[...], preferred_element_type=F32) + bdn_ref[0]
        y_ref[...] = y.astype(y_ref.dtype).reshape(y_ref.shape)

    @pl.when(j >= nact_ref[0])
    def _():
        y_ref[...] = jnp.zeros_like(y_ref)


def _experts(block_expert, n_active, x_sorted, w_up, b_up, w_down, b_down):
    n_blocks = block_expert.shape[0]
    tm = EXPERT_ROWS
    d = x_sorted.shape[2]
    e, _, up_w = w_up.shape
    dff = w_down.shape[1]
    by_expert = lambda j, be, na: (be[j], 0, 0)
    x_map = lambda j, be, na: (jnp.minimum(j, na[0] - 1), 0, 0)
    rows3 = (tm // SEG_ALIGN, SEG_ALIGN, d)
    grid_spec = pltpu.PrefetchScalarGridSpec(
        num_scalar_prefetch=2,
        grid=(n_blocks,),
        in_specs=[pl.BlockSpec(rows3, x_map),
                  pl.BlockSpec((1, d, up_w), by_expert),
                  pl.BlockSpec((1, 1, up_w), by_expert),
                  pl.BlockSpec((1, dff, d), by_expert),
                  pl.BlockSpec((1, 1, d), by_expert)],
        out_specs=pl.BlockSpec(rows3, lambda j, be, na: (j, 0, 0)),
        scratch_shapes=[pltpu.VMEM((d, up_w), BF16),
                        pltpu.VMEM((dff, d), BF16)])
    return pl.pallas_call(
        _experts_kernel,
        grid_spec=grid_spec,
        out_shape=jax.ShapeDtypeStruct(x_sorted.shape, BF16),
        compiler_params=_cparams("arbitrary"),
        name="experts",
    )(block_expert, n_active, x_sorted, w_up, b_up.reshape(e, 1, up_w), w_down,
      b_down.reshape(e, 1, d))


def _combine_kernel(cnt_ref, lstart_ref, dst_ref, total_ref, lrow_ref, gate_ref, x1_ref, mod_ref, g_ref, b_ref,
                    y_hbm, o_ref, ybuf, sem):
    i = pl.program_id(0)
    last = pl.num_programs(0) - 1
    slot = i % 2
    rows = ybuf.shape[1] * SEG_ALIGN
    tt = x1_ref.shape[0]
    d = x1_ref.shape[1]
    chunk_groups = PERM_CHUNK // SEG_ALIGN

    def make_copy(tile_slot):
        def mk(ls, ds, size):
            return pltpu.make_async_copy(y_hbm.at[pl.ds(ds, size)], ybuf.at[tile_slot, pl.ds(ls, size)],
                                         sem.at[tile_slot])
        return mk

    @pl.when(i == 0)
    def _():
        ybuf[...] = jnp.zeros_like(ybuf)
        _segment_starts(cnt_ref, lstart_ref, dst_ref, i, make_copy(slot))

    @pl.when(i < last)
    def _():
        _segment_starts(cnt_ref, lstart_ref, dst_ref, i + 1, make_copy(1 - slot))

    def weights(j):
        c_iota = lax.broadcasted_iota(jnp.int32, (tt, PERM_CHUNK), 1) + j * PERM_CHUNK
        w = jnp.zeros((tt, PERM_CHUNK), F32)
        for k in range(TOP_K):
            w = jnp.where(c_iota == lrow_ref[:, k:k + 1], gate_ref[:, k:k + 1], w)
        return w.astype(BF16)

    def gather(j, w):
        y_chunk = ybuf[slot, j * chunk_groups:(j + 1) * chunk_groups].reshape(PERM_CHUNK, d)
        return jnp.dot(w, y_chunk, preferred_element_type=F32)

    def finish(f):
        o_ref[...] = _layer_norm(DN_ALPHA * x1_ref[...] + mod_ref[0, 5:6, :] * f, g_ref[...], b_ref[...])

    n_chunks = rows // PERM_CHUNK
    w_next = weights(0)
    _segment_waits(total_ref, i, make_copy(slot))
    f = None
    for j in range(n_chunks - 1):
        w_cur = w_next
        if j + 2 < n_chunks:
            w_next = weights(j + 1)
        part = gather(j, w_cur)
        f = part if f is None else f + part

    need_last = total_ref[i] > (n_chunks - 1) * chunk_groups

    @pl.when(need_last)
    def _():
        finish(f + gather(n_chunks - 1, weights(n_chunks - 1)))

    @pl.when(jnp.logical_not(need_last))
    def _():
        finish(f)


def _combine(cnt, lstart, dst, total, lrow, gates, x1, mod, g, b, y_sorted, tokens_per_sample):
    n, d = x1.shape
    tt = TOKEN_TILE
    tiles_per_sample = tokens_per_sample // tt
    const2 = lambda i, *_: (0, 0)
    tok = lambda i, *_: (i, 0)
    grid_spec = pltpu.PrefetchScalarGridSpec(
        num_scalar_prefetch=4,
        grid=(n // tt,),
        in_specs=[pl.BlockSpec((tt, TOP_K), tok),
                  pl.BlockSpec((tt, TOP_K), tok),
                  pl.BlockSpec((tt, d), tok),
                  pl.BlockSpec((1, 6, d), lambda i, *_: (i // tiles_per_sample, 0, 0)),
                  pl.BlockSpec((1, d), const2),
                  pl.BlockSpec((1, d), const2),
                  pl.BlockSpec(memory_space=pl.ANY)],
        out_specs=pl.BlockSpec((tt, d), tok),
        scratch_shapes=[pltpu.VMEM((2, TILE_SORTED_ROWS // SEG_ALIGN, SEG_ALIGN, d), BF16),
                        pltpu.SemaphoreType.DMA((2,))])
    return pl.pallas_call(
        _combine_kernel,
        grid_spec=grid_spec,
        out_shape=jax.ShapeDtypeStruct((n, d), F32),
        compiler_params=_cparams("arbitrary"),
        name="combine",
    )(cnt, lstart, dst, total, lrow, gates, x1, mod, g, b, y_sorted)


def _rope_tables(t):
    rows = jnp.repeat(jnp.arange(t // GRID_W, dtype=jnp.int32), GRID_W).astype(F32)
    cols = jnp.tile(jnp.arange(GRID_W, dtype=jnp.int32), t // GRID_W).astype(F32)
    n_freq = HEAD_DIM // 4
    inv = ROPE_BASE ** (-jnp.arange(n_freq, dtype=F32) / n_freq)
    ang_r = rows[:, None] * inv
    ang_c = cols[:, None] * inv
    ang = jnp.concatenate([ang_r, ang_r, ang_c, ang_c], -1)
    ang = jnp.concatenate([ang, ang], -1)
    cos, sin = jnp.cos(ang), jnp.sin(ang)
    first_half = (jnp.arange(LANES) % 32) < 16
    return cos, jnp.where(first_half, -sin, 0.0), jnp.where(first_half, 0.0, sin)


def kernel(x, c, ctx, c_ctx, ln_in_g, ln_in_b, w_ada, b_ada, w_in, ret_log_decay_f, ret_log_decay_b,
           ret_gn_g, ret_gn_b, q_norm_g, k_norm_g, w_o, ln1_g, ln1_b, router_w, router_b,
           w_up, b_up, w_down, b_down, ln2_g, ln2_b):
    b, t, d = x.shape
    n = b * t
    row = lambda v: v.reshape(1, -1)

    cc = jnp.concatenate([c, c_ctx[None]], 0)
    cc = jnp.pad(cc, ((0, (-cc.shape[0]) % 8), (0, 0)))
    mod_all = _ada_mod(cc, w_ada[0], row(b_ada[0]))
    mod = mod_all[:b].reshape(b, 6, d)
    mod_ctx = mod_all[b:b + 1].reshape(1, 6, d)

    w_in_bf = w_in[0].astype(BF16)
    lane = jnp.arange(LANES)
    seg = (lane[:, None] // HEAD_DIM == lane[None, :] // HEAD_DIM).astype(BF16)
    qg = jnp.tile(q_norm_g[0], ATT_HEADS).reshape(1, -1)
    kg = jnp.tile(k_norm_g[0], ATT_KV_HEADS).reshape(1, -1)
    lng, lnb = row(ln_in_g), row(ln_in_b)

    rq, rkt, rv, rg, aq, akt, av = _in_proj(x, mod, lng, lnb, w_in_bf, seg, qg, kg, _rope_tables(t))
    _, rktc, rvc, _, _, aktc, avc = _in_proj(ctx, mod_ctx, lng, lnb, w_in_bf, seg, qg, kg, None)

    ret = _retention(row(ret_log_decay_f[0]), row(ret_log_decay_b[0]), rq, rkt, rv, rg, rktc, rvc,
                     row(ret_gn_g[0]), row(ret_gn_b[0]))
    tk = ctx.shape[1] + t
    av_ones = jnp.concatenate([jnp.concatenate([avc, av], 1), jnp.ones((b, tk, LANES), BF16)], 2)
    att = _attention(aq, jnp.concatenate([aktc, akt], 2), av_ones.astype(jnp.float8_e4m3fn))

    rw = jnp.pad(router_w[0], ((0, 0), (0, LANES - N_EXPERTS)))
    rb = jnp.pad(row(router_b[0]), ((0, 0), (0, LANES - N_EXPERTS)))
    rw_hi = rw.astype(BF16)
    rw_lo = (rw - rw_hi.astype(F32)).astype(BF16)
    tt = TOKEN_TILE
    n_tiles = n // tt
    before = jnp.triu(jnp.ones((tt, tt), BF16), 1)
    below = jnp.tril(jnp.ones((N_EXPERTS, N_EXPERTS), BF16), -1)
    x1, h2, gates_t, lrow_t, cnt, lstart, tbase, phi, rem = _post_mixer(
        ret, att, x, mod, lng, lnb, w_o[0].astype(BF16), row(ln1_g[0]), row(ln1_b[0]),
        rw_hi, rw_lo, rb, before, below)
    token_major = lambda v: v.transpose(0, 2, 1).reshape(n, TOP_K)
    gates, lrow = token_major(gates_t), token_major(lrow_t)

    tm = EXPERT_ROWS
    cnt, lstart, tbase, phi, rem = (v[:, :, 0] for v in (cnt, lstart, tbase, phi, rem))
    totals = jnp.max(tbase + cnt, axis=0)
    exp_rows = (totals + tm - 1) // tm * tm
    exp_end = jnp.cumsum(exp_rows)
    exp_start = exp_end - exp_rows
    dst = (exp_start[None, :] + tbase).reshape(-1)
    max_rows = n * TOP_K + N_EXPERTS * (SEG_ALIGN - 1) + N_EXPERTS * (tm - 1)
    n_blocks = -(-max_rows // tm)
    block_start = jnp.arange(n_blocks, dtype=jnp.int32) * tm
    block_expert = jnp.minimum(jnp.sum(exp_end[None, :] <= block_start[:, None], axis=1),
                               N_EXPERTS - 1).astype(jnp.int32)
    n_active = (exp_end[-1:] // tm).astype(jnp.int32)
    groups = lambda v: (v // SEG_ALIGN).astype(jnp.int32).reshape(-1)
    cnt_g, lstart_g, dst_g, total_g = groups(cnt), groups(lstart), groups(dst), groups(jnp.sum(cnt, axis=1))

    flat = lambda v: v.astype(jnp.int32).reshape(-1)
    ncopy = cnt // SEG_ALIGN - (rem > 0)
    real_rows = jnp.where(cnt > 0, cnt - phi - jnp.where(rem > 0, SEG_ALIGN - rem, 0), 0)
    real_totals = jnp.sum(real_rows, axis=0)
    flush = (real_totals % SEG_ALIGN) > 0
    flush_dst = (exp_start + real_totals // SEG_ALIGN * SEG_ALIGN) // SEG_ALIGN
    x_sorted = _dispatch(cnt_g, flat(ncopy), lstart_g, dst_g, flat(jnp.sum(ncopy, axis=1)), total_g, flat(phi),
                         flat(rem), flat(flush), flat(flush_dst), groups(exp_rows - totals),
                         groups(exp_start + totals), n_active, h2, lrow_t, n_blocks * tm)
    y_sorted = _experts(block_expert, n_active, x_sorted, w_up[0], b_up[0], w_down[0], b_down[0])
    out = _combine(cnt_g, lstart_g, dst_g, total_g, lrow, gates, x1, mod, row(ln2_g[0]), row(ln2_b[0]),
                   y_sorted, t)
    return out.reshape(b, t, d)
```

```python
import functools

import jax
import jax.numpy as jnp
from jax import lax
from jax.experimental import pallas as pl
from jax.experimental.pallas import tpu as pltpu

GRID_W = 64
HEAD_DIM = 64
ROPE_BASE = 10000.0
LN_EPS = 1e-6
RET_HEADS = 4
RET_DK = 64
RET_DV = 128
RET_CHUNK = 128
ATT_HEADS = 8
ATT_KV_HEADS = 2
ATT_GROUP = ATT_HEADS // ATT_KV_HEADS
N_EXPERTS = 32
TOP_K = 4
SWIGLU_LIMIT = 7.0
SWIGLU_ALPHA = 1.702
DEPTH = 1
DN_ALPHA = (2.0 * DEPTH) ** 0.25
LOG2_E = 1.4426950408889634

RET_Q_W = RET_HEADS * RET_DK
RET_V_W = RET_HEADS * RET_DV
ATT_Q_W = ATT_HEADS * HEAD_DIM
ATT_K_W = ATT_KV_HEADS * HEAD_DIM

LANES = 128
VMEM_LIMIT_BYTES = 56 * 1024 * 1024

TOKEN_TILE = 512
ATT_Q_TILE = 512
ATT_Q_SUB = 256
ATT_P_SHIFT = 8.0
EXPERT_ROWS = 512
SEG_ALIGN = 16
TILE_SORTED_ROWS = 3072
PERM_CHUNK = 512
TABLE_COLUMNS = 8

F32 = jnp.float32
BF16 = jnp.bfloat16


def _cparams(*sem):
    return pltpu.CompilerParams(dimension_semantics=sem, vmem_limit_bytes=VMEM_LIMIT_BYTES)


def _layer_norm(x, g, b):
    mu = jnp.mean(x, -1, keepdims=True)
    xc = x - mu
    var = jnp.mean(xc * xc, -1, keepdims=True)
    return xc * lax.rsqrt(var + LN_EPS) * g + b


def _silu(x):
    return x * (1.0 / (1.0 + jnp.exp(-x)))


def _ada_kernel(c_ref, w_ref, b_ref, o_ref):
    a = _silu(c_ref[...])
    o_ref[...] = jnp.dot(a, w_ref[...], preferred_element_type=F32,
                         precision=lax.Precision.HIGHEST) + b_ref[...]


def _ada_mod(cc, w, b):
    m, d = cc.shape
    n = w.shape[1]
    tn = 1536
    return pl.pallas_call(
        _ada_kernel,
        grid=(n // tn,),
        in_specs=[pl.BlockSpec((m, d), lambda j: (0, 0)),
                  pl.BlockSpec((d, tn), lambda j: (0, j)),
                  pl.BlockSpec((1, tn), lambda j: (0, j))],
        out_specs=pl.BlockSpec((m, tn), lambda j: (0, j)),
        out_shape=jax.ShapeDtypeStruct((m, n), F32),
        compiler_params=_cparams("arbitrary"),
        name="ada_mod",
    )(cc, w, b)


def _rope(x, cos, sinn, sinp):
    outs = []
    for j in range(x.shape[1] // LANES):
        xg = x[:, j * LANES:(j + 1) * LANES]
        outs.append(xg * cos + pltpu.roll(xg, LANES - 16, 1) * sinn + pltpu.roll(xg, 16, 1) * sinp)
    return outs[0] if len(outs) == 1 else jnp.concatenate(outs, axis=1)


def _head_rms(x, seg, gain):
    outs = []
    for j in range(x.shape[1] // LANES):
        xg = x[:, j * LANES:(j + 1) * LANES]
        sq = xg * xg
        hi = sq.astype(BF16)
        lo = (sq - hi.astype(F32)).astype(BF16)
        ms = (jnp.dot(hi, seg, preferred_element_type=F32)
              + jnp.dot(lo, seg, preferred_element_type=F32)) * (1.0 / HEAD_DIM)
        outs.append(xg * lax.rsqrt(ms + LN_EPS) * gain[:, j * LANES:(j + 1) * LANES])
    return outs[0] if len(outs) == 1 else jnp.concatenate(outs, axis=1)


def _in_proj_kernel(*refs, rope):
    if rope:
        (x_ref, mod_ref, lng_ref, lnb_ref, w_ref, seg_ref, qg_ref, kg_ref, cos_ref, sinn_ref, sinp_ref,
         rq_ref, rkt_ref, rv_ref, rg_ref, aq_ref, akt_ref, av_ref) = refs
    else:
        (x_ref, mod_ref, lng_ref, lnb_ref, w_ref, seg_ref, qg_ref, kg_ref,
         rq_ref, rkt_ref, rv_ref, rg_ref, aq_ref, akt_ref, av_ref) = refs
    x = x_ref[0]
    xn = _layer_norm(x, lng_ref[...], lnb_ref[...])
    h = (xn * (1.0 + mod_ref[0, 1:2, :]) + mod_ref[0, 0:1, :]).astype(BF16)
    seg = seg_ref[...]

    def rotate(v):
        return _rope(v, cos_ref[...], sinn_ref[...], sinp_ref[...]) if rope else v

    o_rv = 2 * RET_Q_W
    o_aq = o_rv + 2 * RET_V_W
    o_ak = o_aq + ATT_Q_W
    o_end = o_ak + 2 * ATT_K_W

    def project(lo, hi):
        return jnp.dot(h, w_ref[:, lo:hi], preferred_element_type=F32)

    def finish_aq(u):
        aq = rotate(_head_rms(u, seg, qg_ref[...]))
        aq_ref[0] = (aq * (HEAD_DIM ** -0.5 * LOG2_E)).astype(BF16)

    def finish_akv(u):
        ak = rotate(_head_rms(u[:, :ATT_K_W], seg, kg_ref[...]))
        akt_ref[0] = ak.T.astype(BF16)
        av_ref[0] = u[:, ATT_K_W:].astype(BF16)

    def finish_rqk(u):
        rq_ref[0] = rotate(u[:, :RET_Q_W])
        rkt_ref[0] = (rotate(u[:, RET_Q_W:]) * (RET_DK ** -0.5)).T

    def finish_rvg(u):
        rv_ref[0] = u[:, :RET_V_W].astype(BF16)
        rg_ref[0] = u[:, RET_V_W:].astype(BF16)

    stages = [((o_aq, o_ak), finish_aq), ((o_ak, o_end), finish_akv),
              ((0, o_rv), finish_rqk), ((o_rv, o_aq), finish_rvg)]
    u_next = project(*stages[0][0])
    for idx, (_, finish) in enumerate(stages):
        u_cur = u_next
        if idx + 1 < len(stages):
            u_next = project(*stages[idx + 1][0])
        finish(u_cur)


def _in_proj(x, mod, lng, lnb, w_bf, seg, qg, kg, tables):
    b, t, d = x.shape
    tt = min(TOKEN_TILE, t)
    rope = tables is not None
    mod_b = mod.shape[0]
    mod_map = (lambda bi, i: (bi, 0, 0)) if mod_b > 1 else (lambda bi, i: (0, 0, 0))
    const2 = lambda bi, i: (0, 0)
    tok3 = lambda bi, i: (bi, i, 0)
    in_specs = [pl.BlockSpec((1, tt, d), tok3),
                pl.BlockSpec((1, 6, d), mod_map),
                pl.BlockSpec((1, d), const2),
                pl.BlockSpec((1, d), const2),
                pl.BlockSpec(w_bf.shape, const2),
                pl.BlockSpec(seg.shape, const2),
                pl.BlockSpec(qg.shape, const2),
                pl.BlockSpec(kg.shape, const2)]
    args = [x, mod, lng, lnb, w_bf, seg, qg, kg]
    if rope:
        in_specs += [pl.BlockSpec((tt, LANES), lambda bi, i: (i, 0))] * 3
        args += list(tables)
    out_shape = [jax.ShapeDtypeStruct((b, t, RET_Q_W), F32),
                 jax.ShapeDtypeStruct((b, RET_Q_W, t), F32),
                 jax.ShapeDtypeStruct((b, t, RET_V_W), BF16),
                 jax.ShapeDtypeStruct((b, t, RET_V_W), BF16),
                 jax.ShapeDtypeStruct((b, t, ATT_Q_W), BF16),
                 jax.ShapeDtypeStruct((b, ATT_K_W, t), BF16),
                 jax.ShapeDtypeStruct((b, t, ATT_K_W), BF16)]
    out_specs = [pl.BlockSpec((1, tt, RET_Q_W), tok3),
                 pl.BlockSpec((1, RET_Q_W, tt), lambda bi, i: (bi, 0, i)),
                 pl.BlockSpec((1, tt, RET_V_W), tok3),
                 pl.BlockSpec((1, tt, RET_V_W), tok3),
                 pl.BlockSpec((1, tt, ATT_Q_W), tok3),
                 pl.BlockSpec((1, ATT_K_W, tt), lambda bi, i: (bi, 0, i)),
                 pl.BlockSpec((1, tt, ATT_K_W), tok3)]
    return pl.pallas_call(
        functools.partial(_in_proj_kernel, rope=rope),
        grid=(b, t // tt),
        in_specs=in_specs,
        out_specs=out_specs,
        out_shape=out_shape,
        compiler_params=_cparams("parallel", "parallel"),
        name="in_proj_rope" if rope else "in_proj_ctx",
    )(*args)


def _retention_kernel(lgf_ref, lgb_ref, q_ref, kt_ref, v_ref, g_ref, ktc_ref, vc_ref, gng_ref, gnb_ref,
                      o_ref, kv_ref, rhs_ref):
    pair = pl.program_id(1)
    t = q_ref.shape[1]
    n_chunks = t // RET_CHUNK
    ctx_len = ktc_ref.shape[2]
    c = RET_CHUNK
    dk = RET_DK
    pos_col = lax.broadcasted_iota(jnp.int32, (c, 1), 0).astype(F32)
    pos_row = lax.broadcasted_iota(jnp.int32, (1, c), 1).astype(F32)
    row = lax.broadcasted_iota(jnp.int32, (c, c), 0)
    col = lax.broadcasted_iota(jnp.int32, (c, c), 1)
    diff = (row - col).astype(F32)
    lane = lax.broadcasted_iota(jnp.int32, (c, 2 * dk), 1)
    cpos = lax.broadcasted_iota(jnp.int32, (1, ctx_len), 1).astype(F32)

    heads = []
    for hh in range(2):
        head = pair * 2 + hh
        lgf = lgf_ref[0, head]
        lgb = lgb_ref[0, head]
        heads.append(dict(
            kl=slice(hh * dk, (hh + 1) * dk),
            vl=slice(hh * RET_DV, (hh + 1) * RET_DV),
            lgf=lgf, lgb=lgb,
            kdf=jnp.exp(lgf * (c - 1.0 - pos_row)),
            kdb=jnp.exp(lgb * pos_row),
            qd=jnp.where(lane < dk, jnp.exp(lgf * (pos_col + 1.0)), jnp.exp(lgb * (c - pos_col))),
            decay=(jnp.where(diff >= 0, jnp.exp(lgf * jnp.maximum(diff, 0.0)), 0.0)
                   + jnp.where(diff <= 0, jnp.exp(lgb * jnp.maximum(-diff, 0.0)), 0.0))))

    def kv_body(n, carry):
        r = pl.ds(pl.multiple_of(n * c, c), c)
        for hh, hd in enumerate(heads):
            kt = kt_ref[0, hd["kl"], r]
            vc = v_ref[0, r, hd["vl"]]
            both = jnp.concatenate([kt * hd["kdf"], kt * hd["kdb"]], axis=0).astype(BF16)
            kv_ref[hh, n] = jnp.dot(both, vc, preferred_element_type=F32)
            rhs_ref[hh, n, 0:c, :] = vc
        return carry

    lax.fori_loop(0, n_chunks, kv_body, 0, unroll=4)

    for hh, hd in enumerate(heads):
        lgf, lgb = hd["lgf"], hd["lgb"]
        ktc = ktc_ref[0, hd["kl"], :]
        both = jnp.concatenate([ktc * jnp.exp(lgf * (ctx_len - 1.0 - cpos)), ktc * jnp.exp(lgb * cpos)],
                               axis=0).astype(BF16)
        s_ctx = jnp.dot(both, vc_ref[0, :, hd["vl"]], preferred_element_type=F32)
        s_f, s_b = s_ctx[0:dk], s_ctx[dk:]
        gfc = jnp.exp(lgf * c)
        gbc = jnp.exp(lgb * c)
        for n in range(n_chunks):
            rhs_ref[hh, n, c:c + dk, :] = s_f.astype(BF16)
            s_f = gfc * s_f + kv_ref[hh, n, 0:dk, :]
        for n in range(n_chunks - 1, -1, -1):
            rhs_ref[hh, n, c + dk:, :] = s_b.astype(BF16)
            s_b = gbc * s_b + kv_ref[hh, n, dk:, :]

    def out_body(n, carry):
        r = pl.ds(pl.multiple_of(n * c, c), c)
        qq = q_ref[0, r, :]
        swapped = pltpu.roll(qq, dk, 1)
        for hh, hd in enumerate(heads):
            vl = hd["vl"]
            s = jnp.dot(q_ref[0, r, hd["kl"]].astype(BF16), kt_ref[0, hd["kl"], r].astype(BF16),
                        preferred_element_type=F32)
            q2 = jnp.where(lane < dk, qq, swapped) if hh == 0 else jnp.where(lane < dk, swapped, qq)
            lhs = jnp.concatenate([(s * hd["decay"]).astype(BF16), (q2 * hd["qd"]).astype(BF16)], axis=1)
            y = jnp.dot(lhs, rhs_ref[hh, n], preferred_element_type=F32)
            mu = jnp.mean(y, -1, keepdims=True)
            yc = y - mu
            var = jnp.mean(yc * yc, -1, keepdims=True)
            yn = yc * lax.rsqrt(var + LN_EPS) * gng_ref[:, vl] + gnb_ref[:, vl]
            o_ref[0, r, vl] = (yn * _silu(g_ref[0, r, vl].astype(F32))).astype(o_ref.dtype)
        return carry

    lax.fori_loop(0, n_chunks, out_body, 0, unroll=8)


def _retention(lgf, lgb, rq, rkt, rv, rg, rktc, rvc, gng, gnb):
    b, t, _ = rq.shape
    ctx_len = rktc.shape[2]
    n_chunks = t // RET_CHUNK
    pair3 = lambda bi, p: (bi, 0, p)
    pair_rows = lambda bi, p: (bi, p, 0)
    smem = pl.BlockSpec(memory_space=pltpu.SMEM)
    return pl.pallas_call(
        _retention_kernel,
        grid=(b, RET_HEADS // 2),
        in_specs=[smem, smem,
                  pl.BlockSpec((1, t, 2 * RET_DK), pair3),
                  pl.BlockSpec((1, 2 * RET_DK, t), pair_rows),
                  pl.BlockSpec((1, t, 2 * RET_DV), pair3),
                  pl.BlockSpec((1, t, 2 * RET_DV), pair3),
                  pl.BlockSpec((1, 2 * RET_DK, ctx_len), pair_rows),
                  pl.BlockSpec((1, ctx_len, 2 * RET_DV), pair3),
                  pl.BlockSpec((1, 2 * RET_DV), lambda bi, p: (0, p)),
                  pl.BlockSpec((1, 2 * RET_DV), lambda bi, p: (0, p))],
        out_specs=pl.BlockSpec((1, t, 2 * RET_DV), pair3),
        out_shape=jax.ShapeDtypeStruct((b, t, RET_V_W), BF16),
        scratch_shapes=[pltpu.VMEM((2, n_chunks, 2 * RET_DK, RET_DV), F32),
                        pltpu.VMEM((2, n_chunks, RET_CHUNK + 2 * RET_DK, RET_DV), BF16)],
        compiler_params=_cparams("parallel", "parallel"),
        name="retention",
    )(lgf, lgb, rq, rkt, rv, rg, rktc, rvc, gng, gnb)


def _attention_kernel(q_ref, kt_ref, v_ref, o_ref, s_buf, p_buf):
    ts = ATT_Q_SUB
    lane = lax.broadcasted_iota(jnp.int32, (ts, LANES), 1)
    pieces = [(sub, kvh) for sub in range(q_ref.shape[1] // ts) for kvh in range(ATT_KV_HEADS)]

    def scores(i):
        sub, kvh = pieces[i]
        q = jnp.concatenate(
            [q_ref[0, sub * ts:(sub + 1) * ts, (kvh * ATT_GROUP + g) * HEAD_DIM:(kvh * ATT_GROUP + g + 1) * HEAD_DIM]
             for g in range(ATT_GROUP)], axis=0)
        s_buf[i % 2] = jnp.dot(q, kt_ref[0, kvh * HEAD_DIM:(kvh + 1) * HEAD_DIM, :], preferred_element_type=F32)

    def numerators(i):
        s = s_buf[i % 2]
        top = jnp.max(s, -1, keepdims=True) - ATT_P_SHIFT
        p_buf[i % 2] = jnp.exp2(s - top).astype(jnp.float8_e4m3fn)

    def values(i):
        sub, kvh = pieces[i]
        ov = jnp.dot(p_buf[i % 2], v_ref[0], preferred_element_type=F32)
        o = ov[:, :LANES] / ov[:, LANES:LANES + 1]
        for pair in range(ATT_GROUP // 2):
            halves = []
            for half in range(2):
                g = pair * 2 + half
                og = o[g * ts:(g + 1) * ts, :]
                if kvh != half:
                    og = pltpu.roll(og, HEAD_DIM, 1)
                halves.append(og)
            col = (kvh * ATT_GROUP // 2 + pair) * LANES
            o_ref[0, sub * ts:(sub + 1) * ts, col:col + LANES] = jnp.where(
                lane < HEAD_DIM, halves[0], halves[1]).astype(o_ref.dtype)

    scores(0)
    scores(1)
    numerators(0)
    for i in range(len(pieces)):
        if i + 2 < len(pieces):
            scores(i + 2)
        if i + 1 < len(pieces):
            numerators(i + 1)
        values(i)


def _attention(aq, akt, av_ones):
    b, t, _ = aq.shape
    tk = akt.shape[2]
    tq = ATT_Q_TILE
    return pl.pallas_call(
        _attention_kernel,
        grid=(b, t // tq),
        in_specs=[pl.BlockSpec((1, tq, ATT_Q_W), lambda bi, i: (bi, i, 0)),
                  pl.BlockSpec((1, ATT_K_W, tk), lambda bi, i: (bi, 0, 0)),
                  pl.BlockSpec((1, tk, av_ones.shape[2]), lambda bi, i: (bi, 0, 0))],
        out_specs=pl.BlockSpec((1, tq, ATT_Q_W), lambda bi, i: (bi, i, 0)),
        out_shape=jax.ShapeDtypeStruct((b, t, ATT_Q_W), BF16),
        scratch_shapes=[pltpu.VMEM((2, ATT_GROUP * ATT_Q_SUB, tk), F32),
                        pltpu.VMEM((2, ATT_GROUP * ATT_Q_SUB, tk), jnp.float8_e4m3fn)],
        compiler_params=_cparams("parallel", "parallel"),
        name="attention",
    )(aq, akt, av_ones)


def _post_mixer_kernel(ret_ref, att_ref, x_ref, mod_ref, lng_ref, lnb_ref, wo_ref, l1g_ref, l1b_ref,
                       rwh_ref, rwl_ref, rb_ref, before_ref, below_ref,
                       x1_ref, h2_ref, gate_ref, lrow_ref, table_ref, base_acc):
    first = (pl.program_id(0) == 0) & (pl.program_id(1) == 0)

    @pl.when(first)
    def _():
        base_acc[...] = jnp.zeros_like(base_acc)

    half = ret_ref.shape[2]
    y = (jnp.dot(ret_ref[0], wo_ref[0:half, :], preferred_element_type=F32)
         + jnp.dot(att_ref[0], wo_ref[half:, :], preferred_element_type=F32))
    xn = _layer_norm(x_ref[0], lng_ref[...], lnb_ref[...])
    x1 = _layer_norm(DN_ALPHA * xn + mod_ref[0, 2:3, :] * y, l1g_ref[...], l1b_ref[...])
    x1_ref[...] = x1
    h2 = x1 * (1.0 + mod_ref[0, 4:5, :]) + mod_ref[0, 3:4, :]
    h_hi = h2.astype(BF16)
    h2_ref[...] = h_hi

    h_lo = (h2 - h_hi.astype(F32)).astype(BF16)
    logits = (jnp.dot(h_hi, rwh_ref[...], preferred_element_type=F32)
              + jnp.dot(h_lo, rwh_ref[...], preferred_element_type=F32)
              + jnp.dot(h_hi, rwl_ref[...], preferred_element_type=F32)) + rb_ref[...]

    tt = logits.shape[0]
    work = logits.T[0:N_EXPERTS, :]
    e_iota = lax.broadcasted_iota(jnp.int32, (N_EXPERTS, tt), 0)
    sels, vals = [], []
    onehot = jnp.zeros((N_EXPERTS, tt), F32)
    for _ in range(TOP_K):
        m = jnp.max(work, 0, keepdims=True)
        idx = jnp.min(jnp.where(work == m, e_iota, N_EXPERTS), 0, keepdims=True)
        sel = e_iota == idx
        sels.append(sel)
        vals.append(m)
        onehot = onehot + sel.astype(F32)
        work = jnp.where(sel, -jnp.inf, work)
    exps = [jnp.exp(v - vals[0]) for v in vals]
    denom = exps[0] + exps[1] + exps[2] + exps[3]

    count = jnp.sum(onehot, 1, keepdims=True)
    base = base_acc[...]
    phi = base - SEG_ALIGN * jnp.floor(base * (1.0 / SEG_ALIGN))
    units = jnp.where(count > 0, jnp.floor((phi + count + (SEG_ALIGN - 1.0)) * (1.0 / SEG_ALIGN)), 0.0)
    lstart = SEG_ALIGN * jnp.dot(below_ref[...], jnp.broadcast_to(units, (N_EXPERTS, LANES)).astype(BF16),
                                 preferred_element_type=F32)[:, 0:1]
    rank = jnp.dot(onehot.astype(BF16), before_ref[...], preferred_element_type=F32) + (lstart + phi)
    end = phi + count

    k_iota = lax.broadcasted_iota(jnp.int32, (TOP_K, tt), 0)
    gate_out = jnp.zeros((TOP_K, tt), F32)
    lrow_out = jnp.zeros((TOP_K, tt), F32)
    for k in range(TOP_K):
        pk = jnp.sum(jnp.where(sels[k], rank, 0.0), 0, keepdims=True)
        gate_out = jnp.where(k_iota == k, exps[k] / denom, gate_out)
        lrow_out = jnp.where(k_iota == k, pk, lrow_out)
    gate_ref[0] = gate_out
    lrow_ref[0] = lrow_out.astype(jnp.int32)
    columns = [units * SEG_ALIGN, lstart, base - phi, jnp.where(count > 0, phi, 0.0),
               jnp.where(count > 0, end - SEG_ALIGN * jnp.floor(end * (1.0 / SEG_ALIGN)), 0.0)]
    c_iota = lax.broadcasted_iota(jnp.int32, (N_EXPERTS, TABLE_COLUMNS), 1)
    table = jnp.zeros((N_EXPERTS, TABLE_COLUMNS), F32)
    for k, column in enumerate(columns):
        table = jnp.where(c_iota == k, column, table)
    table_ref[0] = table.astype(jnp.int32)
    base_acc[...] = base + count


def _post_mixer(ret, att, x, mod, lng, lnb, wo_bf, l1g, l1b, rw_hi, rw_lo, rb, before, below):
    b, t, d = x.shape
    tt = TOKEN_TILE
    nt = t // tt
    n = b * t
    const2 = lambda bi, i: (0, 0)
    tok3 = lambda bi, i: (bi, i, 0)
    flat = lambda bi, i: (bi * nt + i, 0)
    tile3 = lambda bi, i: (bi * nt + i, 0, 0)
    vec = pl.BlockSpec((1, d), const2)
    table = jax.ShapeDtypeStruct((b * nt, N_EXPERTS, TABLE_COLUMNS), jnp.int32)
    table_spec = pl.BlockSpec((1, N_EXPERTS, TABLE_COLUMNS), tile3)
    per_k = pl.BlockSpec((1, TOP_K, tt), tile3)
    return pl.pallas_call(
        _post_mixer_kernel,
        grid=(b, nt),
        in_specs=[pl.BlockSpec((1, tt, ret.shape[2]), tok3),
                  pl.BlockSpec((1, tt, att.shape[2]), tok3),
                  pl.BlockSpec((1, tt, d), tok3),
                  pl.BlockSpec((1, 6, d), lambda bi, i: (bi, 0, 0)),
                  vec, vec,
                  pl.BlockSpec(wo_bf.shape, const2),
                  vec, vec,
                  pl.BlockSpec(rw_hi.shape, const2),
                  pl.BlockSpec(rw_lo.shape, const2),
                  pl.BlockSpec(rb.shape, const2),
                  pl.BlockSpec((tt, tt), const2),
                  pl.BlockSpec((N_EXPERTS, N_EXPERTS), const2)],
        out_specs=[pl.BlockSpec((tt, d), flat),
                   pl.BlockSpec((tt, d), flat),
                   per_k, per_k, table_spec],
        out_shape=[jax.ShapeDtypeStruct((n, d), F32),
                   jax.ShapeDtypeStruct((n, d), BF16),
                   jax.ShapeDtypeStruct((b * nt, TOP_K, tt), F32),
                   jax.ShapeDtypeStruct((b * nt, TOP_K, tt), jnp.int32),
                   table],
        scratch_shapes=[pltpu.VMEM((N_EXPERTS, 1), F32)],
        compiler_params=_cparams("arbitrary", "arbitrary"),
        name="post_mixer",
    )(ret, att, x, mod, lng, lnb, wo_bf, l1g, l1b, rw_hi, rw_lo, rb, before, below)


def _segment_starts(cnt_ref, lstart_ref, dst_ref, tile, make_copy):
    def body(e, carry):
        t = tile * N_EXPERTS + e
        n = cnt_ref[t]

        @pl.when(n > 0)
        def _():
            make_copy(lstart_ref[t], dst_ref[t], n).start()
        return carry
    lax.fori_loop(0, N_EXPERTS, body, 0, unroll=4)


def _segment_waits(total_ref, tile, make_copy):
    make_copy(0, 0, total_ref[tile]).wait()


def _zero_fill_copies(tail_len_ref, tail_dst_ref, nact_ref, zbuf, x_hbm, zsem, act):
    block = zbuf.shape[0]

    def tail_body(e, carry):
        n = tail_len_ref[e]

        @pl.when(n > 0)
        def _():
            copy = pltpu.make_async_copy(zbuf.at[pl.ds(0, n)], x_hbm.at[pl.ds(tail_dst_ref[e], n)], zsem)
            getattr(copy, act)()
        return carry
    lax.fori_loop(0, N_EXPERTS, tail_body, 0)

    def block_body(j, carry):
        copy = pltpu.make_async_copy(zbuf, x_hbm.at[pl.ds(j * block, block)], zsem)
        getattr(copy, act)()
        return carry
    lax.fori_loop(nact_ref[0], x_hbm.shape[0] // block, block_body, 0)


def _dispatch_kernel(cnt_ref, ncopy_ref, lstart_ref, dst_ref, total_ref, extent_ref, rem_ref, flush_ref,
                     flush_dst_ref, tail_len_ref, tail_dst_ref, nact_ref,
                     h_ref, lrow_ref, x_hbm, sbuf, sem, zbuf, zsem, carry):
    i = pl.program_id(0)
    last = pl.num_programs(0) - 1
    slot = i % 2
    rows = sbuf.shape[1] * SEG_ALIGN
    tt = h_ref.shape[0]
    d = h_ref.shape[1]
    chunk_groups = PERM_CHUNK // SEG_ALIGN

    def make_copy(tile_slot):
        def mk(ls, ds, size):
            return pltpu.make_async_copy(sbuf.at[tile_slot, pl.ds(ls, size)], x_hbm.at[pl.ds(ds, size)],
                                         sem.at[tile_slot])
        return mk

    @pl.when(i == 0)
    def _():
        zbuf[...] = jnp.zeros_like(zbuf)
        carry[...] = jnp.zeros_like(carry)
        _zero_fill_copies(tail_len_ref, tail_dst_ref, nact_ref, zbuf, x_hbm, zsem, "start")

    def perm(j):
        r_iota = lax.broadcasted_iota(jnp.int32, (PERM_CHUNK, tt), 0) + j * PERM_CHUNK
        hit = r_iota == lrow_ref[0, 0:1, :]
        for k in range(1, TOP_K):
            hit = hit | (r_iota == lrow_ref[0, k:k + 1, :])
        return jnp.where(hit, 1.0, 0.0).astype(BF16)

    def place(j, p):
        sbuf[slot, j * chunk_groups:(j + 1) * chunk_groups] = jnp.dot(
            p, h_ref[...], preferred_element_type=F32).astype(BF16).reshape(chunk_groups, SEG_ALIGN, d)

    n_chunks = rows // PERM_CHUNK
    p_next = perm(0)
    for j in range(n_chunks - 1):
        p_cur = p_next
        if j + 2 < n_chunks:
            p_next = perm(j + 1)
        place(j, p_cur)

    @pl.when(extent_ref[i] > (n_chunks - 1) * chunk_groups)
    def _():
        place(n_chunks - 1, perm(n_chunks - 1))
    def merge_body(e, c):
        t = i * N_EXPERTS + e
        n = cnt_ref[t]
        first = lstart_ref[t]
        present = n > 0
        old = carry[e]
        sbuf[slot, first] = sbuf[slot, first] + jnp.where(present, old, jnp.zeros_like(old))
        tail = sbuf[slot, jnp.maximum(first + n - 1, 0)]
        carry[e] = jnp.where(present, jnp.where(rem_ref[t] > 0, tail, jnp.zeros_like(tail)), old)
        return c
    lax.fori_loop(0, N_EXPERTS, merge_body, 0, unroll=4)

    _segment_starts(ncopy_ref, lstart_ref, dst_ref, i, make_copy(slot))

    @pl.when(i > 0)
    def _():
        _segment_waits(total_ref, i - 1, make_copy(1 - slot))

    def flush_copies(act):
        def body(e, c):
            @pl.when(flush_ref[e] > 0)
            def _():
                copy = pltpu.make_async_copy(carry.at[pl.ds(e, 1)], x_hbm.at[pl.ds(flush_dst_ref[e], 1)], zsem)
                getattr(copy, act)()
            return c
        lax.fori_loop(0, N_EXPERTS, body, 0)

    @pl.when(i == last)
    def _():
        flush_copies("start")
        _segment_waits(total_ref, i, make_copy(slot))
        _zero_fill_copies(tail_len_ref, tail_dst_ref, nact_ref, zbuf, x_hbm, zsem, "wait")
        flush_copies("wait")


def _dispatch(cnt, ncopy, lstart, dst, total, extent, rem, flush, flush_dst, tail_len, tail_dst, n_active,
              h2, lrow_t, n_rows):
    n, d = h2.shape
    tt = TOKEN_TILE
    grid_spec = pltpu.PrefetchScalarGridSpec(
        num_scalar_prefetch=12,
        grid=(n // tt,),
        in_specs=[pl.BlockSpec((tt, d), lambda i, *_: (i, 0)),
                  pl.BlockSpec((1, TOP_K, tt), lambda i, *_: (i, 0, 0))],
        out_specs=pl.BlockSpec(memory_space=pl.ANY),
        scratch_shapes=[pltpu.VMEM((2, TILE_SORTED_ROWS // SEG_ALIGN, SEG_ALIGN, d), BF16),
                        pltpu.SemaphoreType.DMA((2,)),
                        pltpu.VMEM((EXPERT_ROWS // SEG_ALIGN, SEG_ALIGN, d), BF16),
                        pltpu.SemaphoreType.DMA(()),
                        pltpu.VMEM((N_EXPERTS, SEG_ALIGN, d), BF16)])
    return pl.pallas_call(
        _dispatch_kernel,
        grid_spec=grid_spec,
        out_shape=jax.ShapeDtypeStruct((n_rows // SEG_ALIGN, SEG_ALIGN, d), BF16),
        compiler_params=_cparams("arbitrary"),
        name="dispatch",
    )(cnt, ncopy, lstart, dst, total, extent, rem, flush, flush_dst, tail_len, tail_dst, n_active, h2, lrow_t)


def _experts_kernel(be_ref, nact_ref, x_ref, wup_ref, bup_ref, wdn_ref, bdn_ref, y_ref,
                    wup_bf, wdn_bf):
    j = pl.program_id(0)
    dff = wdn_ref.shape[1]

    @pl.when(j < nact_ref[0])
    def _():
        prev = be_ref[jnp.maximum(j - 1, 0)]

        @pl.when((j == 0) | (be_ref[j] != prev))
        def _():
            wup_bf[...] = wup_ref[0].astype(BF16)
            wdn_bf[...] = wdn_ref[0].astype(BF16)

        x = x_ref[...].reshape(-1, x_ref.shape[2])
        u = jnp.dot(x, wup_bf[...], preferred_element_type=F32) + bup_ref[0]
        glu = jnp.minimum(u[:, :dff], SWIGLU_LIMIT)
        lin = jnp.clip(u[:, dff:], -SWIGLU_LIMIT, SWIGLU_LIMIT)
        act = glu * (1.0 / (1.0 + jnp.exp(-SWIGLU_ALPHA * glu))) * (lin + 1.0)
        y = jnp.dot(act.astype(BF16), wdn_bf[...], preferred_element_type=F32) + bdn_ref[0]
        y_ref[...] = y.astype(y_ref.dtype).reshape(y_ref.shape)

    @pl.when(j >= nact_ref[0])
    def _():
        y_ref[...] = jnp.zeros_like(y_ref)


def _experts(block_expert, n_active, x_sorted, w_up, b_up, w_down, b_down):
    n_blocks = block_expert.shape[0]
    tm = EXPERT_ROWS
    d = x_sorted.shape[2]
    e, _, up_w = w_up.shape
    dff = w_down.shape[1]
    by_expert = lambda j, be, na: (be[j], 0, 0)
    x_map = lambda j, be, na: (jnp.minimum(j, na[0] - 1), 0, 0)
    rows3 = (tm // SEG_ALIGN, SEG_ALIGN, d)
    grid_spec = pltpu.PrefetchScalarGridSpec(
        num_scalar_prefetch=2,
        grid=(n_blocks,),
        in_specs=[pl.BlockSpec(rows3, x_map),
                  pl.BlockSpec((1, d, up_w), by_expert),
                  pl.BlockSpec((1, 1, up_w), by_expert),
                  pl.BlockSpec((1, dff, d), by_expert),
                  pl.BlockSpec((1, 1, d), by_expert)],
        out_specs=pl.BlockSpec(rows3, lambda j, be, na: (j, 0, 0)),
        scratch_shapes=[pltpu.VMEM((d, up_w), BF16),
                        pltpu.VMEM((dff, d), BF16)])
    return pl.pallas_call(
        _experts_kernel,
        grid_spec=grid_spec,
        out_shape=jax.ShapeDtypeStruct(x_sorted.shape, BF16),
        compiler_params=_cparams("arbitrary"),
        name="experts",
    )(block_expert, n_active, x_sorted, w_up, b_up.reshape(e, 1, up_w), w_down,
      b_down.reshape(e, 1, d))


def _combine_kernel(cnt_ref, lstart_ref, dst_ref, total_ref, lrow_ref, gate_ref, x1_ref, mod_ref, g_ref, b_ref,
                    y_hbm, o_ref, ybuf, sem):
    i = pl.program_id(0)
    last = pl.num_programs(0) - 1
    slot = i % 2
    rows = ybuf.shape[1] * SEG_ALIGN
    tt = x1_ref.shape[0]
    d = x1_ref.shape[1]
    chunk_groups = PERM_CHUNK // SEG_ALIGN

    def make_copy(tile_slot):
        def mk(ls, ds, size):
            return pltpu.make_async_copy(y_hbm.at[pl.ds(ds, size)], ybuf.at[tile_slot, pl.ds(ls, size)],
                                         sem.at[tile_slot])
        return mk

    @pl.when(i == 0)
    def _():
        ybuf[...] = jnp.zeros_like(ybuf)
        _segment_starts(cnt_ref, lstart_ref, dst_ref, i, make_copy(slot))

    @pl.when(i < last)
    def _():
        _segment_starts(cnt_ref, lstart_ref, dst_ref, i + 1, make_copy(1 - slot))

    def weights(j):
        c_iota = lax.broadcasted_iota(jnp.int32, (tt, PERM_CHUNK), 1) + j * PERM_CHUNK
        w = jnp.zeros((tt, PERM_CHUNK), F32)
        for k in range(TOP_K):
            w = jnp.where(c_iota == lrow_ref[:, k:k + 1], gate_ref[:, k:k + 1], w)
        return w.astype(BF16)

    def gather(j, w):
        y_chunk = ybuf[slot, j * chunk_groups:(j + 1) * chunk_groups].reshape(PERM_CHUNK, d)
        return jnp.dot(w, y_chunk, preferred_element_type=F32)

    def finish(f):
        o_ref[...] = _layer_norm(DN_ALPHA * x1_ref[...] + mod_ref[0, 5:6, :] * f, g_ref[...], b_ref[...])

    n_chunks = rows // PERM_CHUNK
    w_next = weights(0)
    _segment_waits(total_ref, i, make_copy(slot))
    f = None
    for j in range(n_chunks - 1):
        w_cur = w_next
        if j + 2 < n_chunks:
            w_next = weights(j + 1)
        part = gather(j, w_cur)
        f = part if f is None else f + part

    need_last = total_ref[i] > (n_chunks - 1) * chunk_groups

    @pl.when(need_last)
    def _():
        finish(f + gather(n_chunks - 1, weights(n_chunks - 1)))

    @pl.when(jnp.logical_not(need_last))
    def _():
        finish(f)


def _combine(cnt, lstart, dst, total, lrow, gates, x1, mod, g, b, y_sorted, tokens_per_sample):
    n, d = x1.shape
    tt = TOKEN_TILE
    tiles_per_sample = tokens_per_sample // tt
    const2 = lambda i, *_: (0, 0)
    tok = lambda i, *_: (i, 0)
    grid_spec = pltpu.PrefetchScalarGridSpec(
        num_scalar_prefetch=4,
        grid=(n // tt,),
        in_specs=[pl.BlockSpec((tt, TOP_K), tok),
                  pl.BlockSpec((tt, TOP_K), tok),
                  pl.BlockSpec((tt, d), tok),
                  pl.BlockSpec((1, 6, d), lambda i, *_: (i // tiles_per_sample, 0, 0)),
                  pl.BlockSpec((1, d), const2),
                  pl.BlockSpec((1, d), const2),
                  pl.BlockSpec(memory_space=pl.ANY)],
        out_specs=pl.BlockSpec((tt, d), tok),
        scratch_shapes=[pltpu.VMEM((2, TILE_SORTED_ROWS // SEG_ALIGN, SEG_ALIGN, d), BF16),
                        pltpu.SemaphoreType.DMA((2,))])
    return pl.pallas_call(
        _combine_kernel,
        grid_spec=grid_spec,
        out_shape=jax.ShapeDtypeStruct((n, d), F32),
        compiler_params=_cparams("arbitrary"),
        name="combine",
    )(cnt, lstart, dst, total, lrow, gates, x1, mod, g, b, y_sorted)


def _rope_tables(t):
    rows = jnp.repeat(jnp.arange(t // GRID_W, dtype=jnp.int32), GRID_W).astype(F32)
    cols = jnp.tile(jnp.arange(GRID_W, dtype=jnp.int32), t // GRID_W).astype(F32)
    n_freq = HEAD_DIM // 4
    inv = ROPE_BASE ** (-jnp.arange(n_freq, dtype=F32) / n_freq)
    ang_r = rows[:, None] * inv
    ang_c = cols[:, None] * inv
    ang = jnp.concatenate([ang_r, ang_r, ang_c, ang_c], -1)
    ang = jnp.concatenate([ang, ang], -1)
    cos, sin = jnp.cos(ang), jnp.sin(ang)
    first_half = (jnp.arange(LANES) % 32) < 16
    return cos, jnp.where(first_half, -sin, 0.0), jnp.where(first_half, 0.0, sin)


def kernel(x, c, ctx, c_ctx, ln_in_g, ln_in_b, w_ada, b_ada, w_in, ret_log_decay_f, ret_log_decay_b,
           ret_gn_g, ret_gn_b, q_norm_g, k_norm_g, w_o, ln1_g, ln1_b, router_w, router_b,
           w_up, b_up, w_down, b_down, ln2_g, ln2_b):
    b, t, d = x.shape
    n = b * t
    row = lambda v: v.reshape(1, -1)

    cc = jnp.concatenate([c, c_ctx[None]], 0)
    cc = jnp.pad(cc, ((0, (-cc.shape[0]) % 8), (0, 0)))
    mod_all = _ada_mod(cc, w_ada[0], row(b_ada[0]))
    mod = mod_all[:b].reshape(b, 6, d)
    mod_ctx = mod_all[b:b + 1].reshape(1, 6, d)

    w_in_bf = w_in[0].astype(BF16)
    lane = jnp.arange(LANES)
    seg = (lane[:, None] // HEAD_DIM == lane[None, :] // HEAD_DIM).astype(BF16)
    qg = jnp.tile(q_norm_g[0], ATT_HEADS).reshape(1, -1)
    kg = jnp.tile(k_norm_g[0], ATT_KV_HEADS).reshape(1, -1)
    lng, lnb = row(ln_in_g), row(ln_in_b)

    rq, rkt, rv, rg, aq, akt, av = _in_proj(x, mod, lng, lnb, w_in_bf, seg, qg, kg, _rope_tables(t))
    _, rktc, rvc, _, _, aktc, avc = _in_proj(ctx, mod_ctx, lng, lnb, w_in_bf, seg, qg, kg, None)

    ret = _retention(row(ret_log_decay_f[0]), row(ret_log_decay_b[0]), rq, rkt, rv, rg, rktc, rvc,
                     row(ret_gn_g[0]), row(ret_gn_b[0]))
    tk = ctx.shape[1] + t
    av_ones = jnp.concatenate([jnp.concatenate([avc, av], 1), jnp.ones((b, tk, LANES), BF16)], 2)
    att = _attention(aq, jnp.concatenate([aktc, akt], 2), av_ones.astype(jnp.float8_e4m3fn))

    rw = jnp.pad(router_w[0], ((0, 0), (0, LANES - N_EXPERTS)))
    rb = jnp.pad(row(router_b[0]), ((0, 0), (0, LANES - N_EXPERTS)))
    rw_hi = rw.astype(BF16)
    rw_lo = (rw - rw_hi.astype(F32)).astype(BF16)
    tt = TOKEN_TILE
    n_tiles = n // tt
    before = jnp.triu(jnp.ones((tt, tt), BF16), 1)
    below = jnp.tril(jnp.ones((N_EXPERTS, N_EXPERTS), BF16), -1)
    x1, h2, gates_t, lrow_t, table = _post_mixer(
        ret, att, x, mod, lng, lnb, w_o[0].astype(BF16), row(ln1_g[0]), row(ln1_b[0]),
        rw_hi, rw_lo, rb, before, below)
    token_major = lambda v: v.transpose(0, 2, 1).reshape(n, TOP_K)
    gates, lrow = token_major(gates_t), token_major(lrow_t)

    tm = EXPERT_ROWS
    cnt, lstart, tbase, phi, rem = (table[:, :, k] for k in range(5))
    totals = jnp.max(tbase + cnt, axis=0)
    exp_rows = (totals + tm - 1) // tm * tm
    exp_end = jnp.cumsum(exp_rows)
    exp_start = exp_end - exp_rows
    dst = (exp_start[None, :] + tbase).reshape(-1)
    max_rows = n * TOP_K + N_EXPERTS * (SEG_ALIGN - 1) + N_EXPERTS * (tm - 1)
    n_blocks = -(-max_rows // tm)
    block_start = jnp.arange(n_blocks, dtype=jnp.int32) * tm
    block_expert = jnp.minimum(jnp.sum(exp_end[None, :] <= block_start[:, None], axis=1),
                               N_EXPERTS - 1).astype(jnp.int32)
    n_active = (exp_end[-1:] // tm).astype(jnp.int32)
    groups = lambda v: (v // SEG_ALIGN).astype(jnp.int32).reshape(-1)
    cnt_g, lstart_g, dst_g, total_g = groups(cnt), groups(lstart), groups(dst), groups(jnp.sum(cnt, axis=1))

    flat = lambda v: v.astype(jnp.int32).reshape(-1)
    ncopy = cnt // SEG_ALIGN - (rem > 0)
    real_rows = jnp.where(cnt > 0, cnt - phi - jnp.where(rem > 0, SEG_ALIGN - rem, 0), 0)
    real_totals = jnp.sum(real_rows, axis=0)
    flush = (real_totals % SEG_ALIGN) > 0
    flush_dst = (exp_start + real_totals // SEG_ALIGN * SEG_ALIGN) // SEG_ALIGN
    x_sorted = _dispatch(cnt_g, flat(ncopy), lstart_g, dst_g, flat(jnp.sum(ncopy, axis=1)), total_g, flat(rem),
                         flat(flush), flat(flush_dst), groups(exp_rows - totals), groups(exp_start + totals),
                         n_active, h2, lrow_t, n_blocks * tm)
    y_sorted = _experts(block_expert, n_active, x_sorted, w_up[0], b_up[0], w_down[0], b_down[0])
    out = _combine(cnt_g, lstart_g, dst_g, total_g, lrow, gates, x1, mod, row(ln2_g[0]), row(ln2_b[0]),
                   y_sorted, t)
    return out.reshape(b, t, d)
```

```python
import functools

import jax
import jax.numpy as jnp
from jax import lax
from jax.experimental import pallas as pl
from jax.experimental.pallas import tpu as pltpu

GRID_W = 64
HEAD_DIM = 64
ROPE_BASE = 10000.0
LN_EPS = 1e-6
RET_HEADS = 4
RET_DK = 64
RET_DV = 128
RET_CHUNK = 128
ATT_HEADS = 8
ATT_KV_HEADS = 2
ATT_GROUP = ATT_HEADS // ATT_KV_HEADS
N_EXPERTS = 32
TOP_K = 4
SWIGLU_LIMIT = 7.0
SWIGLU_ALPHA = 1.702
DEPTH = 1
DN_ALPHA = (2.0 * DEPTH) ** 0.25
LOG2_E = 1.4426950408889634

RET_Q_W = RET_HEADS * RET_DK
RET_V_W = RET_HEADS * RET_DV
ATT_Q_W = ATT_HEADS * HEAD_DIM
ATT_K_W = ATT_KV_HEADS * HEAD_DIM

LANES = 128
VMEM_LIMIT_BYTES = 56 * 1024 * 1024

TOKEN_TILE = 512
ATT_Q_TILE = 512
ATT_Q_SUB = 128
ATT_P_SHIFT = 8.0
EXPERT_ROWS = 512
SEG_ALIGN = 16
TILE_SORTED_ROWS = 3072
PERM_CHUNK = 512
TABLE_COLUMNS = 8

F32 = jnp.float32
BF16 = jnp.bfloat16


def _cparams(*sem):
    return pltpu.CompilerParams(dimension_semantics=sem, vmem_limit_bytes=VMEM_LIMIT_BYTES)


def _layer_norm(x, g, b):
    mu = jnp.mean(x, -1, keepdims=True)
    xc = x - mu
    var = jnp.mean(xc * xc, -1, keepdims=True)
    return xc * lax.rsqrt(var + LN_EPS) * g + b


def _silu(x):
    return x * (1.0 / (1.0 + jnp.exp(-x)))


def _ada_kernel(c_ref, w_ref, b_ref, o_ref):
    a = _silu(c_ref[...])
    o_ref[...] = jnp.dot(a, w_ref[...], preferred_element_type=F32,
                         precision=lax.Precision.HIGHEST) + b_ref[...]


def _ada_mod(cc, w, b):
    m, d = cc.shape
    n = w.shape[1]
    tn = 1536
    return pl.pallas_call(
        _ada_kernel,
        grid=(n // tn,),
        in_specs=[pl.BlockSpec((m, d), lambda j: (0, 0)),
                  pl.BlockSpec((d, tn), lambda j: (0, j)),
                  pl.BlockSpec((1, tn), lambda j: (0, j))],
        out_specs=pl.BlockSpec((m, tn), lambda j: (0, j)),
        out_shape=jax.ShapeDtypeStruct((m, n), F32),
        compiler_params=_cparams("arbitrary"),
        name="ada_mod",
    )(cc, w, b)


def _rope(x, cos, sinn, sinp):
    outs = []
    for j in range(x.shape[1] // LANES):
        xg = x[:, j * LANES:(j + 1) * LANES]
        outs.append(xg * cos + pltpu.roll(xg, LANES - 16, 1) * sinn + pltpu.roll(xg, 16, 1) * sinp)
    return outs[0] if len(outs) == 1 else jnp.concatenate(outs, axis=1)


def _head_rms(x, seg, gain):
    outs = []
    for j in range(x.shape[1] // LANES):
        xg = x[:, j * LANES:(j + 1) * LANES]
        sq = xg * xg
        hi = sq.astype(BF16)
        lo = (sq - hi.astype(F32)).astype(BF16)
        ms = (jnp.dot(hi, seg, preferred_element_type=F32)
              + jnp.dot(lo, seg, preferred_element_type=F32)) * (1.0 / HEAD_DIM)
        outs.append(xg * lax.rsqrt(ms + LN_EPS) * gain[:, j * LANES:(j + 1) * LANES])
    return outs[0] if len(outs) == 1 else jnp.concatenate(outs, axis=1)


def _in_proj_kernel(*refs, rope):
    if rope:
        (x_ref, mod_ref, lng_ref, lnb_ref, w_ref, seg_ref, qg_ref, kg_ref, cos_ref, sinn_ref, sinp_ref,
         rq_ref, rkt_ref, rv_ref, rg_ref, aq_ref, akt_ref, av_ref) = refs
    else:
        (x_ref, mod_ref, lng_ref, lnb_ref, w_ref, seg_ref, qg_ref, kg_ref,
         rq_ref, rkt_ref, rv_ref, rg_ref, aq_ref, akt_ref, av_ref) = refs
    x = x_ref[0]
    xn = _layer_norm(x, lng_ref[...], lnb_ref[...])
    h = (xn * (1.0 + mod_ref[0, 1:2, :]) + mod_ref[0, 0:1, :]).astype(BF16)
    seg = seg_ref[...]

    def rotate(v):
        return _rope(v, cos_ref[...], sinn_ref[...], sinp_ref[...]) if rope else v

    o_rv = 2 * RET_Q_W
    o_aq = o_rv + 2 * RET_V_W
    o_ak = o_aq + ATT_Q_W
    o_end = o_ak + 2 * ATT_K_W

    def project(lo, hi):
        return jnp.dot(h, w_ref[:, lo:hi], preferred_element_type=F32)

    def finish_aq(u):
        aq = rotate(_head_rms(u, seg, qg_ref[...]))
        aq_ref[0] = (aq * (HEAD_DIM ** -0.5 * LOG2_E)).astype(BF16)

    def finish_akv(u):
        ak = rotate(_head_rms(u[:, :ATT_K_W], seg, kg_ref[...]))
        akt_ref[0] = ak.T.astype(BF16)
        av_ref[0] = u[:, ATT_K_W:].astype(BF16)

    def finish_rqk(u):
        rq_ref[0] = rotate(u[:, :RET_Q_W])
        rkt_ref[0] = (rotate(u[:, RET_Q_W:]) * (RET_DK ** -0.5)).T

    def finish_rvg(u):
        rv_ref[0] = u[:, :RET_V_W].astype(BF16)
        rg_ref[0] = u[:, RET_V_W:].astype(BF16)

    stages = [((o_aq, o_ak), finish_aq), ((o_ak, o_end), finish_akv),
              ((0, o_rv), finish_rqk), ((o_rv, o_aq), finish_rvg)]
    u_next = project(*stages[0][0])
    for idx, (_, finish) in enumerate(stages):
        u_cur = u_next
        if idx + 1 < len(stages):
            u_next = project(*stages[idx + 1][0])
        finish(u_cur)


def _in_proj(x, mod, lng, lnb, w_bf, seg, qg, kg, tables):
    b, t, d = x.shape
    tt = min(TOKEN_TILE, t)
    rope = tables is not None
    mod_b = mod.shape[0]
    mod_map = (lambda bi, i: (bi, 0, 0)) if mod_b > 1 else (lambda bi, i: (0, 0, 0))
    const2 = lambda bi, i: (0, 0)
    tok3 = lambda bi, i: (bi, i, 0)
    in_specs = [pl.BlockSpec((1, tt, d), tok3),
                pl.BlockSpec((1, 6, d), mod_map),
                pl.BlockSpec((1, d), const2),
                pl.BlockSpec((1, d), const2),
                pl.BlockSpec(w_bf.shape, const2),
                pl.BlockSpec(seg.shape, const2),
                pl.BlockSpec(qg.shape, const2),
                pl.BlockSpec(kg.shape, const2)]
    args = [x, mod, lng, lnb, w_bf, seg, qg, kg]
    if rope:
        in_specs += [pl.BlockSpec((tt, LANES), lambda bi, i: (i, 0))] * 3
        args += list(tables)
    out_shape = [jax.ShapeDtypeStruct((b, t, RET_Q_W), F32),
                 jax.ShapeDtypeStruct((b, RET_Q_W, t), F32),
                 jax.ShapeDtypeStruct((b, t, RET_V_W), BF16),
                 jax.ShapeDtypeStruct((b, t, RET_V_W), BF16),
                 jax.ShapeDtypeStruct((b, t, ATT_Q_W), BF16),
                 jax.ShapeDtypeStruct((b, ATT_K_W, t), BF16),
                 jax.ShapeDtypeStruct((b, t, ATT_K_W), BF16)]
    out_specs = [pl.BlockSpec((1, tt, RET_Q_W), tok3),
                 pl.BlockSpec((1, RET_Q_W, tt), lambda bi, i: (bi, 0, i)),
                 pl.BlockSpec((1, tt, RET_V_W), tok3),
                 pl.BlockSpec((1, tt, RET_V_W), tok3),
                 pl.BlockSpec((1, tt, ATT_Q_W), tok3),
                 pl.BlockSpec((1, ATT_K_W, tt), lambda bi, i: (bi, 0, i)),
                 pl.BlockSpec((1, tt, ATT_K_W), tok3)]
    return pl.pallas_call(
        functools.partial(_in_proj_kernel, rope=rope),
        grid=(b, t // tt),
        in_specs=in_specs,
        out_specs=out_specs,
        out_shape=out_shape,
        compiler_params=_cparams("parallel", "parallel"),
        name="in_proj_rope" if rope else "in_proj_ctx",
    )(*args)


def _retention_kernel(lgf_ref, lgb_ref, q_ref, kt_ref, v_ref, g_ref, ktc_ref, vc_ref, gng_ref, gnb_ref,
                      o_ref, kv_ref, rhs_ref):
    pair = pl.program_id(1)
    t = q_ref.shape[1]
    n_chunks = t // RET_CHUNK
    ctx_len = ktc_ref.shape[2]
    c = RET_CHUNK
    dk = RET_DK
    pos_col = lax.broadcasted_iota(jnp.int32, (c, 1), 0).astype(F32)
    pos_row = lax.broadcasted_iota(jnp.int32, (1, c), 1).astype(F32)
    row = lax.broadcasted_iota(jnp.int32, (c, c), 0)
    col = lax.broadcasted_iota(jnp.int32, (c, c), 1)
    diff = (row - col).astype(F32)
    lane = lax.broadcasted_iota(jnp.int32, (c, 2 * dk), 1)
    cpos = lax.broadcasted_iota(jnp.int32, (1, ctx_len), 1).astype(F32)

    heads = []
    for hh in range(2):
        head = pair * 2 + hh
        lgf = lgf_ref[0, head]
        lgb = lgb_ref[0, head]
        heads.append(dict(
            kl=slice(hh * dk, (hh + 1) * dk),
            vl=slice(hh * RET_DV, (hh + 1) * RET_DV),
            lgf=lgf, lgb=lgb,
            kdf=jnp.exp(lgf * (c - 1.0 - pos_row)),
            kdb=jnp.exp(lgb * pos_row),
            qd=jnp.where(lane < dk, jnp.exp(lgf * (pos_col + 1.0)), jnp.exp(lgb * (c - pos_col))),
            decay=(jnp.where(diff >= 0, jnp.exp(lgf * jnp.maximum(diff, 0.0)), 0.0)
                   + jnp.where(diff <= 0, jnp.exp(lgb * jnp.maximum(-diff, 0.0)), 0.0))))

    def kv_body(n, carry):
        r = pl.ds(pl.multiple_of(n * c, c), c)
        for hh, hd in enumerate(heads):
            kt = kt_ref[0, hd["kl"], r]
            vc = v_ref[0, r, hd["vl"]]
            both = jnp.concatenate([kt * hd["kdf"], kt * hd["kdb"]], axis=0).astype(BF16)
            kv_ref[hh, n] = jnp.dot(both, vc, preferred_element_type=F32)
            rhs_ref[hh, n, 0:c, :] = vc
        return carry

    lax.fori_loop(0, n_chunks, kv_body, 0, unroll=4)

    for hh, hd in enumerate(heads):
        lgf, lgb = hd["lgf"], hd["lgb"]
        ktc = ktc_ref[0, hd["kl"], :]
        both = jnp.concatenate([ktc * jnp.exp(lgf * (ctx_len - 1.0 - cpos)), ktc * jnp.exp(lgb * cpos)],
                               axis=0).astype(BF16)
        s_ctx = jnp.dot(both, vc_ref[0, :, hd["vl"]], preferred_element_type=F32)
        s_f, s_b = s_ctx[0:dk], s_ctx[dk:]
        gfc = jnp.exp(lgf * c)
        gbc = jnp.exp(lgb * c)
        for n in range(n_chunks):
            rhs_ref[hh, n, c:c + dk, :] = s_f.astype(BF16)
            s_f = gfc * s_f + kv_ref[hh, n, 0:dk, :]
        for n in range(n_chunks - 1, -1, -1):
            rhs_ref[hh, n, c + dk:, :] = s_b.astype(BF16)
            s_b = gbc * s_b + kv_ref[hh, n, dk:, :]

    def out_body(n, carry):
        r = pl.ds(pl.multiple_of(n * c, c), c)
        qq = q_ref[0, r, :]
        swapped = pltpu.roll(qq, dk, 1)
        for hh, hd in enumerate(heads):
            vl = hd["vl"]
            s = jnp.dot(q_ref[0, r, hd["kl"]].astype(BF16), kt_ref[0, hd["kl"], r].astype(BF16),
                        preferred_element_type=F32)
            q2 = jnp.where(lane < dk, qq, swapped) if hh == 0 else jnp.where(lane < dk, swapped, qq)
            lhs = jnp.concatenate([(s * hd["decay"]).astype(BF16), (q2 * hd["qd"]).astype(BF16)], axis=1)
            y = jnp.dot(lhs, rhs_ref[hh, n], preferred_element_type=F32)
            mu = jnp.mean(y, -1, keepdims=True)
            yc = y - mu
            var = jnp.mean(yc * yc, -1, keepdims=True)
            yn = yc * lax.rsqrt(var + LN_EPS) * gng_ref[:, vl] + gnb_ref[:, vl]
            o_ref[0, r, vl] = (yn * _silu(g_ref[0, r, vl].astype(F32))).astype(o_ref.dtype)
        return carry

    lax.fori_loop(0, n_chunks, out_body, 0, unroll=8)


def _retention(lgf, lgb, rq, rkt, rv, rg, rktc, rvc, gng, gnb):
    b, t, _ = rq.shape
    ctx_len = rktc.shape[2]
    n_chunks = t // RET_CHUNK
    pair3 = lambda bi, p: (bi, 0, p)
    pair_rows = lambda bi, p: (bi, p, 0)
    smem = pl.BlockSpec(memory_space=pltpu.SMEM)
    return pl.pallas_call(
        _retention_kernel,
        grid=(b, RET_HEADS // 2),
        in_specs=[smem, smem,
                  pl.BlockSpec((1, t, 2 * RET_DK), pair3),
                  pl.BlockSpec((1, 2 * RET_DK, t), pair_rows),
                  pl.BlockSpec((1, t, 2 * RET_DV), pair3),
                  pl.BlockSpec((1, t, 2 * RET_DV), pair3),
                  pl.BlockSpec((1, 2 * RET_DK, ctx_len), pair_rows),
                  pl.BlockSpec((1, ctx_len, 2 * RET_DV), pair3),
                  pl.BlockSpec((1, 2 * RET_DV), lambda bi, p: (0, p)),
                  pl.BlockSpec((1, 2 * RET_DV), lambda bi, p: (0, p))],
        out_specs=pl.BlockSpec((1, t, 2 * RET_DV), pair3),
        out_shape=jax.ShapeDtypeStruct((b, t, RET_V_W), BF16),
        scratch_shapes=[pltpu.VMEM((2, n_chunks, 2 * RET_DK, RET_DV), F32),
                        pltpu.VMEM((2, n_chunks, RET_CHUNK + 2 * RET_DK, RET_DV), BF16)],
        compiler_params=_cparams("parallel", "parallel"),
        name="retention",
    )(lgf, lgb, rq, rkt, rv, rg, rktc, rvc, gng, gnb)


def _attention_kernel(q_ref, kt_ref, v_ref, o_ref, s_buf, p_buf):
    ts = ATT_Q_SUB
    lane = lax.broadcasted_iota(jnp.int32, (ts, LANES), 1)
    pieces = [(sub, kvh) for sub in range(q_ref.shape[1] // ts) for kvh in range(ATT_KV_HEADS)]

    def scores(i):
        sub, kvh = pieces[i]
        q = jnp.concatenate(
            [q_ref[0, sub * ts:(sub + 1) * ts, (kvh * ATT_GROUP + g) * HEAD_DIM:(kvh * ATT_GROUP + g + 1) * HEAD_DIM]
             for g in range(ATT_GROUP)], axis=0)
        s_buf[i % 2] = jnp.dot(q, kt_ref[0, kvh * HEAD_DIM:(kvh + 1) * HEAD_DIM, :], preferred_element_type=F32)

    def numerators(i):
        s = s_buf[i % 2]
        top = jnp.max(s, -1, keepdims=True) - ATT_P_SHIFT
        p_buf[i % 2] = jnp.exp2(s - top).astype(jnp.float8_e4m3fn)

    def values(i):
        sub, kvh = pieces[i]
        ov = jnp.dot(p_buf[i % 2], v_ref[0], preferred_element_type=F32)
        o = ov[:, :LANES] / ov[:, LANES:LANES + 1]
        for pair in range(ATT_GROUP // 2):
            halves = []
            for half in range(2):
                g = pair * 2 + half
                og = o[g * ts:(g + 1) * ts, :]
                if kvh != half:
                    og = pltpu.roll(og, HEAD_DIM, 1)
                halves.append(og)
            col = (kvh * ATT_GROUP // 2 + pair) * LANES
            o_ref[0, sub * ts:(sub + 1) * ts, col:col + LANES] = jnp.where(
                lane < HEAD_DIM, halves[0], halves[1]).astype(o_ref.dtype)

    scores(0)
    scores(1)
    numerators(0)
    for i in range(len(pieces)):
        if i + 2 < len(pieces):
            scores(i + 2)
        if i + 1 < len(pieces):
            numerators(i + 1)
        values(i)


def _attention(aq, akt, av_ones):
    b, t, _ = aq.shape
    tk = akt.shape[2]
    tq = ATT_Q_TILE
    return pl.pallas_call(
        _attention_kernel,
        grid=(b, t // tq),
        in_specs=[pl.BlockSpec((1, tq, ATT_Q_W), lambda bi, i: (bi, i, 0)),
                  pl.BlockSpec((1, ATT_K_W, tk), lambda bi, i: (bi, 0, 0)),
                  pl.BlockSpec((1, tk, av_ones.shape[2]), lambda bi, i: (bi, 0, 0))],
        out_specs=pl.BlockSpec((1, tq, ATT_Q_W), lambda bi, i: (bi, i, 0)),
        out_shape=jax.ShapeDtypeStruct((b, t, ATT_Q_W), BF16),
        scratch_shapes=[pltpu.VMEM((2, ATT_GROUP * ATT_Q_SUB, tk), F32),
                        pltpu.VMEM((2, ATT_GROUP * ATT_Q_SUB, tk), jnp.float8_e4m3fn)],
        compiler_params=_cparams("parallel", "parallel"),
        name="attention",
    )(aq, akt, av_ones)


def _post_mixer_kernel(ret_ref, att_ref, x_ref, mod_ref, lng_ref, lnb_ref, wo_ref, l1g_ref, l1b_ref,
                       rwh_ref, rwl_ref, rb_ref, before_ref, below_ref,
                       x1_ref, h2_ref, gate_ref, lrow_ref, table_ref, base_acc):
    first = (pl.program_id(0) == 0) & (pl.program_id(1) == 0)

    @pl.when(first)
    def _():
        base_acc[...] = jnp.zeros_like(base_acc)

    half = ret_ref.shape[2]
    y = (jnp.dot(ret_ref[0], wo_ref[0:half, :], preferred_element_type=F32)
         + jnp.dot(att_ref[0], wo_ref[half:, :], preferred_element_type=F32))
    xn = _layer_norm(x_ref[0], lng_ref[...], lnb_ref[...])
    x1 = _layer_norm(DN_ALPHA * xn + mod_ref[0, 2:3, :] * y, l1g_ref[...], l1b_ref[...])
    x1_ref[...] = x1
    h2 = x1 * (1.0 + mod_ref[0, 4:5, :]) + mod_ref[0, 3:4, :]
    h_hi = h2.astype(BF16)
    h2_ref[...] = h_hi

    h_lo = (h2 - h_hi.astype(F32)).astype(BF16)
    logits = (jnp.dot(h_hi, rwh_ref[...], preferred_element_type=F32)
              + jnp.dot(h_lo, rwh_ref[...], preferred_element_type=F32)
              + jnp.dot(h_hi, rwl_ref[...], preferred_element_type=F32)) + rb_ref[...]

    tt = logits.shape[0]
    work = logits.T[0:N_EXPERTS, :]
    e_iota = lax.broadcasted_iota(jnp.int32, (N_EXPERTS, tt), 0)
    sels, vals = [], []
    onehot = jnp.zeros((N_EXPERTS, tt), F32)
    for _ in range(TOP_K):
        m = jnp.max(work, 0, keepdims=True)
        idx = jnp.min(jnp.where(work == m, e_iota, N_EXPERTS), 0, keepdims=True)
        sel = e_iota == idx
        sels.append(sel)
        vals.append(m)
        onehot = onehot + sel.astype(F32)
        work = jnp.where(sel, -jnp.inf, work)
    exps = [jnp.exp(v - vals[0]) for v in vals]
    denom = exps[0] + exps[1] + exps[2] + exps[3]

    count = jnp.sum(onehot, 1, keepdims=True)
    base = base_acc[...]
    phi = base - SEG_ALIGN * jnp.floor(base * (1.0 / SEG_ALIGN))
    units = jnp.where(count > 0, jnp.floor((phi + count + (SEG_ALIGN - 1.0)) * (1.0 / SEG_ALIGN)), 0.0)
    lstart = SEG_ALIGN * jnp.dot(below_ref[...], jnp.broadcast_to(units, (N_EXPERTS, LANES)).astype(BF16),
                                 preferred_element_type=F32)[:, 0:1]
    rank = jnp.dot(onehot.astype(BF16), before_ref[...], preferred_element_type=F32) + (lstart + phi)
    end = phi + count

    k_iota = lax.broadcasted_iota(jnp.int32, (TOP_K, tt), 0)
    gate_out = jnp.zeros((TOP_K, tt), F32)
    lrow_out = jnp.zeros((TOP_K, tt), F32)
    for k in range(TOP_K):
        pk = jnp.sum(jnp.where(sels[k], rank, 0.0), 0, keepdims=True)
        gate_out = jnp.where(k_iota == k, exps[k] / denom, gate_out)
        lrow_out = jnp.where(k_iota == k, pk, lrow_out)
    gate_ref[0] = gate_out
    lrow_ref[0] = lrow_out.astype(jnp.int32)
    columns = [units * SEG_ALIGN, lstart, base - phi, jnp.where(count > 0, phi, 0.0),
               jnp.where(count > 0, end - SEG_ALIGN * jnp.floor(end * (1.0 / SEG_ALIGN)), 0.0)]
    c_iota = lax.broadcasted_iota(jnp.int32, (N_EXPERTS, TABLE_COLUMNS), 1)
    table = jnp.zeros((N_EXPERTS, TABLE_COLUMNS), F32)
    for k, column in enumerate(columns):
        table = jnp.where(c_iota == k, column, table)
    table_ref[0] = table.astype(jnp.int32)
    base_acc[...] = base + count


def _post_mixer(ret, att, x, mod, lng, lnb, wo_bf, l1g, l1b, rw_hi, rw_lo, rb, before, below):
    b, t, d = x.shape
    tt = TOKEN_TILE
    nt = t // tt
    n = b * t
    const2 = lambda bi, i: (0, 0)
    tok3 = lambda bi, i: (bi, i, 0)
    flat = lambda bi, i: (bi * nt + i, 0)
    tile3 = lambda bi, i: (bi * nt + i, 0, 0)
    vec = pl.BlockSpec((1, d), const2)
    table = jax.ShapeDtypeStruct((b * nt, N_EXPERTS, TABLE_COLUMNS), jnp.int32)
    table_spec = pl.BlockSpec((1, N_EXPERTS, TABLE_COLUMNS), tile3)
    per_k = pl.BlockSpec((1, TOP_K, tt), tile3)
    return pl.pallas_call(
        _post_mixer_kernel,
        grid=(b, nt),
        in_specs=[pl.BlockSpec((1, tt, ret.shape[2]), tok3),
                  pl.BlockSpec((1, tt, att.shape[2]), tok3),
                  pl.BlockSpec((1, tt, d), tok3),
                  pl.BlockSpec((1, 6, d), lambda bi, i: (bi, 0, 0)),
                  vec, vec,
                  pl.BlockSpec(wo_bf.shape, const2),
                  vec, vec,
                  pl.BlockSpec(rw_hi.shape, const2),
                  pl.BlockSpec(rw_lo.shape, const2),
                  pl.BlockSpec(rb.shape, const2),
                  pl.BlockSpec((tt, tt), const2),
                  pl.BlockSpec((N_EXPERTS, N_EXPERTS), const2)],
        out_specs=[pl.BlockSpec((tt, d), flat),
                   pl.BlockSpec((tt, d), flat),
                   per_k, per_k, table_spec],
        out_shape=[jax.ShapeDtypeStruct((n, d), F32),
                   jax.ShapeDtypeStruct((n, d), BF16),
                   jax.ShapeDtypeStruct((b * nt, TOP_K, tt), F32),
                   jax.ShapeDtypeStruct((b * nt, TOP_K, tt), jnp.int32),
                   table],
        scratch_shapes=[pltpu.VMEM((N_EXPERTS, 1), F32)],
        compiler_params=_cparams("arbitrary", "arbitrary"),
        name="post_mixer",
    )(ret, att, x, mod, lng, lnb, wo_bf, l1g, l1b, rw_hi, rw_lo, rb, before, below)


def _segment_starts(cnt_ref, lstart_ref, dst_ref, tile, make_copy):
    def body(e, carry):
        t = tile * N_EXPERTS + e
        n = cnt_ref[t]

        @pl.when(n > 0)
        def _():
            make_copy(lstart_ref[t], dst_ref[t], n).start()
        return carry
    lax.fori_loop(0, N_EXPERTS, body, 0, unroll=4)


def _segment_waits(total_ref, tile, make_copy):
    make_copy(0, 0, total_ref[tile]).wait()


def _zero_fill_copies(tail_len_ref, tail_dst_ref, nact_ref, zbuf, x_hbm, zsem, act):
    block = zbuf.shape[0]

    def tail_body(e, carry):
        n = tail_len_ref[e]

        @pl.when(n > 0)
        def _():
            copy = pltpu.make_async_copy(zbuf.at[pl.ds(0, n)], x_hbm.at[pl.ds(tail_dst_ref[e], n)], zsem)
            getattr(copy, act)()
        return carry
    lax.fori_loop(0, N_EXPERTS, tail_body, 0)

    def block_body(j, carry):
        copy = pltpu.make_async_copy(zbuf, x_hbm.at[pl.ds(j * block, block)], zsem)
        getattr(copy, act)()
        return carry
    lax.fori_loop(nact_ref[0], x_hbm.shape[0] // block, block_body, 0)


def _dispatch_kernel(cnt_ref, ncopy_ref, lstart_ref, dst_ref, total_ref, extent_ref, rem_ref, flush_ref,
                     flush_dst_ref, tail_len_ref, tail_dst_ref, nact_ref,
                     h_ref, lrow_ref, x_hbm, sbuf, sem, zbuf, zsem, carry):
    i = pl.program_id(0)
    last = pl.num_programs(0) - 1
    slot = i % 2
    rows = sbuf.shape[1] * SEG_ALIGN
    tt = h_ref.shape[0]
    d = h_ref.shape[1]
    chunk_groups = PERM_CHUNK // SEG_ALIGN

    def make_copy(tile_slot):
        def mk(ls, ds, size):
            return pltpu.make_async_copy(sbuf.at[tile_slot, pl.ds(ls, size)], x_hbm.at[pl.ds(ds, size)],
                                         sem.at[tile_slot])
        return mk

    @pl.when(i == 0)
    def _():
        zbuf[...] = jnp.zeros_like(zbuf)
        carry[...] = jnp.zeros_like(carry)
        _zero_fill_copies(tail_len_ref, tail_dst_ref, nact_ref, zbuf, x_hbm, zsem, "start")

    def perm(j):
        r_iota = lax.broadcasted_iota(jnp.int32, (PERM_CHUNK, tt), 0) + j * PERM_CHUNK
        hit = r_iota == lrow_ref[0, 0:1, :]
        for k in range(1, TOP_K):
            hit = hit | (r_iota == lrow_ref[0, k:k + 1, :])
        return jnp.where(hit, 1.0, 0.0).astype(BF16)

    def place(j, p):
        sbuf[slot, j * chunk_groups:(j + 1) * chunk_groups] = jnp.dot(
            p, h_ref[...], preferred_element_type=F32).astype(BF16).reshape(chunk_groups, SEG_ALIGN, d)

    n_chunks = rows // PERM_CHUNK
    p_next = perm(0)
    for j in range(n_chunks - 1):
        p_cur = p_next
        if j + 2 < n_chunks:
            p_next = perm(j + 1)
        place(j, p_cur)

    @pl.when(extent_ref[i] > (n_chunks - 1) * chunk_groups)
    def _():
        place(n_chunks - 1, perm(n_chunks - 1))
    def merge_body(e, c):
        t = i * N_EXPERTS + e
        n = cnt_ref[t]
        first = lstart_ref[t]
        present = n > 0
        old = carry[e]
        sbuf[slot, first] = sbuf[slot, first] + jnp.where(present, old, jnp.zeros_like(old))
        tail = sbuf[slot, jnp.maximum(first + n - 1, 0)]
        carry[e] = jnp.where(present, jnp.where(rem_ref[t] > 0, tail, jnp.zeros_like(tail)), old)
        return c
    lax.fori_loop(0, N_EXPERTS, merge_body, 0, unroll=4)

    _segment_starts(ncopy_ref, lstart_ref, dst_ref, i, make_copy(slot))

    @pl.when(i > 0)
    def _():
        _segment_waits(total_ref, i - 1, make_copy(1 - slot))

    def flush_copies(act):
        def body(e, c):
            @pl.when(flush_ref[e] > 0)
            def _():
                copy = pltpu.make_async_copy(carry.at[pl.ds(e, 1)], x_hbm.at[pl.ds(flush_dst_ref[e], 1)], zsem)
                getattr(copy, act)()
            return c
        lax.fori_loop(0, N_EXPERTS, body, 0)

    @pl.when(i == last)
    def _():
        flush_copies("start")
        _segment_waits(total_ref, i, make_copy(slot))
        _zero_fill_copies(tail_len_ref, tail_dst_ref, nact_ref, zbuf, x_hbm, zsem, "wait")
        flush_copies("wait")


def _dispatch(cnt, ncopy, lstart, dst, total, extent, rem, flush, flush_dst, tail_len, tail_dst, n_active,
              h2, lrow_t, n_rows):
    n, d = h2.shape
    tt = TOKEN_TILE
    grid_spec = pltpu.PrefetchScalarGridSpec(
        num_scalar_prefetch=12,
        grid=(n // tt,),
        in_specs=[pl.BlockSpec((tt, d), lambda i, *_: (i, 0)),
                  pl.BlockSpec((1, TOP_K, tt), lambda i, *_: (i, 0, 0))],
        out_specs=pl.BlockSpec(memory_space=pl.ANY),
        scratch_shapes=[pltpu.VMEM((2, TILE_SORTED_ROWS // SEG_ALIGN, SEG_ALIGN, d), BF16),
                        pltpu.SemaphoreType.DMA((2,)),
                        pltpu.VMEM((EXPERT_ROWS // SEG_ALIGN, SEG_ALIGN, d), BF16),
                        pltpu.SemaphoreType.DMA(()),
                        pltpu.VMEM((N_EXPERTS, SEG_ALIGN, d), BF16)])
    return pl.pallas_call(
        _dispatch_kernel,
        grid_spec=grid_spec,
        out_shape=jax.ShapeDtypeStruct((n_rows // SEG_ALIGN, SEG_ALIGN, d), BF16),
        compiler_params=_cparams("arbitrary"),
        name="dispatch",
    )(cnt, ncopy, lstart, dst, total, extent, rem, flush, flush_dst, tail_len, tail_dst, n_active, h2, lrow_t)


def _experts_kernel(be_ref, nact_ref, x_ref, wup_ref, bup_ref, wdn_ref, bdn_ref, y_ref,
                    wup_bf, wdn_bf):
    j = pl.program_id(0)
    dff = wdn_ref.shape[1]

    @pl.when(j < nact_ref[0])
    def _():
        prev = be_ref[jnp.maximum(j - 1, 0)]

        @pl.when((j == 0) | (be_ref[j] != prev))
        def _():
            wup_bf[...] = wup_ref[0].astype(BF16)
            wdn_bf[...] = wdn_ref[0].astype(BF16)

        x = x_ref[...].reshape(-1, x_ref.shape[2])
        u = jnp.dot(x, wup_bf[...], preferred_element_type=F32) + bup_ref[0]
        glu = jnp.minimum(u[:, :dff], SWIGLU_LIMIT)
        lin = jnp.clip(u[:, dff:], -SWIGLU_LIMIT, SWIGLU_LIMIT)
        act = glu * (1.0 / (1.0 + jnp.exp(-SWIGLU_ALPHA * glu))) * (lin + 1.0)
        y = jnp.dot(act.astype(BF16), wdn_bf[...], preferred_element_type=F32) + bdn_ref[0]
        y_ref[...] = y.astype(y_ref.dtype).reshape(y_ref.shape)

    @pl.when(j >= nact_ref[0])
    def _():
        y_ref[...] = jnp.zeros_like(y_ref)


def _experts(block_expert, n_active, x_sorted, w_up, b_up, w_down, b_down):
    n_blocks = block_expert.shape[0]
    tm = EXPERT_ROWS
    d = x_sorted.shape[2]
    e, _, up_w = w_up.shape
    dff = w_down.shape[1]
    by_expert = lambda j, be, na: (be[j], 0, 0)
    x_map = lambda j, be, na: (jnp.minimum(j, na[0] - 1), 0, 0)
    rows3 = (tm // SEG_ALIGN, SEG_ALIGN, d)
    grid_spec = pltpu.PrefetchScalarGridSpec(
        num_scalar_prefetch=2,
        grid=(n_blocks,),
        in_specs=[pl.BlockSpec(rows3, x_map),
                  pl.BlockSpec((1, d, up_w), by_expert),
                  pl.BlockSpec((1, 1, up_w), by_expert),
                  pl.BlockSpec((1, dff, d), by_expert),
                  pl.BlockSpec((1, 1, d), by_expert)],
        out_specs=pl.BlockSpec(rows3, lambda j, be, na: (j, 0, 0)),
        scratch_shapes=[pltpu.VMEM((d, up_w), BF16),
                        pltpu.VMEM((dff, d), BF16)])
    return pl.pallas_call(
        _experts_kernel,
        grid_spec=grid_spec,
        out_shape=jax.ShapeDtypeStruct(x_sorted.shape, BF16),
        compiler_params=_cparams("arbitrary"),
        name="experts",
    )(block_expert, n_active, x_sorted, w_up, b_up.reshape(e, 1, up_w), w_down,
      b_down.reshape(e, 1, d))


def _combine_kernel(cnt_ref, lstart_ref, dst_ref, total_ref, lrow_ref, gate_ref, x1_ref, mod_ref, g_ref, b_ref,
                    y_hbm, o_ref, ybuf, sem):
    i = pl.program_id(0)
    last = pl.num_programs(0) - 1
    slot = i % 2
    rows = ybuf.shape[1] * SEG_ALIGN
    tt = x1_ref.shape[0]
    d = x1_ref.shape[1]
    chunk_groups = PERM_CHUNK // SEG_ALIGN

    def make_copy(tile_slot):
        def mk(ls, ds, size):
            return pltpu.make_async_copy(y_hbm.at[pl.ds(ds, size)], ybuf.at[tile_slot, pl.ds(ls, size)],
                                         sem.at[tile_slot])
        return mk

    @pl.when(i == 0)
    def _():
        ybuf[...] = jnp.zeros_like(ybuf)
        _segment_starts(cnt_ref, lstart_ref, dst_ref, i, make_copy(slot))

    @pl.when(i < last)
    def _():
        _segment_starts(cnt_ref, lstart_ref, dst_ref, i + 1, make_copy(1 - slot))

    def weights(j):
        c_iota = lax.broadcasted_iota(jnp.int32, (tt, PERM_CHUNK), 1) + j * PERM_CHUNK
        w = jnp.zeros((tt, PERM_CHUNK), F32)
        for k in range(TOP_K):
            w = jnp.where(c_iota == lrow_ref[:, k:k + 1], gate_ref[:, k:k + 1], w)
        return w.astype(BF16)

    def gather(j, w):
        y_chunk = ybuf[slot, j * chunk_groups:(j + 1) * chunk_groups].reshape(PERM_CHUNK, d)
        return jnp.dot(w, y_chunk, preferred_element_type=F32)

    def finish(f):
        o_ref[...] = _layer_norm(DN_ALPHA * x1_ref[...] + mod_ref[0, 5:6, :] * f, g_ref[...], b_ref[...])

    n_chunks = rows // PERM_CHUNK
    w_next = weights(0)
    _segment_waits(total_ref, i, make_copy(slot))
    f = None
    for j in range(n_chunks - 1):
        w_cur = w_next
        if j + 2 < n_chunks:
            w_next = weights(j + 1)
        part = gather(j, w_cur)
        f = part if f is None else f + part

    need_last = total_ref[i] > (n_chunks - 1) * chunk_groups

    @pl.when(need_last)
    def _():
        finish(f + gather(n_chunks - 1, weights(n_chunks - 1)))

    @pl.when(jnp.logical_not(need_last))
    def _():
        finish(f)


def _combine(cnt, lstart, dst, total, lrow, gates, x1, mod, g, b, y_sorted, tokens_per_sample):
    n, d = x1.shape
    tt = TOKEN_TILE
    tiles_per_sample = tokens_per_sample // tt
    const2 = lambda i, *_: (0, 0)
    tok = lambda i, *_: (i, 0)
    grid_spec = pltpu.PrefetchScalarGridSpec(
        num_scalar_prefetch=4,
        grid=(n // tt,),
        in_specs=[pl.BlockSpec((tt, TOP_K), tok),
                  pl.BlockSpec((tt, TOP_K), tok),
                  pl.BlockSpec((tt, d), tok),
                  pl.BlockSpec((1, 6, d), lambda i, *_: (i // tiles_per_sample, 0, 0)),
                  pl.BlockSpec((1, d), const2),
                  pl.BlockSpec((1, d), const2),
                  pl.BlockSpec(memory_space=pl.ANY)],
        out_specs=pl.BlockSpec((tt, d), tok),
        scratch_shapes=[pltpu.VMEM((2, TILE_SORTED_ROWS // SEG_ALIGN, SEG_ALIGN, d), BF16),
                        pltpu.SemaphoreType.DMA((2,))])
    return pl.pallas_call(
        _combine_kernel,
        grid_spec=grid_spec,
        out_shape=jax.ShapeDtypeStruct((n, d), F32),
        compiler_params=_cparams("arbitrary"),
        name="combine",
    )(cnt, lstart, dst, total, lrow, gates, x1, mod, g, b, y_sorted)


def _rope_tables(t):
    rows = jnp.repeat(jnp.arange(t // GRID_W, dtype=jnp.int32), GRID_W).astype(F32)
    cols = jnp.tile(jnp.arange(GRID_W, dtype=jnp.int32), t // GRID_W).astype(F32)
    n_freq = HEAD_DIM // 4
    inv = ROPE_BASE ** (-jnp.arange(n_freq, dtype=F32) / n_freq)
    ang_r = rows[:, None] * inv
    ang_c = cols[:, None] * inv
    ang = jnp.concatenate([ang_r, ang_r, ang_c, ang_c], -1)
    ang = jnp.concatenate([ang, ang], -1)
    cos, sin = jnp.cos(ang), jnp.sin(ang)
    first_half = (jnp.arange(LANES) % 32) < 16
    return cos, jnp.where(first_half, -sin, 0.0), jnp.where(first_half, 0.0, sin)


def kernel(x, c, ctx, c_ctx, ln_in_g, ln_in_b, w_ada, b_ada, w_in, ret_log_decay_f, ret_log_decay_b,
           ret_gn_g, ret_gn_b, q_norm_g, k_norm_g, w_o, ln1_g, ln1_b, router_w, router_b,
           w_up, b_up, w_down, b_down, ln2_g, ln2_b):
    b, t, d = x.shape
    n = b * t
    row = lambda v: v.reshape(1, -1)

    cc = jnp.concatenate([c, c_ctx[None]], 0)
    cc = jnp.pad(cc, ((0, (-cc.shape[0]) % 8), (0, 0)))
    mod_all = _ada_mod(cc, w_ada[0], row(b_ada[0]))
    mod = mod_all[:b].reshape(b, 6, d)
    mod_ctx = mod_all[b:b + 1].reshape(1, 6, d)

    w_in_bf = w_in[0].astype(BF16)
    lane = jnp.arange(LANES)
    seg = (lane[:, None] // HEAD_DIM == lane[None, :] // HEAD_DIM).astype(BF16)
    qg = jnp.tile(q_norm_g[0], ATT_HEADS).reshape(1, -1)
    kg = jnp.tile(k_norm_g[0], ATT_KV_HEADS).reshape(1, -1)
    lng, lnb = row(ln_in_g), row(ln_in_b)

    rq, rkt, rv, rg, aq, akt, av = _in_proj(x, mod, lng, lnb, w_in_bf, seg, qg, kg, _rope_tables(t))
    _, rktc, rvc, _, _, aktc, avc = _in_proj(ctx, mod_ctx, lng, lnb, w_in_bf, seg, qg, kg, None)

    ret = _retention(row(ret_log_decay_f[0]), row(ret_log_decay_b[0]), rq, rkt, rv, rg, rktc, rvc,
                     row(ret_gn_g[0]), row(ret_gn_b[0]))
    tk = ctx.shape[1] + t
    av_ones = jnp.concatenate([jnp.concatenate([avc, av], 1), jnp.ones((b, tk, LANES), BF16)], 2)
    att = _attention(aq, jnp.concatenate([aktc, akt], 2), av_ones.astype(jnp.float8_e4m3fn))

    rw = jnp.pad(router_w[0], ((0, 0), (0, LANES - N_EXPERTS)))
    rb = jnp.pad(row(router_b[0]), ((0, 0), (0, LANES - N_EXPERTS)))
    rw_hi = rw.astype(BF16)
    rw_lo = (rw - rw_hi.astype(F32)).astype(BF16)
    tt = TOKEN_TILE
    n_tiles = n // tt
    before = jnp.triu(jnp.ones((tt, tt), BF16), 1)
    below = jnp.tril(jnp.ones((N_EXPERTS, N_EXPERTS), BF16), -1)
    x1, h2, gates_t, lrow_t, table = _post_mixer(
        ret, att, x, mod, lng, lnb, w_o[0].astype(BF16), row(ln1_g[0]), row(ln1_b[0]),
        rw_hi, rw_lo, rb, before, below)
    token_major = lambda v: v.transpose(0, 2, 1).reshape(n, TOP_K)
    gates, lrow = token_major(gates_t), token_major(lrow_t)

    tm = EXPERT_ROWS
    cnt, lstart, tbase, phi, rem = (table[:, :, k] for k in range(5))
    totals = jnp.max(tbase + cnt, axis=0)
    exp_rows = (totals + tm - 1) // tm * tm
    exp_end = jnp.cumsum(exp_rows)
    exp_start = exp_end - exp_rows
    dst = (exp_start[None, :] + tbase).reshape(-1)
    max_rows = n * TOP_K + N_EXPERTS * (SEG_ALIGN - 1) + N_EXPERTS * (tm - 1)
    n_blocks = -(-max_rows // tm)
    block_start = jnp.arange(n_blocks, dtype=jnp.int32) * tm
    block_expert = jnp.minimum(jnp.sum(exp_end[None, :] <= block_start[:, None], axis=1),
                               N_EXPERTS - 1).astype(jnp.int32)
    n_active = (exp_end[-1:] // tm).astype(jnp.int32)
    groups = lambda v: (v // SEG_ALIGN).astype(jnp.int32).reshape(-1)
    cnt_g, lstart_g, dst_g, total_g = groups(cnt), groups(lstart), groups(dst), groups(jnp.sum(cnt, axis=1))

    flat = lambda v: v.astype(jnp.int32).reshape(-1)
    ncopy = cnt // SEG_ALIGN - (rem > 0)
    real_rows = jnp.where(cnt > 0, cnt - phi - jnp.where(rem > 0, SEG_ALIGN - rem, 0), 0)
    real_totals = jnp.sum(real_rows, axis=0)
    flush = (real_totals % SEG_ALIGN) > 0
    flush_dst = (exp_start + real_totals // SEG_ALIGN * SEG_ALIGN) // SEG_ALIGN
    x_sorted = _dispatch(cnt_g, flat(ncopy), lstart_g, dst_g, flat(jnp.sum(ncopy, axis=1)), total_g, flat(rem),
                         flat(flush), flat(flush_dst), groups(exp_rows - totals), groups(exp_start + totals),
                         n_active, h2, lrow_t, n_blocks * tm)
    y_sorted = _experts(block_expert, n_active, x_sorted, w_up[0], b_up[0], w_down[0], b_down[0])
    out = _combine(cnt_g, lstart_g, dst_g, total_g, lrow, gates, x1, mod, row(ln2_g[0]), row(ln2_b[0]),
                   y_sorted, t)
    return out.reshape(b, t, d)
```
